```python
import math
import jax
import jax.numpy as jnp
from jax import lax
import numpy as np

D_MODEL = 2048
BATCH = 1
SEQ = 8192
DEPTH = 2

ATTN_HEAD_DIM = 128
ATTN_WIDTH = D_MODEL // 2
N_ATTN_HEADS = ATTN_WIDTH // ATTN_HEAD_DIM
POOL_WIDTH = D_MODEL // 4
POOL_WINDOWS = (2, 4, 8, 16)
POOL_GROUP = POOL_WIDTH // len(POOL_WINDOWS)
LRU_WIDTH = D_MODEL // 4
N_LRU_BLOCKS = 4
LRU_BLOCK = LRU_WIDTH // N_LRU_BLOCKS
LRU_C = 8.0
CONV_WIDTH = 4
N_IDX_HEADS = 16
IDX_DIM = 64
TOPK_MAX = 256
QUERY_BLOCK = 128
N_BUCKETS = 32
MAX_DISTANCE = 128
MLP_HIDDEN = 4 * D_MODEL
NORM_EPS = 1e-6
COLUMN_SIZES = (ATTN_WIDTH, ATTN_WIDTH, ATTN_WIDTH, N_IDX_HEADS * IDX_DIM, IDX_DIM,
                N_IDX_HEADS, POOL_WIDTH, LRU_WIDTH, LRU_WIDTH)
IN_WIDTH = 3 * ATTN_WIDTH + N_IDX_HEADS * IDX_DIM + IDX_DIM + N_IDX_HEADS + POOL_WIDTH + 2 * LRU_WIDTH

kernel_name = 'hybrid_dsa_pool_rglru_block'


def rms_norm(x, g):
    xf = x.astype(jnp.float32)
    y = xf * lax.rsqrt(jnp.mean(xf * xf, axis=-1, keepdims=True) + NORM_EPS)
    return (y * g.astype(jnp.float32)).astype(x.dtype)


def split_columns(proj):
    out = []
    start = 0
    for size in COLUMN_SIZES:
        out.append(proj[..., start:start + size])
        start += size
    return out


def t5_bucket(rel):
    max_exact = N_BUCKETS // 2
    rel = jnp.maximum(rel, 0)
    relf = jnp.maximum(rel.astype(jnp.float32), 1.0)
    large = max_exact + (jnp.log(relf / max_exact) / math.log(MAX_DISTANCE / max_exact)
                         * (N_BUCKETS - max_exact)).astype(jnp.int32)
    large = jnp.minimum(large, N_BUCKETS - 1)
    return jnp.where(rel < max_exact, rel, large)


def dsa_attention(q, k, v, q_idx, k_idx, w_idx, positions, rel_bias):
    B, L = q.shape[0], q.shape[1]
    top_k = min(TOPK_MAX, L // 4)
    qb = min(QUERY_BLOCK, L)
    n_blocks = L // qb
    scale = ATTN_HEAD_DIM ** -0.5
    idx_scale = (N_IDX_HEADS * IDX_DIM) ** -0.5
    k_idx_f = k_idx.astype(jnp.float32)
    gather = jax.vmap(lambda arr, ids: arr[ids])

    def block(start):
        qs = lax.dynamic_slice_in_dim(q, start, qb, axis=1)
        qi = lax.dynamic_slice_in_dim(q_idx, start, qb, axis=1)
        wi = lax.dynamic_slice_in_dim(w_idx, start, qb, axis=1)
        pq = lax.dynamic_slice_in_dim(positions, start, qb, axis=1)
        dots = jnp.einsum('bqhd,bsd->bqhs', qi.astype(jnp.float32), k_idx_f)
        score = jnp.einsum('bqh,bqhs->bqs', wi.astype(jnp.float32), jax.nn.relu(dots)) * idx_scale
        admissible = positions[:, None, :] <= pq[:, :, None]
        score = jnp.where(admissible, score, -jnp.inf)
        _, sel = lax.top_k(score, top_k)
        valid = jnp.take_along_axis(admissible, sel, axis=-1)
        k_sel = gather(k, sel)
        v_sel = gather(v, sel)
        pos_sel = gather(positions, sel)
        bias = rel_bias[t5_bucket(pq[:, :, None] - pos_sel)]
        logits = (jnp.einsum('bqhd,bqkhd->bhqk', qs, k_sel).astype(jnp.float32) * scale
                  + jnp.transpose(bias.astype(jnp.float32), (0, 3, 1, 2)))
        logits = jnp.where(valid[:, None, :, :], logits, -jnp.inf)
        p = jax.nn.softmax(logits, axis=-1).astype(v.dtype)
        out = jnp.einsum('bhqk,bqkhd->bqhd', p, v_sel)
        return out.reshape(B, qb, -1)

    outs = lax.map(block, jnp.arange(n_blocks, dtype=jnp.int32) * qb)
    return jnp.transpose(outs, (1, 0, 2, 3)).reshape(B, L, -1)


def pool_mixer(u, pool_w, pool_scale):
    B, L, C = u.shape
    uf = u.astype(jnp.float32)
    cs = jnp.concatenate([jnp.zeros((B, 1, C), jnp.float32), jnp.cumsum(uf, axis=1)], axis=1)
    t = jnp.arange(L)
    diffs = []
    for g, win in enumerate(POOL_WINDOWS):
        sl = slice(g * POOL_GROUP, (g + 1) * POOL_GROUP)
        csg = cs[:, :, sl]
        lo = jnp.maximum(t + 1 - win, 0)
        count = jnp.minimum(t + 1, win).astype(jnp.float32)
        diffs.append((csg[:, 1:] - csg[:, lo]) / count[None, :, None] - uf[:, :, sl])
    d = jnp.stack(diffs, axis=2)
    y = jnp.einsum('blgc,gcd->blgd', d, pool_w.astype(jnp.float32)).reshape(B, L, C)
    return (y * pool_scale.astype(jnp.float32)).astype(u.dtype)


def rglru_mixer(u, gate, conv_w, conv_b, r_w, r_b, i_w, i_b, lam):
    B, L, C = u.shape
    xc = lax.conv_general_dilated(u, conv_w[:, None, :], window_strides=(1,),
                                  padding=[(CONV_WIDTH - 1, 0)],
                                  dimension_numbers=('NWC', 'WIO', 'NWC'),
                                  feature_group_count=C) + conv_b
    xb = xc.reshape(B, L, N_LRU_BLOCKS, LRU_BLOCK)
    r = jax.nn.sigmoid(jnp.einsum('blgc,gcd->blgd', xb, r_w).reshape(B, L, C) + r_b)
    i = jax.nn.sigmoid(jnp.einsum('blgc,gcd->blgd', xb, i_w).reshape(B, L, C) + i_b)
    log_a = -LRU_C * r.astype(jnp.float32) * jax.nn.softplus(-lam.astype(jnp.float32))
    a = jnp.exp(log_a)
    b = jnp.sqrt(-jnp.expm1(2.0 * log_a)) * (i * xc).astype(jnp.float32)

    def combine(left, right):
        a1, b1 = left
        a2, b2 = right
        return a1 * a2, a2 * b1 + b2

    _, h = lax.associative_scan(combine, (a, b), axis=1)
    return (h * jax.nn.gelu(gate.astype(jnp.float32))).astype(u.dtype)


def setup_inputs(seed: int = 0) -> dict:
    key = jax.random.key(seed)
    ks = jax.random.split(key, 28)
    f32 = jnp.float32
    D = D_MODEL

    def nrm(k, shape, s):
        return jax.random.normal(k, shape, f32) * s

    def gain(k, n):
        return 1.0 + nrm(k, (DEPTH, n), 0.05)

    x = nrm(ks[0], (BATCH, SEQ, D), 1.0)
    c = nrm(ks[1], (BATCH, D), 1.0)
    offset = jax.random.randint(ks[2], (BATCH, 1), 0, 4096, dtype=jnp.int32)
    positions = offset + jnp.arange(SEQ, dtype=jnp.int32)[None, :]
    rel_bias = nrm(ks[3], (N_BUCKETS, N_ATTN_HEADS), 0.5)
    w_mod = nrm(ks[4], (DEPTH, D, 6 * D), 0.5 * D ** -0.5)
    b_mod = nrm(ks[5], (DEPTH, 6 * D), 0.01)
    g_pre_mix = gain(ks[6], D)
    g_post_mix = gain(ks[7], D)
    g_pre_mlp = gain(ks[8], D)
    g_post_mlp = gain(ks[9], D)
    w_in = nrm(ks[10], (DEPTH, D, IN_WIDTH), D ** -0.5)
    pool_w = nrm(ks[11], (DEPTH, len(POOL_WINDOWS), POOL_GROUP, POOL_GROUP), POOL_GROUP ** -0.5)
    pool_scale = 1.0 + nrm(ks[12], (DEPTH, POOL_WIDTH), 0.1)
    conv_w = nrm(ks[13], (DEPTH, CONV_WIDTH, LRU_WIDTH), CONV_WIDTH ** -0.5)
    conv_b = nrm(ks[14], (DEPTH, LRU_WIDTH), 0.01)
    gate_r_w = nrm(ks[15], (DEPTH, N_LRU_BLOCKS, LRU_BLOCK, LRU_BLOCK), LRU_BLOCK ** -0.5)
    gate_r_b = nrm(ks[16], (DEPTH, LRU_WIDTH), 0.1)
    gate_i_w = nrm(ks[17], (DEPTH, N_LRU_BLOCKS, LRU_BLOCK, LRU_BLOCK), LRU_BLOCK ** -0.5)
    gate_i_b = nrm(ks[18], (DEPTH, LRU_WIDTH), 0.1)
    a_c = jax.random.uniform(ks[19], (DEPTH, LRU_WIDTH), f32, 0.9, 0.999)
    s = a_c ** (1.0 / LRU_C)
    lru_lambda = jnp.log(s) - jnp.log1p(-s)
    g_attn_out = gain(ks[20], ATTN_WIDTH)
    g_pool_out = gain(ks[21], POOL_WIDTH)
    g_lru_out = gain(ks[22], LRU_WIDTH)
    w_out = nrm(ks[23], (DEPTH, D, D), D ** -0.5)
    w_mlp_up = nrm(ks[24], (DEPTH, D, MLP_HIDDEN), D ** -0.5)
    w_mlp_down = nrm(ks[25], (DEPTH, MLP_HIDDEN, D), MLP_HIDDEN ** -0.5)
    return {'x': x, 'c': c, 'positions': positions, 'rel_bias': rel_bias,
            'w_mod': w_mod, 'b_mod': b_mod,
            'g_pre_mix': g_pre_mix, 'g_post_mix': g_post_mix,
            'g_pre_mlp': g_pre_mlp, 'g_post_mlp': g_post_mlp,
            'w_in': w_in, 'pool_w': pool_w, 'pool_scale': pool_scale,
            'conv_w': conv_w, 'conv_b': conv_b,
            'gate_r_w': gate_r_w, 'gate_r_b': gate_r_b,
            'gate_i_w': gate_i_w, 'gate_i_b': gate_i_b, 'lru_lambda': lru_lambda,
            'g_attn_out': g_attn_out, 'g_pool_out': g_pool_out, 'g_lru_out': g_lru_out,
            'w_out': w_out, 'w_mlp_up': w_mlp_up, 'w_mlp_down': w_mlp_down}


def reference(x, c, positions, rel_bias, w_mod, b_mod, g_pre_mix, g_post_mix, g_pre_mlp,
              g_post_mlp, w_in, pool_w, pool_scale, conv_w, conv_b, gate_r_w, gate_r_b,
              gate_i_w, gate_i_b, lru_lambda, g_attn_out, g_pool_out, g_lru_out, w_out,
              w_mlp_up, w_mlp_down):
    B, L, _ = x.shape
    c_act = jax.nn.silu(c)
    for l in range(DEPTH):
        mod = c_act @ w_mod[l] + b_mod[l]
        shift1, scale1, gate1, shift2, scale2, gate2 = [m[:, None, :] for m in jnp.split(mod, 6, axis=-1)]

        h = rms_norm(x, g_pre_mix[l]) * (1.0 + scale1) + shift1
        proj = h @ w_in[l]
        q, k, v, q_idx, k_idx, w_idx, pool_in, lru_in, lru_gate = split_columns(proj)
        attn = dsa_attention(q.reshape(B, L, N_ATTN_HEADS, ATTN_HEAD_DIM),
                             k.reshape(B, L, N_ATTN_HEADS, ATTN_HEAD_DIM),
                             v.reshape(B, L, N_ATTN_HEADS, ATTN_HEAD_DIM),
                             q_idx.reshape(B, L, N_IDX_HEADS, IDX_DIM), k_idx, w_idx,
                             positions, rel_bias)
        pool = pool_mixer(pool_in, pool_w[l], pool_scale[l])
        lru = rglru_mixer(lru_in, lru_gate, conv_w[l], conv_b[l], gate_r_w[l], gate_r_b[l],
                          gate_i_w[l], gate_i_b[l], lru_lambda[l])
        mixed = jnp.concatenate([rms_norm(attn, g_attn_out[l]),
                                 rms_norm(pool, g_pool_out[l]),
                                 rms_norm(lru, g_lru_out[l])], axis=-1)
        y = mixed @ w_out[l]
        x = x + gate1 * rms_norm(y, g_post_mix[l])

        h = rms_norm(x, g_pre_mlp[l]) * (1.0 + scale2) + shift2
        u = jax.nn.relu(h @ w_mlp_up[l])
        y = (u * u) @ w_mlp_down[l]
        x = x + gate2 * rms_norm(y, g_post_mlp[l])
    return x
```

```python
import functools
import math

import jax
import jax.numpy as jnp
from jax import lax
from jax.experimental import pallas as pl
from jax.experimental.pallas import tpu as pltpu

D_MODEL = 2048
SEQ = 8192
DEPTH = 2
ATTN_HEAD_DIM = 128
ATTN_WIDTH = 1024
N_ATTN_HEADS = 8
POOL_WIDTH = 512
POOL_WINDOWS = (2, 4, 8, 16)
POOL_GROUP = 128
LRU_WIDTH = 512
N_LRU_BLOCKS = 4
LRU_BLOCK = 128
LRU_C = 8.0
CONV_WIDTH = 4
N_IDX_HEADS = 16
IDX_DIM = 64
TOP_K = 256
N_BUCKETS = 32
MAX_DISTANCE = 128
MLP_HIDDEN = 4 * D_MODEL
NORM_EPS = 1e-6

AUX_POOL = 0
AUX_LRU_IN = 512
AUX_LRU_GATE = 1024
AUX_KIDX = 1536
AUX_WIDTH = 1664
LANE = 128

ATT_TQ = 256
ATT_TK = 256
MASK_NEG = -1e30
VMEM_LIMIT = 56 * 1024 * 1024

F32 = jnp.float32
BF16 = jnp.bfloat16
_NT = (((1,), (1,)), ((), ()))


def _cparams(*sem):
    return pltpu.CompilerParams(dimension_semantics=sem, vmem_limit_bytes=VMEM_LIMIT)


def _rms(x, g):
    ms = jnp.mean(x * x, axis=-1, keepdims=True)
    return (x * lax.rsqrt(ms + NORM_EPS)) * g


def _mod_kernel(c_ref, w_ref, b_ref, o_ref):
    c = c_ref[...]
    c_act = c * jax.nn.sigmoid(c)
    o_ref[0] = jnp.sum(c_act * w_ref[0], axis=0, keepdims=True) + b_ref[0]


def _modulation(c, w_mod, b_mod):
    tn = 1024
    n = w_mod.shape[-1]
    return pl.pallas_call(
        _mod_kernel,
        grid=(DEPTH, n // tn),
        in_specs=[
            pl.BlockSpec((D_MODEL, 1), lambda l, j: (0, 0)),
            pl.BlockSpec((1, D_MODEL, tn), lambda l, j: (l, 0, j)),
            pl.BlockSpec((1, 1, tn), lambda l, j: (l, 0, j)),
        ],
        out_specs=pl.BlockSpec((1, 1, tn), lambda l, j: (l, 0, j)),
        out_shape=jax.ShapeDtypeStruct((DEPTH, 1, n), F32),
        compiler_params=_cparams("parallel", "parallel"),
        name="modulation",
    )(c.reshape(D_MODEL, 1), w_mod, b_mod.reshape(DEPTH, 1, n))


def _adaln_mm_kernel(x_ref, g_ref, sc_ref, sh_ref, w_ref, o_ref, h_ref, *, tm, relu2):
    rc = 64

    @pl.when(pl.program_id(1) == 0)
    def _():
        g = g_ref[...]
        sc1 = 1.0 + sc_ref[...]
        sh = sh_ref[...]

        def body(r, carry):
            rows = pl.ds(pl.multiple_of(r * rc, rc), rc)
            h = _rms(x_ref[rows, :], g) * sc1 + sh
            h_ref[rows, :] = h.astype(BF16)
            return carry

        lax.fori_loop(0, tm // rc, body, 0)

    acc = jnp.dot(h_ref[...], w_ref[...], preferred_element_type=F32)
    if relu2:
        acc = jnp.maximum(acc, 0.0)
        acc = acc * acc
    o_ref[...] = acc.astype(o_ref.dtype)


def _adaln_matmul(x, g, scale, shift, w, out_dtype, *, tm, tn, relu2=False, name):
    m, k = x.shape
    n = w.shape[1]
    vec = pl.BlockSpec((1, k), lambda i, j: (0, 0))
    return pl.pallas_call(
        functools.partial(_adaln_mm_kernel, tm=tm, relu2=relu2),
        grid=(m // tm, n // tn),
        in_specs=[
            pl.BlockSpec((tm, k), lambda i, j: (i, 0)),
            vec, vec, vec,
            pl.BlockSpec((k, tn), lambda i, j: (0, j)),
        ],
        out_specs=pl.BlockSpec((tm, tn), lambda i, j: (i, j)),
        out_shape=jax.ShapeDtypeStruct((m, n), out_dtype),
        scratch_shapes=[pltpu.VMEM((tm, k), BF16)],
        compiler_params=_cparams("parallel", "arbitrary"),
        name=name,
    )(x, g, scale, shift, w)


def _band_kernel(rb_ref, o_ref):
    d = pl.program_id(0)
    t = lax.broadcasted_iota(jnp.int32, (ATT_TQ, ATT_TK), 0)
    s = lax.broadcasted_iota(jnp.int32, (ATT_TQ, ATT_TK), 1)
    rel = jnp.maximum(d * ATT_TQ + t - s, 0)
    max_exact = N_BUCKETS // 2
    relf = jnp.maximum(rel.astype(F32), 1.0)
    large = max_exact + (jnp.log(relf / max_exact) / math.log(MAX_DISTANCE / max_exact)
                         * (N_BUCKETS - max_exact)).astype(jnp.int32)
    large = jnp.minimum(large, N_BUCKETS - 1)
    bucket = jnp.where(rel < max_exact, rel, large)
    for h in range(N_ATTN_HEADS):
        val = jnp.zeros((ATT_TQ, ATT_TK), F32)
        for b in range(N_BUCKETS):
            val = jnp.where(bucket == b, rb_ref[b, h], val)
        o_ref[0, h] = val


def _bias_band(rel_bias):
    return pl.pallas_call(
        _band_kernel,
        grid=(3,),
        in_specs=[pl.BlockSpec(memory_space=pltpu.SMEM)],
        out_specs=pl.BlockSpec((1, N_ATTN_HEADS, ATT_TQ, ATT_TK), lambda d: (d, 0, 0, 0)),
        out_shape=jax.ShapeDtypeStruct((3, N_ATTN_HEADS, ATT_TQ, ATT_TK), F32),
        compiler_params=_cparams("parallel"),
        name="bias_band",
    )(rel_bias)


def _key_to_float(u):
    key = jnp.bitwise_xor(u, jnp.int32(-2 ** 31))
    bits = jnp.where(key >= 0, key, jnp.bitwise_xor(key, jnp.int32(0x7FFFFFFF)))
    return lax.bitcast_convert_type(bits, F32)


def _select_kernel(qi_ref, kw_ref, o_ref, kb_ref, wb_ref, sc_ref):
    i = pl.program_id(0)
    tq, tk = ATT_TQ, ATT_TK
    nk = SEQ // tk
    idx_scale = (N_IDX_HEADS * IDX_DIM) ** -0.5

    @pl.when(i == 0)
    def _():
        kb_ref[...] = kw_ref[:, 0:IDX_DIM].astype(BF16)

    w = kw_ref[pl.ds(pl.multiple_of(i * tq, tq), tq), :]
    for h in range(N_IDX_HEADS):
        wb_ref[h] = jnp.broadcast_to(w[:, IDX_DIM + h:IDX_DIM + h + 1], (tq, tk))

    nchunks = i + 1
    row = i * tq + lax.broadcasted_iota(jnp.int32, (tq, tk), 0)
    col0 = lax.broadcasted_iota(jnp.int32, (tq, tk), 1)

    def score_body(j, carry):
        kc = kb_ref[pl.ds(pl.multiple_of(j * tk, tk), tk), :]
        acc = jnp.zeros((tq, tk), F32)
        for h in range(N_IDX_HEADS):
            d = lax.dot_general(qi_ref[:, h * IDX_DIM:(h + 1) * IDX_DIM], kc, _NT,
                                preferred_element_type=F32)
            acc = acc + wb_ref[h] * jnp.maximum(d, 0.0)
        score = acc * idx_scale
        sc_ref[j] = jnp.where(col0 + j * tk <= row, score, -jnp.inf)
        return carry

    lax.fori_loop(0, nchunks, score_body, 0)

    def count_ge(cand):
        cand_b = jnp.broadcast_to(cand, (tq, tk))

        def body(j, acc):
            hit = jnp.where(sc_ref[j] >= cand_b, 1.0, 0.0)
            for c in range(tk // LANE):
                acc = acc + hit[:, c * LANE:(c + 1) * LANE]
            return acc

        acc = lax.fori_loop(0, nchunks, body, jnp.zeros((tq, LANE), F32))
        return jnp.sum(acc, axis=1, keepdims=True)

    lo0 = jnp.full((tq, 1), 0x007FFFFF, jnp.int32)
    hi0 = jnp.full((tq, 1), -8388608, jnp.int32)

    def bis_body(_, carry):
        lo, hi = carry
        mid = lo + lax.shift_right_logical(hi - lo + 1, jnp.int32(1))
        ok = count_ge(_key_to_float(mid)) >= float(TOP_K)
        return jnp.where(ok, mid, lo), jnp.where(ok, hi, mid - 1)

    lo, _ = lax.fori_loop(0, 32, bis_body, (lo0, hi0))
    thr = jnp.broadcast_to(_key_to_float(lo), (tq, tk))

    for j in range(nk):
        @pl.when(j < nchunks)
        def _():
            keep = jnp.logical_and(sc_ref[j] >= thr, col0 + j * tk <= row)
            o_ref[j] = jnp.where(keep, 0.0, MASK_NEG).astype(BF16)

        @pl.when(j >= nchunks)
        def _():
            o_ref[j] = jnp.full((tq, tk), MASK_NEG, BF16)


def _select(attn_grp, aux):
    tq, tk = ATT_TQ, ATT_TK
    nq, nk = SEQ // tq, SEQ // tk
    return pl.pallas_call(
        _select_kernel,
        grid=(nq,),
        in_specs=[
            pl.BlockSpec((tq, N_IDX_HEADS * IDX_DIM), lambda i: (i, 3)),
            pl.BlockSpec((SEQ, LANE), lambda i: (0, AUX_KIDX // LANE)),
        ],
        out_specs=pl.BlockSpec((None, nk, tq, tk), lambda i: (i, 0, 0, 0)),
        out_shape=jax.ShapeDtypeStruct((nq, nk, tq, tk), BF16),
        scratch_shapes=[
            pltpu.VMEM((SEQ, IDX_DIM), BF16),
            pltpu.VMEM((N_IDX_HEADS, tq, tk), F32),
            pltpu.VMEM((nk, tq, tk), F32),
        ],
        compiler_params=_cparams("arbitrary"),
        name="select",
    )(attn_grp, aux)


def _attn_kernel(q_ref, k_ref, v_ref, mb_ref, band_ref, o_ref, m_ref, l_ref, acc_ref):
    i = pl.program_id(0)
    j = pl.program_id(1)
    scale = ATTN_HEAD_DIM ** -0.5
    dh = ATTN_HEAD_DIM

    @pl.when(j == 0)
    def _():
        m_ref[...] = jnp.full(m_ref.shape, MASK_NEG, F32)
        l_ref[...] = jnp.zeros(l_ref.shape, F32)
        acc_ref[...] = jnp.zeros(acc_ref.shape, F32)

    @pl.when(j <= i)
    def _():
        sel = jnp.minimum(i - j, 2)
        mb = mb_ref[...].astype(F32)
        for h in range(N_ATTN_HEADS):
            hs = slice(h * dh, (h + 1) * dh)
            s = lax.dot_general(q_ref[:, hs], k_ref[:, hs], _NT, preferred_element_type=F32)
            s = s * scale + band_ref[sel, h] + mb
            m_prev = m_ref[h]
            m_new = jnp.maximum(m_prev, jnp.max(s, axis=1, keepdims=True))
            alpha = jnp.exp(m_prev - m_new)
            p = jnp.exp(s - m_new)
            l_ref[h] = alpha * l_ref[h] + jnp.sum(p, axis=1, keepdims=True)
            acc_ref[:, hs] = alpha * acc_ref[:, hs] + jnp.dot(
                p.astype(BF16), v_ref[:, hs], preferred_element_type=F32)
            m_ref[h] = m_new

    @pl.when(j == i)
    def _():
        for h in range(N_ATTN_HEADS):
            hs = slice(h * dh, (h + 1) * dh)
            o_ref[:, hs] = acc_ref[:, hs] / l_ref[h]


def _attention(attn_grp, maskb, band):
    tq, tk = ATT_TQ, ATT_TK
    nq, nk = SEQ // tq, SEQ // tk
    return pl.pallas_call(
        _attn_kernel,
        grid=(nq, nk),
        in_specs=[
            pl.BlockSpec((tq, ATTN_WIDTH), lambda i, j: (i, 0)),
            pl.BlockSpec((tk, ATTN_WIDTH), lambda i, j: (jnp.minimum(j, i), 1)),
            pl.BlockSpec((tk, ATTN_WIDTH), lambda i, j: (jnp.minimum(j, i), 2)),
            pl.BlockSpec((None, None, tq, tk), lambda i, j: (i, jnp.minimum(j, i), 0, 0)),
            pl.BlockSpec((3, N_ATTN_HEADS, tq, tk), lambda i, j: (0, 0, 0, 0)),
        ],
        out_specs=pl.BlockSpec((tq, ATTN_WIDTH), lambda i, j: (i, 0)),
        out_shape=jax.ShapeDtypeStruct((SEQ, ATTN_WIDTH), F32),
        scratch_shapes=[
            pltpu.VMEM((N_ATTN_HEADS, tq, 1), F32),
            pltpu.VMEM((N_ATTN_HEADS, tq, 1), F32),
            pltpu.VMEM((tq, ATTN_WIDTH), F32),
        ],
        compiler_params=_cparams("parallel", "arbitrary"),
        name="attention",
    )(attn_grp, attn_grp, attn_grp, maskb, band)


POOL_HALO = 16
CONV_HALO = 8


def _block_diag_dot(x, w_ref):
    outs = []
    for g in range(w_ref.shape[0]):
        xs = x[:, g * LRU_BLOCK:(g + 1) * LRU_BLOCK].astype(BF16)
        outs.append(jnp.dot(xs, w_ref[g], preferred_element_type=F32))
    return jnp.concatenate(outs, axis=1)


def _pool_lru_kernel(pin_ref, lin_ref, gate_ref, pw_ref, ps_ref, cw_ref, cb_ref, rw_ref, rb_ref,
                     iw_ref, ib_ref, lam_ref, pool_o, lru_o, pprev_ref, lprev_ref, h_ref, *, T):
    i = pl.program_id(0)

    @pl.when(i == 0)
    def _():
        pprev_ref[...] = jnp.zeros(pprev_ref.shape, F32)
        lprev_ref[...] = jnp.zeros(lprev_ref.shape, F32)
        h_ref[...] = jnp.zeros(h_ref.shape, F32)

    t_glob = i * T + lax.broadcasted_iota(jnp.int32, (T, 1), 0)

    u = pin_ref[...]
    ext = jnp.concatenate([pprev_ref[...], u], axis=0)
    pprev_ref[...] = u[T - POOL_HALO:, :]
    p2 = ext[1:] + ext[:-1]
    p4 = p2[2:, POOL_GROUP:] + p2[:-2, POOL_GROUP:]
    p8 = p4[4:, POOL_GROUP:] + p4[:-4, POOL_GROUP:]
    p16 = p8[8:, POOL_GROUP:] + p8[:-8, POOL_GROUP:]
    wsums = (p2[15:15 + T, :POOL_GROUP], p4[13:13 + T, :POOL_GROUP],
             p8[9:9 + T, :POOL_GROUP], p16[1:1 + T, :])
    for g, win in enumerate(POOL_WINDOWS):
        gs = slice(g * POOL_GROUP, (g + 1) * POOL_GROUP)
        count = jnp.minimum(t_glob + 1, win).astype(F32)
        dlt = wsums[g] / count - u[:, gs]
        y = jnp.dot(dlt.astype(BF16), pw_ref[g], preferred_element_type=F32)
        pool_o[:, gs] = y * ps_ref[:, gs]

    x = lin_ref[...]
    lext = jnp.concatenate([lprev_ref[...], x], axis=0)
    lprev_ref[...] = x[T - CONV_HALO:, :]
    xc = cb_ref[...]
    for jj in range(CONV_WIDTH):
        off = CONV_HALO - (CONV_WIDTH - 1) + jj
        xc = xc + cw_ref[jj:jj + 1, :] * lext[off:off + T, :]
    r = jax.nn.sigmoid(_block_diag_dot(xc, rw_ref) + rb_ref[...])
    ig = jax.nn.sigmoid(_block_diag_dot(xc, iw_ref) + ib_ref[...])
    z = -lam_ref[...]
    softplus = jnp.maximum(z, 0.0) + jnp.log1p(jnp.exp(-jnp.abs(z)))
    log_a = (-LRU_C * r) * softplus
    a = jnp.exp(log_a)
    b = jnp.sqrt(-jnp.tanh(log_a) * (a * a + 1.0)) * (ig * xc)

    sh = 1
    while sh < T:
        a_sh = jnp.concatenate([jnp.ones((sh, LRU_WIDTH), F32), a[:T - sh]], axis=0)
        b_sh = jnp.concatenate([jnp.zeros((sh, LRU_WIDTH), F32), b[:T - sh]], axis=0)
        b = a * b_sh + b
        a = a * a_sh
        sh *= 2
    h = a * h_ref[...] + b
    h_ref[...] = h[T - 1:, :]

    gt = gate_ref[...]
    cdf = 0.5 * (1.0 + jnp.tanh(math.sqrt(2.0 / math.pi) * (gt + 0.044715 * (gt * gt * gt))))
    lru_o[...] = h * (gt * cdf)


def _pool_lru(aux, pool_w, pool_scale, conv_w, conv_b, r_w, r_b, i_w, i_b, lam):
    T = 256
    full = lambda a: pl.BlockSpec(a.shape, lambda i: (0,) * a.ndim)
    row = lambda v: v.reshape(1, -1)
    args = (pool_w.astype(BF16), row(pool_scale), conv_w, row(conv_b), r_w.astype(BF16), row(r_b),
            i_w.astype(BF16), row(i_b), row(lam))
    return pl.pallas_call(
        functools.partial(_pool_lru_kernel, T=T),
        grid=(SEQ // T,),
        in_specs=[
            pl.BlockSpec((T, POOL_WIDTH), lambda i: (i, AUX_POOL // POOL_WIDTH)),
            pl.BlockSpec((T, LRU_WIDTH), lambda i: (i, AUX_LRU_IN // LRU_WIDTH)),
            pl.BlockSpec((T, LRU_WIDTH), lambda i: (i, AUX_LRU_GATE // LRU_WIDTH)),
        ] + [full(a) for a in args],
        out_specs=[pl.BlockSpec((T, POOL_WIDTH), lambda i: (i, 0)),
                   pl.BlockSpec((T, LRU_WIDTH), lambda i: (i, 0))],
        out_shape=[jax.ShapeDtypeStruct((SEQ, POOL_WIDTH), F32),
                   jax.ShapeDtypeStruct((SEQ, LRU_WIDTH), F32)],
        scratch_shapes=[pltpu.VMEM((POOL_HALO, POOL_WIDTH), F32),
                        pltpu.VMEM((CONV_HALO, LRU_WIDTH), F32),
                        pltpu.VMEM((1, LRU_WIDTH), F32)],
        compiler_params=_cparams("arbitrary"),
        name="pool_lru",
    )(aux, aux, aux, *args)


def _mix_out_kernel(a_ref, p_ref, l_ref, x_ref, ga_ref, gp_ref, gl_ref, w_ref, gpost_ref, gate_ref,
                    o_ref):
    na = _rms(a_ref[...], ga_ref[...]).astype(BF16)
    npool = _rms(p_ref[...], gp_ref[...]).astype(BF16)
    nl = _rms(l_ref[...], gl_ref[...]).astype(BF16)
    p0, p1 = ATTN_WIDTH, ATTN_WIDTH + POOL_WIDTH
    y = jnp.dot(na, w_ref[0:p0, :], preferred_element_type=F32)
    y = y + jnp.dot(npool, w_ref[p0:p1, :], preferred_element_type=F32)
    y = y + jnp.dot(nl, w_ref[p1:, :], preferred_element_type=F32)
    o_ref[...] = x_ref[...] + gate_ref[...] * _rms(y, gpost_ref[...])


def _mix_out(attn, pool, lru, x, g_attn, g_pool, g_lru, w_out, g_post, gate):
    tm = 256
    vec = lambda n: pl.BlockSpec((1, n), lambda i: (0, 0))
    rows = lambda n: pl.BlockSpec((tm, n), lambda i: (i, 0))
    return pl.pallas_call(
        _mix_out_kernel,
        grid=(SEQ // tm,),
        in_specs=[rows(ATTN_WIDTH), rows(POOL_WIDTH), rows(LRU_WIDTH), rows(D_MODEL),
                  vec(ATTN_WIDTH), vec(POOL_WIDTH), vec(LRU_WIDTH),
                  pl.BlockSpec((D_MODEL, D_MODEL), lambda i: (0, 0)),
                  vec(D_MODEL), vec(D_MODEL)],
        out_specs=rows(D_MODEL),
        out_shape=jax.ShapeDtypeStruct((SEQ, D_MODEL), F32),
        compiler_params=_cparams("parallel"),
        name="mix_out",
    )(attn, pool, lru, x, g_attn, g_pool, g_lru, w_out, g_post, gate)


def _mlp_down_kernel(u_ref, w_ref, x_ref, gpost_ref, gate_ref, o_ref, acc_ref):
    k = pl.program_id(1)

    @pl.when(k == 0)
    def _():
        acc_ref[...] = jnp.zeros(acc_ref.shape, F32)

    acc_ref[...] += jnp.dot(u_ref[...], w_ref[...], preferred_element_type=F32)

    @pl.when(k == pl.num_programs(1) - 1)
    def _():
        o_ref[...] = x_ref[...] + gate_ref[...] * _rms(acc_ref[...], gpost_ref[...])


def _mlp_down(u2, w_down, x, g_post, gate):
    tm, tk = 512, 1024
    vec = pl.BlockSpec((1, D_MODEL), lambda i, k: (0, 0))
    return pl.pallas_call(
        _mlp_down_kernel,
        grid=(SEQ // tm, MLP_HIDDEN // tk),
        in_specs=[pl.BlockSpec((tm, tk), lambda i, k: (i, k)),
                  pl.BlockSpec((tk, D_MODEL), lambda i, k: (k, 0)),
                  pl.BlockSpec((tm, D_MODEL), lambda i, k: (i, 0)),
                  vec, vec],
        out_specs=pl.BlockSpec((tm, D_MODEL), lambda i, k: (i, 0)),
        out_shape=jax.ShapeDtypeStruct((SEQ, D_MODEL), F32),
        scratch_shapes=[pltpu.VMEM((tm, D_MODEL), F32)],
        compiler_params=_cparams("parallel", "arbitrary"),
        name="mlp_down",
    )(u2, w_down, x, g_post, gate)


def kernel(x, c, positions, rel_bias, w_mod, b_mod, g_pre_mix, g_post_mix, g_pre_mlp, g_post_mlp, w_in, pool_w, pool_scale, conv_w, conv_b, gate_r_w, gate_r_b, gate_i_w, gate_i_b, lru_lambda, g_attn_out, g_pool_out, g_lru_out, w_out, w_mlp_up, w_mlp_down):
    assert x.shape == (1, SEQ, D_MODEL) and positions.shape == (1, SEQ)
    del positions
    xs = x[0]
    mod = _modulation(c, w_mod, b_mod)
    band = _bias_band(rel_bias)
    row = lambda v: v.reshape(1, -1)
    n_attn_cols = 4 * ATTN_WIDTH
    aux_lo = n_attn_cols + IDX_DIM + N_IDX_HEADS
    for l in range(DEPTH):
        shift1, scale1, gate1, shift2, scale2, gate2 = [
            mod[l, :, n * D_MODEL:(n + 1) * D_MODEL] for n in range(6)]
        w_attn = w_in[l, :, :n_attn_cols].astype(BF16)
        w_aux = jnp.concatenate(
            [w_in[l, :, aux_lo:], w_in[l, :, n_attn_cols:aux_lo],
             jnp.zeros((D_MODEL, AUX_WIDTH - (AUX_KIDX + IDX_DIM + N_IDX_HEADS)), F32)],
            axis=1).astype(BF16)

        g_pre = row(g_pre_mix[l])
        attn_grp = _adaln_matmul(xs, g_pre, scale1, shift1, w_attn, BF16,
                                 tm=512, tn=512, name="proj_attn")
        aux = _adaln_matmul(xs, g_pre, scale1, shift1, w_aux, F32,
                            tm=512, tn=AUX_WIDTH, name="proj_aux")
        maskb = _select(attn_grp, aux)
        attn = _attention(attn_grp, maskb, band)
        pool, lru = _pool_lru(aux, pool_w[l], pool_scale[l], conv_w[l], conv_b[l], gate_r_w[l],
                              gate_r_b[l], gate_i_w[l], gate_i_b[l], lru_lambda[l])
        xs = _mix_out(attn, pool, lru, xs, row(g_attn_out[l]), row(g_pool_out[l]),
                      row(g_lru_out[l]), w_out[l].astype(BF16), row(g_post_mix[l]), gate1)

        u2 = _adaln_matmul(xs, row(g_pre_mlp[l]), scale2, shift2, w_mlp_up[l].astype(BF16), BF16,
                           tm=512, tn=512, relu2=True, name="mlp_up")
        xs = _mlp_down(u2, w_mlp_down[l].astype(BF16), xs, row(g_post_mlp[l]), gate2)
    return xs[None]
```

```python
import functools
import math

import jax
import jax.numpy as jnp
from jax import lax
from jax.experimental import pallas as pl
from jax.experimental.pallas import tpu as pltpu

D_MODEL = 2048
SEQ = 8192
DEPTH = 2
ATTN_HEAD_DIM = 128
ATTN_WIDTH = 1024
N_ATTN_HEADS = 8
POOL_WIDTH = 512
POOL_WINDOWS = (2, 4, 8, 16)
POOL_GROUP = 128
LRU_WIDTH = 512
N_LRU_BLOCKS = 4
LRU_BLOCK = 128
LRU_C = 8.0
CONV_WIDTH = 4
N_IDX_HEADS = 16
IDX_DIM = 64
TOP_K = 256
N_BUCKETS = 32
MAX_DISTANCE = 128
MLP_HIDDEN = 4 * D_MODEL
NORM_EPS = 1e-6

AUX_POOL = 0
AUX_LRU_IN = 512
AUX_LRU_GATE = 1024
AUX_KIDX = 1536
AUX_WIDTH = 1664
LANE = 128

ATT_TQ = 256
ATT_TK = 256
ATT_UNROLL = 8
MASK_NEG = -1e30
LOG2E = math.log2(math.e)
VMEM_LIMIT = 56 * 1024 * 1024

F32 = jnp.float32
BF16 = jnp.bfloat16
_NT = (((1,), (1,)), ((), ()))


def _cparams(*sem):
    return pltpu.CompilerParams(dimension_semantics=sem, vmem_limit_bytes=VMEM_LIMIT)


def _rms(x, g):
    ms = jnp.mean(x * x, axis=-1, keepdims=True)
    return (x * lax.rsqrt(ms + NORM_EPS)) * g


def _mod_kernel(c_ref, w_ref, b_ref, o_ref):
    c = c_ref[...]
    c_act = c * jax.nn.sigmoid(c)
    o_ref[0] = jnp.sum(c_act * w_ref[0], axis=0, keepdims=True) + b_ref[0]


def _modulation(c, w_mod, b_mod):
    tn = 1024
    n = w_mod.shape[-1]
    return pl.pallas_call(
        _mod_kernel,
        grid=(DEPTH, n // tn),
        in_specs=[
            pl.BlockSpec((D_MODEL, 1), lambda l, j: (0, 0)),
            pl.BlockSpec((1, D_MODEL, tn), lambda l, j: (l, 0, j)),
            pl.BlockSpec((1, 1, tn), lambda l, j: (l, 0, j)),
        ],
        out_specs=pl.BlockSpec((1, 1, tn), lambda l, j: (l, 0, j)),
        out_shape=jax.ShapeDtypeStruct((DEPTH, 1, n), F32),
        compiler_params=_cparams("parallel", "parallel"),
        name="modulation",
    )(c.reshape(D_MODEL, 1), w_mod, b_mod.reshape(DEPTH, 1, n))


def _adaln_mm_kernel(x_ref, g_ref, sc_ref, sh_ref, w_ref, o_ref, h_ref, *, tm, relu2):
    rc = 64

    @pl.when(pl.program_id(1) == 0)
    def _():
        g = g_ref[...]
        sc1 = 1.0 + sc_ref[...]
        sh = sh_ref[...]

        def body(r, carry):
            rows = pl.ds(pl.multiple_of(r * rc, rc), rc)
            h = _rms(x_ref[rows, :], g) * sc1 + sh
            h_ref[rows, :] = h.astype(BF16)
            return carry

        lax.fori_loop(0, tm // rc, body, 0)

    acc = jnp.dot(h_ref[...], w_ref[...], preferred_element_type=F32)
    if relu2:
        acc = jnp.maximum(acc, 0.0)
        acc = acc * acc
    o_ref[...] = acc.astype(o_ref.dtype)


def _adaln_matmul(x, g, scale, shift, w, out_dtype, *, tm, tn, relu2=False, name):
    m, k = x.shape
    n = w.shape[1]
    vec = pl.BlockSpec((1, k), lambda i, j: (0, 0))
    return pl.pallas_call(
        functools.partial(_adaln_mm_kernel, tm=tm, relu2=relu2),
        grid=(m // tm, n // tn),
        in_specs=[
            pl.BlockSpec((tm, k), lambda i, j: (i, 0)),
            vec, vec, vec,
            pl.BlockSpec((k, tn), lambda i, j: (0, j)),
        ],
        out_specs=pl.BlockSpec((tm, tn), lambda i, j: (i, j)),
        out_shape=jax.ShapeDtypeStruct((m, n), out_dtype),
        scratch_shapes=[pltpu.VMEM((tm, k), BF16)],
        compiler_params=_cparams("parallel", "arbitrary"),
        name=name,
    )(x, g, scale, shift, w)


def _band_kernel(rb_ref, o_ref):
    d = pl.program_id(0)
    t = lax.broadcasted_iota(jnp.int32, (ATT_TQ, ATT_TK), 0)
    s = lax.broadcasted_iota(jnp.int32, (ATT_TQ, ATT_TK), 1)
    rel = jnp.maximum(d * ATT_TQ + t - s, 0)
    max_exact = N_BUCKETS // 2
    relf = jnp.maximum(rel.astype(F32), 1.0)
    large = max_exact + (jnp.log(relf / max_exact) / math.log(MAX_DISTANCE / max_exact)
                         * (N_BUCKETS - max_exact)).astype(jnp.int32)
    large = jnp.minimum(large, N_BUCKETS - 1)
    bucket = jnp.where(rel < max_exact, rel, large)
    for h in range(N_ATTN_HEADS):
        val = jnp.zeros((ATT_TQ, ATT_TK), F32)
        for b in range(N_BUCKETS):
            val = jnp.where(bucket == b, rb_ref[b, h], val)
        o_ref[0, h] = val * LOG2E


def _bias_band(rel_bias):
    return pl.pallas_call(
        _band_kernel,
        grid=(3,),
        in_specs=[pl.BlockSpec(memory_space=pltpu.SMEM)],
        out_specs=pl.BlockSpec((1, N_ATTN_HEADS, ATT_TQ, ATT_TK), lambda d: (d, 0, 0, 0)),
        out_shape=jax.ShapeDtypeStruct((3, N_ATTN_HEADS, ATT_TQ, ATT_TK), F32),
        compiler_params=_cparams("parallel"),
        name="bias_band",
    )(rel_bias)


def _key_to_float(u):
    key = jnp.bitwise_xor(u, jnp.int32(-2 ** 31))
    bits = jnp.where(key >= 0, key, jnp.bitwise_xor(key, jnp.int32(0x7FFFFFFF)))
    return lax.bitcast_convert_type(bits, F32)


def _select_kernel(qi_ref, kw_ref, o_ref, kb_ref, wb_ref, sc_ref):
    i = pl.program_id(0)
    tq, tk = ATT_TQ, ATT_TK
    nk = SEQ // tk
    idx_scale = (N_IDX_HEADS * IDX_DIM) ** -0.5

    @pl.when(i == 0)
    def _():
        kb_ref[...] = kw_ref[:, 0:IDX_DIM].astype(BF16)

    w = kw_ref[pl.ds(pl.multiple_of(i * tq, tq), tq), :]
    for h in range(N_IDX_HEADS):
        wb_ref[h] = jnp.broadcast_to(w[:, IDX_DIM + h:IDX_DIM + h + 1], (tq, tk))

    nchunks = i + 1
    row = i * tq + lax.broadcasted_iota(jnp.int32, (tq, tk), 0)
    col0 = lax.broadcasted_iota(jnp.int32, (tq, tk), 1)

    def score_body(j, carry):
        kc = kb_ref[pl.ds(pl.multiple_of(j * tk, tk), tk), :]
        acc = jnp.zeros((tq, tk), F32)
        for h in range(N_IDX_HEADS):
            d = lax.dot_general(qi_ref[:, h * IDX_DIM:(h + 1) * IDX_DIM], kc, _NT,
                                preferred_element_type=F32)
            acc = acc + wb_ref[h] * jnp.maximum(d, 0.0)
        score = acc * idx_scale
        sc_ref[j] = jnp.where(col0 + j * tk <= row, score, -jnp.inf)
        return carry

    lax.fori_loop(0, nchunks, score_body, 0)

    def count_ge(cand):
        cand_b = jnp.broadcast_to(cand, (tq, tk))

        def body(j, acc):
            hit = jnp.where(sc_ref[j] >= cand_b, 1.0, 0.0)
            for c in range(tk // LANE):
                acc = acc + hit[:, c * LANE:(c + 1) * LANE]
            return acc

        acc = lax.fori_loop(0, nchunks, body, jnp.zeros((tq, LANE), F32))
        return jnp.sum(acc, axis=1, keepdims=True)

    lo0 = jnp.full((tq, 1), 0x007FFFFF, jnp.int32)
    hi0 = jnp.full((tq, 1), -8388608, jnp.int32)

    def bis_body(_, carry):
        lo, hi = carry
        mid = lo + lax.shift_right_logical(hi - lo + 1, jnp.int32(1))
        ok = count_ge(_key_to_float(mid)) >= float(TOP_K)
        return jnp.where(ok, mid, lo), jnp.where(ok, hi, mid - 1)

    lo, _ = lax.fori_loop(0, 32, bis_body, (lo0, hi0))
    thr = jnp.broadcast_to(_key_to_float(lo), (tq, tk))

    def word_body(j, word):
        keep = jnp.logical_and(sc_ref[j] >= thr, col0 + j * tk <= row)
        return jnp.bitwise_or(word, jnp.where(keep, jnp.left_shift(jnp.int32(1), j), 0))

    o_ref[...] = lax.fori_loop(0, nchunks, word_body, jnp.zeros((tq, tk), jnp.int32))


def _select(attn_grp, aux):
    tq, tk = ATT_TQ, ATT_TK
    nq, nk = SEQ // tq, SEQ // tk
    return pl.pallas_call(
        _select_kernel,
        grid=(nq,),
        in_specs=[
            pl.BlockSpec((tq, N_IDX_HEADS * IDX_DIM), lambda i: (i, 3)),
            pl.BlockSpec((SEQ, LANE), lambda i: (0, AUX_KIDX // LANE)),
        ],
        out_specs=pl.BlockSpec((tq, tk), lambda i: (i, 0)),
        out_shape=jax.ShapeDtypeStruct((SEQ, tk), jnp.int32),
        scratch_shapes=[
            pltpu.VMEM((SEQ, IDX_DIM), BF16),
            pltpu.VMEM((N_IDX_HEADS, tq, tk), F32),
            pltpu.VMEM((nk, tq, tk), F32),
        ],
        compiler_params=_cparams("arbitrary"),
        name="select",
    )(attn_grp, aux)


def _attn_kernel(q_ref, k_ref, v_ref, w_ref, band_ref, o_ref, s_ref):
    i = pl.program_id(1)
    tq, tk = ATT_TQ, ATT_TK
    dh = ATTN_HEAD_DIM
    c = (ATTN_HEAD_DIM ** -0.5) * LOG2E
    q = q_ref[...]
    word = w_ref[...]
    ngrp = (i + ATT_UNROLL) // ATT_UNROLL

    def pass1(jg, mx):
        for u in range(ATT_UNROLL):
            j = ATT_UNROLL * jg + u
            kt = k_ref[pl.ds(pl.multiple_of(j * tk, tk), tk), :]
            s2 = lax.dot_general(q, kt, _NT, preferred_element_type=F32) * c
            s2 = s2 + band_ref[jnp.clip(i - j, 0, 2)]
            keep = jnp.bitwise_and(word, jnp.left_shift(jnp.int32(1), j)) != 0
            s2 = jnp.where(keep, s2, MASK_NEG)
            s_ref[j] = s2
            for cc in range(tk // LANE):
                mx = jnp.maximum(mx, s2[:, cc * LANE:(cc + 1) * LANE])
        return mx

    mx = lax.fori_loop(0, ngrp, pass1, jnp.full((tq, LANE), MASK_NEG, F32))
    m2 = jnp.broadcast_to(jnp.max(mx, axis=1, keepdims=True), (tq, tk))
    ones = jnp.ones((tk, LANE), BF16)

    def pass2(jg, acc):
        for u in range(ATT_UNROLL):
            j = ATT_UNROLL * jg + u
            p = jnp.exp2(s_ref[j] - m2).astype(BF16)
            vt = v_ref[pl.ds(pl.multiple_of(j * tk, tk), tk), :]
            acc = acc + jnp.dot(p, jnp.concatenate([vt, ones], axis=1), preferred_element_type=F32)
        return acc

    acc = lax.fori_loop(0, ngrp, pass2, jnp.zeros((tq, dh + LANE), F32))
    o_ref[...] = acc[:, :dh] / acc[:, dh:dh + 1]


def _attention(attn_grp, words, band):
    tq, tk = ATT_TQ, ATT_TK
    nq, nk = SEQ // tq, SEQ // tk
    dh = ATTN_HEAD_DIM
    nh = N_ATTN_HEADS
    return pl.pallas_call(
        _attn_kernel,
        grid=(nh, nq),
        in_specs=[
            pl.BlockSpec((tq, dh), lambda h, i: (i, h)),
            pl.BlockSpec((SEQ, dh), lambda h, i: (0, nh + h)),
            pl.BlockSpec((SEQ, dh), lambda h, i: (0, 2 * nh + h)),
            pl.BlockSpec((tq, tk), lambda h, i: (i, 0)),
            pl.BlockSpec((3, None, tq, tk), lambda h, i: (0, h, 0, 0)),
        ],
        out_specs=pl.BlockSpec((tq, dh), lambda h, i: (i, h)),
        out_shape=jax.ShapeDtypeStruct((SEQ, ATTN_WIDTH), F32),
        scratch_shapes=[pltpu.VMEM((nk, tq, tk), F32)],
        compiler_params=_cparams("parallel", "parallel"),
        name="attention",
    )(attn_grp, attn_grp, attn_grp, words, band)


POOL_HALO = 16
CONV_HALO = 8


def _block_diag_dot(x, w_ref):
    outs = []
    for g in range(w_ref.shape[0]):
        xs = x[:, g * LRU_BLOCK:(g + 1) * LRU_BLOCK].astype(BF16)
        outs.append(jnp.dot(xs, w_ref[g], preferred_element_type=F32))
    return jnp.concatenate(outs, axis=1)


def _pool_lru_kernel(pin_ref, lin_ref, gate_ref, pw_ref, ps_ref, cw_ref, cb_ref, rw_ref, rb_ref,
                     iw_ref, ib_ref, lam_ref, pool_o, lru_o, pprev_ref, lprev_ref, h_ref, *, T):
    i = pl.program_id(0)

    @pl.when(i == 0)
    def _():
        pprev_ref[...] = jnp.zeros(pprev_ref.shape, F32)
        lprev_ref[...] = jnp.zeros(lprev_ref.shape, F32)
        h_ref[...] = jnp.zeros(h_ref.shape, F32)

    t_glob = i * T + lax.broadcasted_iota(jnp.int32, (T, 1), 0)

    u = pin_ref[...]
    ext = jnp.concatenate([pprev_ref[...], u], axis=0)
    pprev_ref[...] = u[T - POOL_HALO:, :]
    p2 = ext[1:] + ext[:-1]
    p4 = p2[2:, POOL_GROUP:] + p2[:-2, POOL_GROUP:]
    p8 = p4[4:, POOL_GROUP:] + p4[:-4, POOL_GROUP:]
    p16 = p8[8:, POOL_GROUP:] + p8[:-8, POOL_GROUP:]
    wsums = (p2[15:15 + T, :POOL_GROUP], p4[13:13 + T, :POOL_GROUP],
             p8[9:9 + T, :POOL_GROUP], p16[1:1 + T, :])
    for g, win in enumerate(POOL_WINDOWS):
        gs = slice(g * POOL_GROUP, (g + 1) * POOL_GROUP)
        count = jnp.minimum(t_glob + 1, win).astype(F32)
        dlt = wsums[g] / count - u[:, gs]
        y = jnp.dot(dlt.astype(BF16), pw_ref[g], preferred_element_type=F32)
        pool_o[:, gs] = y * ps_ref[:, gs]

    x = lin_ref[...]
    lext = jnp.concatenate([lprev_ref[...], x], axis=0)
    lprev_ref[...] = x[T - CONV_HALO:, :]
    xc = cb_ref[...]
    for jj in range(CONV_WIDTH):
        off = CONV_HALO - (CONV_WIDTH - 1) + jj
        xc = xc + cw_ref[jj:jj + 1, :] * lext[off:off + T, :]
    r = jax.nn.sigmoid(_block_diag_dot(xc, rw_ref) + rb_ref[...])
    ig = jax.nn.sigmoid(_block_diag_dot(xc, iw_ref) + ib_ref[...])
    z = -lam_ref[...]
    softplus = jnp.maximum(z, 0.0) + jnp.log1p(jnp.exp(-jnp.abs(z)))
    log_a = (-LRU_C * r) * softplus
    a = jnp.exp(log_a)
    b = jnp.sqrt(-jnp.tanh(log_a) * (a * a + 1.0)) * (ig * xc)

    sh = 1
    while sh < T:
        a_sh = jnp.concatenate([jnp.ones((sh, LRU_WIDTH), F32), a[:T - sh]], axis=0)
        b_sh = jnp.concatenate([jnp.zeros((sh, LRU_WIDTH), F32), b[:T - sh]], axis=0)
        b = a * b_sh + b
        a = a * a_sh
        sh *= 2
    h = a * h_ref[...] + b
    h_ref[...] = h[T - 1:, :]

    gt = gate_ref[...]
    cdf = 0.5 * (1.0 + jnp.tanh(math.sqrt(2.0 / math.pi) * (gt + 0.044715 * (gt * gt * gt))))
    lru_o[...] = h * (gt * cdf)


def _pool_lru(aux, pool_w, pool_scale, conv_w, conv_b, r_w, r_b, i_w, i_b, lam):
    T = 256
    full = lambda a: pl.BlockSpec(a.shape, lambda i: (0,) * a.ndim)
    row = lambda v: v.reshape(1, -1)
    args = (pool_w.astype(BF16), row(pool_scale), conv_w, row(conv_b), r_w.astype(BF16), row(r_b),
            i_w.astype(BF16), row(i_b), row(lam))
    return pl.pallas_call(
        functools.partial(_pool_lru_kernel, T=T),
        grid=(SEQ // T,),
        in_specs=[
            pl.BlockSpec((T, POOL_WIDTH), lambda i: (i, AUX_POOL // POOL_WIDTH)),
            pl.BlockSpec((T, LRU_WIDTH), lambda i: (i, AUX_LRU_IN // LRU_WIDTH)),
            pl.BlockSpec((T, LRU_WIDTH), lambda i: (i, AUX_LRU_GATE // LRU_WIDTH)),
        ] + [full(a) for a in args],
        out_specs=[pl.BlockSpec((T, POOL_WIDTH), lambda i: (i, 0)),
                   pl.BlockSpec((T, LRU_WIDTH), lambda i: (i, 0))],
        out_shape=[jax.ShapeDtypeStruct((SEQ, POOL_WIDTH), F32),
                   jax.ShapeDtypeStruct((SEQ, LRU_WIDTH), F32)],
        scratch_shapes=[pltpu.VMEM((POOL_HALO, POOL_WIDTH), F32),
                        pltpu.VMEM((CONV_HALO, LRU_WIDTH), F32),
                        pltpu.VMEM((1, LRU_WIDTH), F32)],
        compiler_params=_cparams("arbitrary"),
        name="pool_lru",
    )(aux, aux, aux, *args)


def _mix_out_kernel(a_ref, p_ref, l_ref, x_ref, ga_ref, gp_ref, gl_ref, w_ref, gpost_ref, gate_ref,
                    o_ref):
    na = _rms(a_ref[...], ga_ref[...]).astype(BF16)
    npool = _rms(p_ref[...], gp_ref[...]).astype(BF16)
    nl = _rms(l_ref[...], gl_ref[...]).astype(BF16)
    p0, p1 = ATTN_WIDTH, ATTN_WIDTH + POOL_WIDTH
    y = jnp.dot(na, w_ref[0:p0, :], preferred_element_type=F32)
    y = y + jnp.dot(npool, w_ref[p0:p1, :], preferred_element_type=F32)
    y = y + jnp.dot(nl, w_ref[p1:, :], preferred_element_type=F32)
    o_ref[...] = x_ref[...] + gate_ref[...] * _rms(y, gpost_ref[...])


def _mix_out(attn, pool, lru, x, g_attn, g_pool, g_lru, w_out, g_post, gate):
    tm = 256
    vec = lambda n: pl.BlockSpec((1, n), lambda i: (0, 0))
    rows = lambda n: pl.BlockSpec((tm, n), lambda i: (i, 0))
    return pl.pallas_call(
        _mix_out_kernel,
        grid=(SEQ // tm,),
        in_specs=[rows(ATTN_WIDTH), rows(POOL_WIDTH), rows(LRU_WIDTH), rows(D_MODEL),
                  vec(ATTN_WIDTH), vec(POOL_WIDTH), vec(LRU_WIDTH),
                  pl.BlockSpec((D_MODEL, D_MODEL), lambda i: (0, 0)),
                  vec(D_MODEL), vec(D_MODEL)],
        out_specs=rows(D_MODEL),
        out_shape=jax.ShapeDtypeStruct((SEQ, D_MODEL), F32),
        compiler_params=_cparams("parallel"),
        name="mix_out",
    )(attn, pool, lru, x, g_attn, g_pool, g_lru, w_out, g_post, gate)


def _mlp_down_kernel(u_ref, w_ref, x_ref, gpost_ref, gate_ref, o_ref, acc_ref):
    k = pl.program_id(1)

    @pl.when(k == 0)
    def _():
        acc_ref[...] = jnp.zeros(acc_ref.shape, F32)

    acc_ref[...] += jnp.dot(u_ref[...], w_ref[...], preferred_element_type=F32)

    @pl.when(k == pl.num_programs(1) - 1)
    def _():
        o_ref[...] = x_ref[...] + gate_ref[...] * _rms(acc_ref[...], gpost_ref[...])


def _mlp_down(u2, w_down, x, g_post, gate):
    tm, tk = 512, 1024
    vec = pl.BlockSpec((1, D_MODEL), lambda i, k: (0, 0))
    return pl.pallas_call(
        _mlp_down_kernel,
        grid=(SEQ // tm, MLP_HIDDEN // tk),
        in_specs=[pl.BlockSpec((tm, tk), lambda i, k: (i, k)),
                  pl.BlockSpec((tk, D_MODEL), lambda i, k: (k, 0)),
                  pl.BlockSpec((tm, D_MODEL), lambda i, k: (i, 0)),
                  vec, vec],
        out_specs=pl.BlockSpec((tm, D_MODEL), lambda i, k: (i, 0)),
        out_shape=jax.ShapeDtypeStruct((SEQ, D_MODEL), F32),
        scratch_shapes=[pltpu.VMEM((tm, D_MODEL), F32)],
        compiler_params=_cparams("parallel", "arbitrary"),
        name="mlp_down",
    )(u2, w_down, x, g_post, gate)


def kernel(x, c, positions, rel_bias, w_mod, b_mod, g_pre_mix, g_post_mix, g_pre_mlp, g_post_mlp, w_in, pool_w, pool_scale, conv_w, conv_b, gate_r_w, gate_r_b, gate_i_w, gate_i_b, lru_lambda, g_attn_out, g_pool_out, g_lru_out, w_out, w_mlp_up, w_mlp_down):
    assert x.shape == (1, SEQ, D_MODEL) and positions.shape == (1, SEQ)
    del positions
    xs = x[0]
    mod = _modulation(c, w_mod, b_mod)
    band = _bias_band(rel_bias)
    row = lambda v: v.reshape(1, -1)
    n_attn_cols = 4 * ATTN_WIDTH
    aux_lo = n_attn_cols + IDX_DIM + N_IDX_HEADS
    for l in range(DEPTH):
        shift1, scale1, gate1, shift2, scale2, gate2 = [
            mod[l, :, n * D_MODEL:(n + 1) * D_MODEL] for n in range(6)]
        w_attn = w_in[l, :, :n_attn_cols].astype(BF16)
        w_aux = jnp.concatenate(
            [w_in[l, :, aux_lo:], w_in[l, :, n_attn_cols:aux_lo],
             jnp.zeros((D_MODEL, AUX_WIDTH - (AUX_KIDX + IDX_DIM + N_IDX_HEADS)), F32)],
            axis=1).astype(BF16)

        g_pre = row(g_pre_mix[l])
        attn_grp = _adaln_matmul(xs, g_pre, scale1, shift1, w_attn, BF16,
                                 tm=512, tn=512, name="proj_attn")
        aux = _adaln_matmul(xs, g_pre, scale1, shift1, w_aux, F32,
                            tm=512, tn=AUX_WIDTH, name="proj_aux")
        words = _select(attn_grp, aux)
        attn = _attention(attn_grp, words, band)
        pool, lru = _pool_lru(aux, pool_w[l], pool_scale[l], conv_w[l], conv_b[l], gate_r_w[l],
                              gate_r_b[l], gate_i_w[l], gate_i_b[l], lru_lambda[l])
        xs = _mix_out(attn, pool, lru, xs, row(g_attn_out[l]), row(g_pool_out[l]),
                      row(g_lru_out[l]), w_out[l].astype(BF16), row(g_post_mix[l]), gate1)

        u2 = _adaln_matmul(xs, row(g_pre_mlp[l]), scale2, shift2, w_mlp_up[l].astype(BF16), BF16,
                           tm=512, tn=512, relu2=True, name="mlp_up")
        xs = _mlp_down(u2, w_mlp_down[l].astype(BF16), xs, row(g_post_mlp[l]), gate2)
    return xs[None]
```

```python
import functools
import math

import jax
import jax.numpy as jnp
from jax import lax
from jax.experimental import pallas as pl
from jax.experimental.pallas import tpu as pltpu

D_MODEL = 2048
SEQ = 8192
DEPTH = 2
ATTN_HEAD_DIM = 128
ATTN_WIDTH = 1024
N_ATTN_HEADS = 8
POOL_WIDTH = 512
POOL_WINDOWS = (2, 4, 8, 16)
POOL_GROUP = 128
LRU_WIDTH = 512
N_LRU_BLOCKS = 4
LRU_BLOCK = 128
LRU_C = 8.0
CONV_WIDTH = 4
N_IDX_HEADS = 16
IDX_DIM = 64
TOP_K = 256
N_BUCKETS = 32
MAX_DISTANCE = 128
MLP_HIDDEN = 4 * D_MODEL
NORM_EPS = 1e-6

AUX_POOL = 0
AUX_LRU_IN = 512
AUX_LRU_GATE = 1024
AUX_KIDX = 1536
AUX_WIDTH = 1664
LANE = 128

SEL_T = 256
ATT_TQ = 512
ATT_TK = SEL_T
ATT_UNROLL = 4
N_BAND = ATT_TQ // ATT_TK + 2
assert MAX_DISTANCE <= ATT_TK // 2 and SEQ // SEL_T <= 32
MASK_NEG = -1e30
LOG2E = math.log2(math.e)
VMEM_LIMIT = 56 * 1024 * 1024

F32 = jnp.float32
BF16 = jnp.bfloat16
_NT = (((1,), (1,)), ((), ()))


def _cparams(*sem):
    return pltpu.CompilerParams(dimension_semantics=sem, vmem_limit_bytes=VMEM_LIMIT)


def _rms(x, g):
    ms = jnp.mean(x * x, axis=-1, keepdims=True)
    return (x * lax.rsqrt(ms + NORM_EPS)) * g


def _mod_kernel(c_ref, w_ref, b_ref, o_ref):
    c = c_ref[...]
    c_act = c * jax.nn.sigmoid(c)
    o_ref[0] = jnp.sum(c_act * w_ref[0], axis=0, keepdims=True) + b_ref[0]


def _modulation(c, w_mod, b_mod):
    tn = 1024
    n = w_mod.shape[-1]
    return pl.pallas_call(
        _mod_kernel,
        grid=(DEPTH, n // tn),
        in_specs=[
            pl.BlockSpec((D_MODEL, 1), lambda l, j: (0, 0)),
            pl.BlockSpec((1, D_MODEL, tn), lambda l, j: (l, 0, j)),
            pl.BlockSpec((1, 1, tn), lambda l, j: (l, 0, j)),
        ],
        out_specs=pl.BlockSpec((1, 1, tn), lambda l, j: (l, 0, j)),
        out_shape=jax.ShapeDtypeStruct((DEPTH, 1, n), F32),
        compiler_params=_cparams("parallel", "parallel"),
        name="modulation",
    )(c.reshape(D_MODEL, 1), w_mod, b_mod.reshape(DEPTH, 1, n))


def _adaln_mm_kernel(x_ref, g_ref, sc_ref, sh_ref, w_ref, o_ref, h_ref, *, tm, relu2):
    rc = 64

    @pl.when(pl.program_id(1) == 0)
    def _():
        g = g_ref[...]
        sc1 = 1.0 + sc_ref[...]
        sh = sh_ref[...]

        def body(r, carry):
            rows = pl.ds(pl.multiple_of(r * rc, rc), rc)
            h = _rms(x_ref[rows, :], g) * sc1 + sh
            h_ref[rows, :] = h.astype(BF16)
            return carry

        lax.fori_loop(0, tm // rc, body, 0)

    acc = jnp.dot(h_ref[...], w_ref[...], preferred_element_type=F32)
    if relu2:
        acc = jnp.maximum(acc, 0.0)
        acc = acc * acc
    o_ref[...] = acc.astype(o_ref.dtype)


def _adaln_matmul(x, g, scale, shift, w, out_dtype, *, tm, tn, relu2=False, name):
    m, k = x.shape
    n = w.shape[1]
    vec = pl.BlockSpec((1, k), lambda i, j: (0, 0))
    return pl.pallas_call(
        functools.partial(_adaln_mm_kernel, tm=tm, relu2=relu2),
        grid=(m // tm, n // tn),
        in_specs=[
            pl.BlockSpec((tm, k), lambda i, j: (i, 0)),
            vec, vec, vec,
            pl.BlockSpec((k, tn), lambda i, j: (0, j)),
        ],
        out_specs=pl.BlockSpec((tm, tn), lambda i, j: (i, j)),
        out_shape=jax.ShapeDtypeStruct((m, n), out_dtype),
        scratch_shapes=[pltpu.VMEM((tm, k), BF16)],
        compiler_params=_cparams("parallel", "arbitrary"),
        name=name,
    )(x, g, scale, shift, w)


def _band_kernel(rb_ref, o_ref):
    d = pl.program_id(0)
    t = lax.broadcasted_iota(jnp.int32, (ATT_TQ, ATT_TK), 0)
    s = lax.broadcasted_iota(jnp.int32, (ATT_TQ, ATT_TK), 1)
    rel = jnp.maximum((d - (ATT_TQ // ATT_TK - 1)) * ATT_TK + t - s, 0)
    max_exact = N_BUCKETS // 2
    relf = jnp.maximum(rel.astype(F32), 1.0)
    large = max_exact + (jnp.log(relf / max_exact) / math.log(MAX_DISTANCE / max_exact)
                         * (N_BUCKETS - max_exact)).astype(jnp.int32)
    large = jnp.minimum(large, N_BUCKETS - 1)
    bucket = jnp.where(rel < max_exact, rel, large)
    for h in range(N_ATTN_HEADS):
        val = jnp.zeros((ATT_TQ, ATT_TK), F32)
        for b in range(N_BUCKETS):
            val = jnp.where(bucket == b, rb_ref[b, h], val)
        o_ref[0, h] = val * LOG2E


def _bias_band(rel_bias):
    return pl.pallas_call(
        _band_kernel,
        grid=(N_BAND,),
        in_specs=[pl.BlockSpec(memory_space=pltpu.SMEM)],
        out_specs=pl.BlockSpec((1, N_ATTN_HEADS, ATT_TQ, ATT_TK), lambda d: (d, 0, 0, 0)),
        out_shape=jax.ShapeDtypeStruct((N_BAND, N_ATTN_HEADS, ATT_TQ, ATT_TK), F32),
        compiler_params=_cparams("parallel"),
        name="bias_band",
    )(rel_bias)


def _float_to_key(x):
    bits = lax.bitcast_convert_type(x, jnp.int32)
    return jnp.where(bits >= 0, bits, jnp.bitwise_xor(bits, jnp.int32(0x7FFFFFFF)))


def _key_to_float(key):
    bits = jnp.where(key >= 0, key, jnp.bitwise_xor(key, jnp.int32(0x7FFFFFFF)))
    return lax.bitcast_convert_type(bits, F32)


def _select_kernel(qi_ref, kw_ref, o_ref, kb_ref, sc_ref, hi_ref, lo_ref):
    i = pl.program_id(0)
    tq = tk = SEL_T
    idx_scale = (N_IDX_HEADS * IDX_DIM) ** -0.5

    @pl.when(i == 0)
    def _():
        kb_ref[...] = kw_ref[:, 0:IDX_DIM].astype(BF16)

    w_t = jnp.transpose(kw_ref[pl.ds(pl.multiple_of(i * tq, tq), tq), :])

    nchunks = i + 1
    key_pos = lax.broadcasted_iota(jnp.int32, (tk, tq), 0)
    qry_pos = i * tq + lax.broadcasted_iota(jnp.int32, (tk, tq), 1)

    def score_body(j, carry):
        kc = kb_ref[pl.ds(pl.multiple_of(j * tk, tk), tk), :]
        acc = jnp.zeros((tk, tq), F32)
        for h in range(N_IDX_HEADS):
            d = lax.dot_general(kc, qi_ref[:, h * IDX_DIM:(h + 1) * IDX_DIM], _NT,
                                preferred_element_type=F32)
            acc = acc + w_t[IDX_DIM + h:IDX_DIM + h + 1, :] * jnp.maximum(d, 0.0)
        score = jnp.where(key_pos + j * tk <= qry_pos, acc * idx_scale, -jnp.inf)
        sc_ref[j] = score
        key = _float_to_key(score)
        hi_ref[j] = lax.shift_right_arithmetic(key, jnp.int32(16)).astype(jnp.int16)
        lo_ref[j] = (jnp.bitwise_and(key, jnp.int32(0xFFFF)) - 32768).astype(jnp.int16)
        return carry

    lax.fori_loop(0, nchunks, score_body, 0)

    i16 = jnp.int16
    pack = 16
    one16 = jnp.ones((pack, tq), i16)
    zero16 = jnp.zeros((pack, tq), i16)

    def count_ge(ref, cand):
        cand_b = jnp.broadcast_to(cand, (pack, tq)).astype(i16)

        def body(j, acc):
            x = ref[j]
            for g in range(tk // pack):
                acc = acc + jnp.where(x[g * pack:(g + 1) * pack, :] >= cand_b, one16, zero16)
            return acc

        acc = lax.fori_loop(0, nchunks, body, zero16)
        return jnp.sum(acc.astype(jnp.int32).astype(F32), axis=0, keepdims=True)

    def kth_largest(ref, rank):
        lo0 = jnp.full((1, tq), -32768, jnp.int32)
        hi0 = jnp.full((1, tq), 32767, jnp.int32)

        def body(_, carry):
            lo, hi, above = carry
            mid = lo + lax.shift_right_logical(hi - lo + 1, jnp.int32(1))
            cnt = count_ge(ref, mid)
            ok = cnt >= rank
            return jnp.where(ok, mid, lo), jnp.where(ok, hi, mid - 1), jnp.where(ok, above, cnt)

        lo, _, above = lax.fori_loop(0, 16, body, (lo0, hi0, jnp.zeros((1, tq), F32)))
        return lo, above

    t_hi, above = kth_largest(hi_ref, jnp.full((1, tq), float(TOP_K), F32))
    t_hi_b = jnp.broadcast_to(t_hi, (tk, tq)).astype(i16)

    def narrow_body(j, carry):
        lo_ref[j] = jnp.where(hi_ref[j] == t_hi_b, lo_ref[j], jnp.full((tk, tq), -32768, i16))
        return carry

    lax.fori_loop(0, nchunks, narrow_body, 0)
    t_lo, _ = kth_largest(lo_ref, float(TOP_K) - above)
    key = jnp.bitwise_or(jnp.left_shift(t_hi, 16), t_lo + 32768)
    thr = jnp.broadcast_to(_key_to_float(key), (tk, tq))

    def word_body(j, word):
        keep = jnp.logical_and(sc_ref[j] >= thr, key_pos + j * tk <= qry_pos)
        return jnp.bitwise_or(word, jnp.where(keep, jnp.left_shift(jnp.int32(1), j), 0))

    word_t = lax.fori_loop(0, nchunks, word_body, jnp.zeros((tk, tq), jnp.int32))
    o_ref[...] = jnp.transpose(word_t)


def _select(attn_grp, aux):
    tq = tk = SEL_T
    nq, nk = SEQ // tq, SEQ // tk
    return pl.pallas_call(
        _select_kernel,
        grid=(nq,),
        in_specs=[
            pl.BlockSpec((tq, N_IDX_HEADS * IDX_DIM), lambda i: (i, 3)),
            pl.BlockSpec((SEQ, LANE), lambda i: (0, AUX_KIDX // LANE)),
        ],
        out_specs=pl.BlockSpec((tq, tk), lambda i: (i, 0)),
        out_shape=jax.ShapeDtypeStruct((SEQ, tk), jnp.int32),
        scratch_shapes=[
            pltpu.VMEM((SEQ, IDX_DIM), BF16),
            pltpu.VMEM((nk, tk, tq), F32),
            pltpu.VMEM((nk, tk, tq), jnp.int16),
            pltpu.VMEM((nk, tk, tq), jnp.int16),
        ],
        compiler_params=_cparams("arbitrary"),
        name="select",
    )(attn_grp, aux)


def _attn_kernel(q_ref, k_ref, v_ref, w_ref, band_ref, o_ref, s_ref):
    i = pl.program_id(1)
    tq, tk, unroll = ATT_TQ, ATT_TK, ATT_UNROLL
    dh = ATTN_HEAD_DIM
    c = (ATTN_HEAD_DIM ** -0.5) * LOG2E
    r = tq // tk
    ngrp = (r * (i + 1) + unroll - 1) // unroll

    def pass1(jg, mx):
        for u in range(unroll):
            j = unroll * jg + u
            kt = k_ref[pl.ds(pl.multiple_of(j * tk, tk), tk), :]
            s2 = lax.dot_general(q_ref[...], kt, _NT, preferred_element_type=F32) * c
            s2 = s2 + band_ref[jnp.clip(r * i + (r - 1) - j, 0, N_BAND - 1)]
            keep = jnp.bitwise_and(w_ref[...], jnp.left_shift(jnp.int32(1), j)) != 0
            s2 = jnp.where(keep, s2, MASK_NEG)
            s_ref[j] = s2
            for cc in range(tk // LANE):
                mx = jnp.maximum(mx, s2[:, cc * LANE:(cc + 1) * LANE])
        return mx

    mx = lax.fori_loop(0, ngrp, pass1, jnp.full((tq, LANE), MASK_NEG, F32))
    m_b = jnp.broadcast_to(jnp.max(mx, axis=1, keepdims=True), (tq, tk))
    ones = jnp.ones((unroll * tk, LANE), BF16)

    def pass2(jg, acc):
        p = jnp.concatenate([jnp.exp2(s_ref[unroll * jg + u] - m_b).astype(BF16)
                             for u in range(unroll)], axis=1)
        vt = v_ref[pl.ds(pl.multiple_of(jg * (unroll * tk), unroll * tk), unroll * tk), :]
        return acc + jnp.dot(p, jnp.concatenate([vt, ones], axis=1), preferred_element_type=F32)

    acc = lax.fori_loop(0, ngrp, pass2, jnp.zeros((tq, dh + LANE), F32))
    o_ref[...] = acc[:, :dh] / acc[:, dh:dh + 1]


def _attention(attn_grp, words, band):
    tq, tk = ATT_TQ, ATT_TK
    nq, nk = SEQ // tq, SEQ // tk
    dh = ATTN_HEAD_DIM
    nh = N_ATTN_HEADS
    return pl.pallas_call(
        _attn_kernel,
        grid=(nh, nq),
        in_specs=[
            pl.BlockSpec((tq, dh), lambda h, i: (i, h)),
            pl.BlockSpec((SEQ, dh), lambda h, i: (0, nh + h)),
            pl.BlockSpec((SEQ, dh), lambda h, i: (0, 2 * nh + h)),
            pl.BlockSpec((tq, tk), lambda h, i: (i, 0)),
            pl.BlockSpec((N_BAND, None, tq, tk), lambda h, i: (0, h, 0, 0)),
        ],
        out_specs=pl.BlockSpec((tq, dh), lambda h, i: (i, h)),
        out_shape=jax.ShapeDtypeStruct((SEQ, ATTN_WIDTH), F32),
        scratch_shapes=[pltpu.VMEM((nk, tq, tk), F32)],
        compiler_params=_cparams("parallel", "parallel"),
        name="attention",
    )(attn_grp, attn_grp, attn_grp, words, band)


POOL_HALO = 16
CONV_HALO = 8


def _block_diag_dot(x, w_ref):
    outs = []
    for g in range(w_ref.shape[0]):
        xs = x[:, g * LRU_BLOCK:(g + 1) * LRU_BLOCK].astype(BF16)
        outs.append(jnp.dot(xs, w_ref[g], preferred_element_type=F32))
    return jnp.concatenate(outs, axis=1)


def _pool_lru_kernel(pin_ref, lin_ref, gate_ref, pw_ref, ps_ref, cw_ref, cb_ref, rw_ref, rb_ref,
                     iw_ref, ib_ref, lam_ref, pool_o, lru_o, pprev_ref, lprev_ref, h_ref, *, T):
    i = pl.program_id(0)

    @pl.when(i == 0)
    def _():
        pprev_ref[...] = jnp.zeros(pprev_ref.shape, F32)
        lprev_ref[...] = jnp.zeros(lprev_ref.shape, F32)
        h_ref[...] = jnp.zeros(h_ref.shape, F32)

    t_glob = i * T + lax.broadcasted_iota(jnp.int32, (T, 1), 0)

    u = pin_ref[...]
    ext = jnp.concatenate([pprev_ref[...], u], axis=0)
    pprev_ref[...] = u[T - POOL_HALO:, :]
    p2 = ext[1:] + ext[:-1]
    p4 = p2[2:, POOL_GROUP:] + p2[:-2, POOL_GROUP:]
    p8 = p4[4:, POOL_GROUP:] + p4[:-4, POOL_GROUP:]
    p16 = p8[8:, POOL_GROUP:] + p8[:-8, POOL_GROUP:]
    wsums = (p2[15:15 + T, :POOL_GROUP], p4[13:13 + T, :POOL_GROUP],
             p8[9:9 + T, :POOL_GROUP], p16[1:1 + T, :])
    for g, win in enumerate(POOL_WINDOWS):
        gs = slice(g * POOL_GROUP, (g + 1) * POOL_GROUP)
        count = jnp.minimum(t_glob + 1, win).astype(F32)
        dlt = wsums[g] / count - u[:, gs]
        y = jnp.dot(dlt.astype(BF16), pw_ref[g], preferred_element_type=F32)
        pool_o[:, gs] = y * ps_ref[:, gs]

    x = lin_ref[...]
    lext = jnp.concatenate([lprev_ref[...], x], axis=0)
    lprev_ref[...] = x[T - CONV_HALO:, :]
    xc = cb_ref[...]
    for jj in range(CONV_WIDTH):
        off = CONV_HALO - (CONV_WIDTH - 1) + jj
        xc = xc + cw_ref[jj:jj + 1, :] * lext[off:off + T, :]
    r = jax.nn.sigmoid(_block_diag_dot(xc, rw_ref) + rb_ref[...])
    ig = jax.nn.sigmoid(_block_diag_dot(xc, iw_ref) + ib_ref[...])
    z = -lam_ref[...]
    softplus = jnp.maximum(z, 0.0) + jnp.log1p(jnp.exp(-jnp.abs(z)))
    log_a = (-LRU_C * r) * softplus
    a = jnp.exp(log_a)
    b = jnp.sqrt(-jnp.tanh(log_a) * (a * a + 1.0)) * (ig * xc)

    sh = 1
    while sh < T:
        a_sh = jnp.concatenate([jnp.ones((sh, LRU_WIDTH), F32), a[:T - sh]], axis=0)
        b_sh = jnp.concatenate([jnp.zeros((sh, LRU_WIDTH), F32), b[:T - sh]], axis=0)
        b = a * b_sh + b
        a = a * a_sh
        sh *= 2
    h = a * h_ref[...] + b
    h_ref[...] = h[T - 1:, :]

    gt = gate_ref[...]
    cdf = 0.5 * (1.0 + jnp.tanh(math.sqrt(2.0 / math.pi) * (gt + 0.044715 * (gt * gt * gt))))
    lru_o[...] = h * (gt * cdf)


def _pool_lru(aux, pool_w, pool_scale, conv_w, conv_b, r_w, r_b, i_w, i_b, lam):
    T = 256
    full = lambda a: pl.BlockSpec(a.shape, lambda i: (0,) * a.ndim)
    row = lambda v: v.reshape(1, -1)
    args = (pool_w.astype(BF16), row(pool_scale), conv_w, row(conv_b), r_w.astype(BF16), row(r_b),
            i_w.astype(BF16), row(i_b), row(lam))
    return pl.pallas_call(
        functools.partial(_pool_lru_kernel, T=T),
        grid=(SEQ // T,),
        in_specs=[
            pl.BlockSpec((T, POOL_WIDTH), lambda i: (i, AUX_POOL // POOL_WIDTH)),
            pl.BlockSpec((T, LRU_WIDTH), lambda i: (i, AUX_LRU_IN // LRU_WIDTH)),
            pl.BlockSpec((T, LRU_WIDTH), lambda i: (i, AUX_LRU_GATE // LRU_WIDTH)),
        ] + [full(a) for a in args],
        out_specs=[pl.BlockSpec((T, POOL_WIDTH), lambda i: (i, 0)),
                   pl.BlockSpec((T, LRU_WIDTH), lambda i: (i, 0))],
        out_shape=[jax.ShapeDtypeStruct((SEQ, POOL_WIDTH), F32),
                   jax.ShapeDtypeStruct((SEQ, LRU_WIDTH), F32)],
        scratch_shapes=[pltpu.VMEM((POOL_HALO, POOL_WIDTH), F32),
                        pltpu.VMEM((CONV_HALO, LRU_WIDTH), F32),
                        pltpu.VMEM((1, LRU_WIDTH), F32)],
        compiler_params=_cparams("arbitrary"),
        name="pool_lru",
    )(aux, aux, aux, *args)


def _mix_out_kernel(a_ref, p_ref, l_ref, x_ref, ga_ref, gp_ref, gl_ref, w_ref, gpost_ref, gate_ref,
                    o_ref):
    na = _rms(a_ref[...], ga_ref[...]).astype(BF16)
    npool = _rms(p_ref[...], gp_ref[...]).astype(BF16)
    nl = _rms(l_ref[...], gl_ref[...]).astype(BF16)
    p0, p1 = ATTN_WIDTH, ATTN_WIDTH + POOL_WIDTH
    y = jnp.dot(na, w_ref[0:p0, :], preferred_element_type=F32)
    y = y + jnp.dot(npool, w_ref[p0:p1, :], preferred_element_type=F32)
    y = y + jnp.dot(nl, w_ref[p1:, :], preferred_element_type=F32)
    o_ref[...] = x_ref[...] + gate_ref[...] * _rms(y, gpost_ref[...])


def _mix_out(attn, pool, lru, x, g_attn, g_pool, g_lru, w_out, g_post, gate):
    tm = 256
    vec = lambda n: pl.BlockSpec((1, n), lambda i: (0, 0))
    rows = lambda n: pl.BlockSpec((tm, n), lambda i: (i, 0))
    return pl.pallas_call(
        _mix_out_kernel,
        grid=(SEQ // tm,),
        in_specs=[rows(ATTN_WIDTH), rows(POOL_WIDTH), rows(LRU_WIDTH), rows(D_MODEL),
                  vec(ATTN_WIDTH), vec(POOL_WIDTH), vec(LRU_WIDTH),
                  pl.BlockSpec((D_MODEL, D_MODEL), lambda i: (0, 0)),
                  vec(D_MODEL), vec(D_MODEL)],
        out_specs=rows(D_MODEL),
        out_shape=jax.ShapeDtypeStruct((SEQ, D_MODEL), F32),
        compiler_params=_cparams("parallel"),
        name="mix_out",
    )(attn, pool, lru, x, g_attn, g_pool, g_lru, w_out, g_post, gate)


def _mlp_down_kernel(u_ref, w_ref, x_ref, gpost_ref, gate_ref, o_ref, acc_ref):
    k = pl.program_id(1)

    @pl.when(k == 0)
    def _():
        acc_ref[...] = jnp.zeros(acc_ref.shape, F32)

    acc_ref[...] += jnp.dot(u_ref[...], w_ref[...], preferred_element_type=F32)

    @pl.when(k == pl.num_programs(1) - 1)
    def _():
        o_ref[...] = x_ref[...] + gate_ref[...] * _rms(acc_ref[...], gpost_ref[...])


def _mlp_down(u2, w_down, x, g_post, gate):
    tm, tk = 512, 2048
    vec = pl.BlockSpec((1, D_MODEL), lambda i, k: (0, 0))
    return pl.pallas_call(
        _mlp_down_kernel,
        grid=(SEQ // tm, MLP_HIDDEN // tk),
        in_specs=[pl.BlockSpec((tm, tk), lambda i, k: (i, k)),
                  pl.BlockSpec((tk, D_MODEL), lambda i, k: (k, 0)),
                  pl.BlockSpec((tm, D_MODEL), lambda i, k: (i, 0)),
                  vec, vec],
        out_specs=pl.BlockSpec((tm, D_MODEL), lambda i, k: (i, 0)),
        out_shape=jax.ShapeDtypeStruct((SEQ, D_MODEL), F32),
        scratch_shapes=[pltpu.VMEM((tm, D_MODEL), F32)],
        compiler_params=_cparams("parallel", "arbitrary"),
        name="mlp_down",
    )(u2, w_down, x, g_post, gate)


def kernel(x, c, positions, rel_bias, w_mod, b_mod, g_pre_mix, g_post_mix, g_pre_mlp, g_post_mlp, w_in, pool_w, pool_scale, conv_w, conv_b, gate_r_w, gate_r_b, gate_i_w, gate_i_b, lru_lambda, g_attn_out, g_pool_out, g_lru_out, w_out, w_mlp_up, w_mlp_down):
    assert x.shape == (1, SEQ, D_MODEL) and positions.shape == (1, SEQ)
    del positions
    xs = x[0]
    mod = _modulation(c, w_mod, b_mod)
    band = _bias_band(rel_bias)
    row = lambda v: v.reshape(1, -1)
    n_attn_cols = 4 * ATTN_WIDTH
    aux_lo = n_attn_cols + IDX_DIM + N_IDX_HEADS
    for l in range(DEPTH):
        shift1, scale1, gate1, shift2, scale2, gate2 = [
            mod[l, :, n * D_MODEL:(n + 1) * D_MODEL] for n in range(6)]
        w_attn = w_in[l, :, :n_attn_cols].astype(BF16)
        w_aux = jnp.concatenate(
            [w_in[l, :, aux_lo:], w_in[l, :, n_attn_cols:aux_lo],
             jnp.zeros((D_MODEL, AUX_WIDTH - (AUX_KIDX + IDX_DIM + N_IDX_HEADS)), F32)],
            axis=1).astype(BF16)

        g_pre = row(g_pre_mix[l])
        attn_grp = _adaln_matmul(xs, g_pre, scale1, shift1, w_attn, BF16,
                                 tm=1024, tn=1024, name="proj_attn")
        aux = _adaln_matmul(xs, g_pre, scale1, shift1, w_aux, F32,
                            tm=512, tn=AUX_WIDTH, name="proj_aux")
        words = _select(attn_grp, aux)
        attn = _attention(attn_grp, words, band)
        pool, lru = _pool_lru(aux, pool_w[l], pool_scale[l], conv_w[l], conv_b[l], gate_r_w[l],
                              gate_r_b[l], gate_i_w[l], gate_i_b[l], lru_lambda[l])
        xs = _mix_out(attn, pool, lru, xs, row(g_attn_out[l]), row(g_pool_out[l]),
                      row(g_lru_out[l]), w_out[l].astype(BF16), row(g_post_mix[l]), gate1)

        u2 = _adaln_matmul(xs, row(g_pre_mlp[l]), scale2, shift2, w_mlp_up[l].astype(BF16), BF16,
                           tm=1024, tn=1024, relu2=True, name="mlp_up")
        xs = _mlp_down(u2, w_mlp_down[l].astype(BF16), xs, row(g_post_mlp[l]), gate2)
    return xs[None]
```

```python
import functools
import math

import jax
import jax.numpy as jnp
from jax import lax
from jax.experimental import pallas as pl
from jax.experimental.pallas import tpu as pltpu

D_MODEL = 2048
SEQ = 8192
DEPTH = 2
ATTN_HEAD_DIM = 128
ATTN_WIDTH = 1024
N_ATTN_HEADS = 8
POOL_WIDTH = 512
POOL_WINDOWS = (2, 4, 8, 16)
POOL_GROUP = 128
LRU_WIDTH = 512
N_LRU_BLOCKS = 4
LRU_BLOCK = 128
LRU_C = 8.0
CONV_WIDTH = 4
N_IDX_HEADS = 16
IDX_DIM = 64
TOP_K = 256
N_BUCKETS = 32
MAX_DISTANCE = 128
MLP_HIDDEN = 4 * D_MODEL
NORM_EPS = 1e-6

AUX_POOL = 0
AUX_LRU_IN = 512
AUX_LRU_GATE = 1024
AUX_KIDX = 1536
AUX_WIDTH = 1664
LANE = 128

SEL_T = 256
SEL_GROUP = 2
ATT_TQ = 512
ATT_TK = SEL_T
ATT_UNROLL = 4
N_BAND = ATT_TQ // ATT_TK + 2
assert MAX_DISTANCE <= ATT_TK // 2 and SEQ // SEL_T <= 32
MASK_NEG = -1e30
LOG2E = math.log2(math.e)
VMEM_LIMIT = 56 * 1024 * 1024

F32 = jnp.float32
BF16 = jnp.bfloat16
_NT = (((1,), (1,)), ((), ()))


def _cparams(*sem):
    return pltpu.CompilerParams(dimension_semantics=sem, vmem_limit_bytes=VMEM_LIMIT)


def _rms(x, g):
    ms = jnp.mean(x * x, axis=-1, keepdims=True)
    return (x * lax.rsqrt(ms + NORM_EPS)) * g


def _mod_kernel(c_ref, w_ref, b_ref, o_ref):
    c = c_ref[...]
    c_act = c * jax.nn.sigmoid(c)
    o_ref[0] = jnp.sum(c_act * w_ref[0], axis=0, keepdims=True) + b_ref[0]


def _modulation(c, w_mod, b_mod):
    tn = 1024
    n = w_mod.shape[-1]
    return pl.pallas_call(
        _mod_kernel,
        grid=(DEPTH, n // tn),
        in_specs=[
            pl.BlockSpec((D_MODEL, 1), lambda l, j: (0, 0)),
            pl.BlockSpec((1, D_MODEL, tn), lambda l, j: (l, 0, j)),
            pl.BlockSpec((1, 1, tn), lambda l, j: (l, 0, j)),
        ],
        out_specs=pl.BlockSpec((1, 1, tn), lambda l, j: (l, 0, j)),
        out_shape=jax.ShapeDtypeStruct((DEPTH, 1, n), F32),
        compiler_params=_cparams("parallel", "parallel"),
        name="modulation",
    )(c.reshape(D_MODEL, 1), w_mod, b_mod.reshape(DEPTH, 1, n))


def _adaln_mm_kernel(x_ref, g_ref, sc_ref, sh_ref, w_ref, o_ref, h_ref, *, tm, relu2):
    rc = 64

    @pl.when(pl.program_id(1) == 0)
    def _():
        g = g_ref[...]
        sc1 = 1.0 + sc_ref[...]
        sh = sh_ref[...]

        def body(r, carry):
            rows = pl.ds(pl.multiple_of(r * rc, rc), rc)
            h = _rms(x_ref[rows, :], g) * sc1 + sh
            h_ref[rows, :] = h.astype(BF16)
            return carry

        lax.fori_loop(0, tm // rc, body, 0)

    acc = jnp.dot(h_ref[...], w_ref[...].astype(BF16), preferred_element_type=F32)
    if relu2:
        acc = jnp.maximum(acc, 0.0)
        acc = acc * acc
    o_ref[...] = acc.astype(o_ref.dtype)


def _adaln_matmul(x, g, scale, shift, w, layer, n, out_dtype, *, tm, tn, relu2=False, name):
    m, k = x.shape
    vec = pl.BlockSpec((1, k), lambda i, j: (0, 0))
    return pl.pallas_call(
        functools.partial(_adaln_mm_kernel, tm=tm, relu2=relu2),
        grid=(m // tm, n // tn),
        in_specs=[
            pl.BlockSpec((tm, k), lambda i, j: (i, 0)),
            vec, vec, vec,
            pl.BlockSpec((None, k, tn), lambda i, j: (layer, 0, j)),
        ],
        out_specs=pl.BlockSpec((tm, tn), lambda i, j: (i, j)),
        out_shape=jax.ShapeDtypeStruct((m, n), out_dtype),
        scratch_shapes=[pltpu.VMEM((tm, k), BF16)],
        compiler_params=_cparams("parallel", "arbitrary"),
        name=name,
    )(x, g, scale, shift, w)


def _band_kernel(rb_ref, o_ref):
    d = pl.program_id(0)
    t = lax.broadcasted_iota(jnp.int32, (ATT_TQ, ATT_TK), 0)
    s = lax.broadcasted_iota(jnp.int32, (ATT_TQ, ATT_TK), 1)
    rel = jnp.maximum((d - (ATT_TQ // ATT_TK - 1)) * ATT_TK + t - s, 0)
    max_exact = N_BUCKETS // 2
    relf = jnp.maximum(rel.astype(F32), 1.0)
    large = max_exact + (jnp.log(relf / max_exact) / math.log(MAX_DISTANCE / max_exact)
                         * (N_BUCKETS - max_exact)).astype(jnp.int32)
    large = jnp.minimum(large, N_BUCKETS - 1)
    bucket = jnp.where(rel < max_exact, rel, large)
    for h in range(N_ATTN_HEADS):
        val = jnp.zeros((ATT_TQ, ATT_TK), F32)
        for b in range(N_BUCKETS):
            val = jnp.where(bucket == b, rb_ref[b, h], val)
        o_ref[0, h] = val * LOG2E


def _bias_band(rel_bias):
    return pl.pallas_call(
        _band_kernel,
        grid=(N_BAND,),
        in_specs=[pl.BlockSpec(memory_space=pltpu.SMEM)],
        out_specs=pl.BlockSpec((1, N_ATTN_HEADS, ATT_TQ, ATT_TK), lambda d: (d, 0, 0, 0)),
        out_shape=jax.ShapeDtypeStruct((N_BAND, N_ATTN_HEADS, ATT_TQ, ATT_TK), F32),
        compiler_params=_cparams("parallel"),
        name="bias_band",
    )(rel_bias)


def _float_to_key(x):
    bits = lax.bitcast_convert_type(x, jnp.int32)
    return jnp.where(bits >= 0, bits, jnp.bitwise_xor(bits, jnp.int32(0x7FFFFFFF)))


def _key_to_float(key):
    bits = jnp.where(key >= 0, key, jnp.bitwise_xor(key, jnp.int32(0x7FFFFFFF)))
    return lax.bitcast_convert_type(bits, F32)


def _select_kernel(qi_ref, kw_ref, o_ref, kb_ref, sc_ref, hi_ref, lo_ref):
    i = pl.program_id(0)
    tq = tk = SEL_T
    idx_scale = (N_IDX_HEADS * IDX_DIM) ** -0.5

    @pl.when(i == 0)
    def _():
        kb_ref[...] = kw_ref[:, 0:IDX_DIM].astype(BF16)

    w_t = jnp.transpose(kw_ref[pl.ds(pl.multiple_of(i * tq, tq), tq), :])

    nchunks = i + 1
    key_pos = lax.broadcasted_iota(jnp.int32, (tk, tq), 0)
    qry_pos = i * tq + lax.broadcasted_iota(jnp.int32, (tk, tq), 1)
    grp = SEL_GROUP

    def score_body(jg, carry):
        kc = kb_ref[pl.ds(pl.multiple_of(jg * (grp * tk), grp * tk), grp * tk), :]
        acc = jnp.zeros((grp * tk, tq), F32)
        for h in range(N_IDX_HEADS):
            d = lax.dot_general(kc, qi_ref[:, h * IDX_DIM:(h + 1) * IDX_DIM], _NT,
                                preferred_element_type=F32)
            acc = acc + w_t[IDX_DIM + h:IDX_DIM + h + 1, :] * jnp.maximum(d, 0.0)
        for u in range(grp):
            j = grp * jg + u
            score = jnp.where(key_pos + j * tk <= qry_pos, acc[u * tk:(u + 1) * tk] * idx_scale, -jnp.inf)
            sc_ref[j] = score
            key = _float_to_key(score)
            hi_ref[j] = lax.shift_right_arithmetic(key, jnp.int32(16)).astype(jnp.int16)
            lo_ref[j] = (jnp.bitwise_and(key, jnp.int32(0xFFFF)) - 32768).astype(jnp.int16)
        return carry

    lax.fori_loop(0, (nchunks + grp - 1) // grp, score_body, 0)

    i16 = jnp.int16
    pack = 16
    one16 = jnp.ones((pack, tq), i16)
    zero16 = jnp.zeros((pack, tq), i16)

    def count_ge(ref, cand):
        cand_b = jnp.broadcast_to(cand, (pack, tq)).astype(i16)

        def body(j, acc):
            x = ref[j]
            for g in range(tk // pack):
                acc = acc + jnp.where(x[g * pack:(g + 1) * pack, :] >= cand_b, one16, zero16)
            return acc

        acc = lax.fori_loop(0, nchunks, body, zero16)
        return jnp.sum(acc.astype(jnp.int32).astype(F32), axis=0, keepdims=True)

    def kth_largest(ref, rank):
        lo0 = jnp.full((1, tq), -32768, jnp.int32)
        hi0 = jnp.full((1, tq), 32767, jnp.int32)

        def body(_, carry):
            lo, hi, above = carry
            mid = lo + lax.shift_right_logical(hi - lo + 1, jnp.int32(1))
            cnt = count_ge(ref, mid)
            ok = cnt >= rank
            return jnp.where(ok, mid, lo), jnp.where(ok, hi, mid - 1), jnp.where(ok, above, cnt)

        lo, _, above = lax.fori_loop(0, 16, body, (lo0, hi0, jnp.zeros((1, tq), F32)))
        return lo, above

    t_hi, above = kth_largest(hi_ref, jnp.full((1, tq), float(TOP_K), F32))
    t_hi_b = jnp.broadcast_to(t_hi, (tk, tq)).astype(i16)

    def narrow_body(j, carry):
        lo_ref[j] = jnp.where(hi_ref[j] == t_hi_b, lo_ref[j], jnp.full((tk, tq), -32768, i16))
        return carry

    lax.fori_loop(0, nchunks, narrow_body, 0)
    t_lo, _ = kth_largest(lo_ref, float(TOP_K) - above)
    key = jnp.bitwise_or(jnp.left_shift(t_hi, 16), t_lo + 32768)
    thr = jnp.broadcast_to(_key_to_float(key), (tk, tq))

    def word_body(j, word):
        keep = jnp.logical_and(sc_ref[j] >= thr, key_pos + j * tk <= qry_pos)
        return jnp.bitwise_or(word, jnp.where(keep, jnp.left_shift(jnp.int32(1), j), 0))

    word_t = lax.fori_loop(0, nchunks, word_body, jnp.zeros((tk, tq), jnp.int32))
    o_ref[...] = jnp.transpose(word_t)


def _select(attn_grp, aux):
    tq = tk = SEL_T
    nq, nk = SEQ // tq, SEQ // tk
    return pl.pallas_call(
        _select_kernel,
        grid=(nq,),
        in_specs=[
            pl.BlockSpec((tq, N_IDX_HEADS * IDX_DIM), lambda i: (i, 3)),
            pl.BlockSpec((SEQ, LANE), lambda i: (0, AUX_KIDX // LANE)),
        ],
        out_specs=pl.BlockSpec((tq, tk), lambda i: (i, 0)),
        out_shape=jax.ShapeDtypeStruct((SEQ, tk), jnp.int32),
        scratch_shapes=[
            pltpu.VMEM((SEQ, IDX_DIM), BF16),
            pltpu.VMEM((nk, tk, tq), F32),
            pltpu.VMEM((nk, tk, tq), jnp.int16),
            pltpu.VMEM((nk, tk, tq), jnp.int16),
        ],
        compiler_params=_cparams("arbitrary"),
        name="select",
    )(attn_grp, aux)


def _attn_kernel(q_ref, k_ref, v_ref, w_ref, band_ref, o_ref, sa_ref, sb_ref, acc_ref):
    i = pl.program_id(1)
    tq, tk, unroll = ATT_TQ, ATT_TK, ATT_UNROLL
    dh = ATTN_HEAD_DIM
    c = (ATTN_HEAD_DIM ** -0.5) * LOG2E
    r = tq // tk
    ngrp = (r * (i + 1) + unroll - 1) // unroll
    ones = jnp.ones((unroll * tk, LANE), BF16)

    def logits(g, s_ref, m_prev):
        mx = jnp.full((tq, LANE), MASK_NEG, F32)
        for u in range(unroll):
            j = unroll * g + u
            kt = k_ref[pl.ds(pl.multiple_of(j * tk, tk), tk), :]
            s2 = lax.dot_general(q_ref[...], kt, _NT, preferred_element_type=F32) * c
            s2 = s2 + band_ref[jnp.clip(r * i + (r - 1) - j, 0, N_BAND - 1)]
            keep = jnp.bitwise_and(w_ref[...], jnp.left_shift(jnp.int32(1), j)) != 0
            s2 = jnp.where(keep, s2, MASK_NEG)
            s_ref[u] = s2
            for cc in range(tk // LANE):
                mx = jnp.maximum(mx, s2[:, cc * LANE:(cc + 1) * LANE])
        return jnp.maximum(m_prev, jnp.max(mx, axis=1, keepdims=True))

    def values(g, s_ref, m_prev, m_cur):
        m_b = jnp.broadcast_to(m_cur, (tq, tk))
        p = jnp.concatenate([jnp.exp2(s_ref[u] - m_b).astype(BF16) for u in range(unroll)], axis=1)
        vt = v_ref[pl.ds(pl.multiple_of(g * (unroll * tk), unroll * tk), unroll * tk), :]
        pv = jnp.dot(p, jnp.concatenate([vt, ones], axis=1), preferred_element_type=F32)
        acc_ref[...] = acc_ref[...] * jnp.exp2(m_prev - m_cur) + pv

    acc_ref[...] = jnp.zeros(acc_ref.shape, F32)
    m_none = jnp.full((tq, 1), MASK_NEG, F32)
    m_0 = logits(0, sa_ref, m_none)
    npair = (ngrp - 1) // 2

    def pair(t, carry):
        m_prev, m_cur = carry
        m_1 = logits(2 * t + 1, sb_ref, m_cur)
        values(2 * t, sa_ref, m_prev, m_cur)
        m_2 = logits(2 * t + 2, sa_ref, m_1)
        values(2 * t + 1, sb_ref, m_cur, m_1)
        return m_1, m_2

    m_prev, m_cur = lax.fori_loop(0, npair, pair, (m_none, m_0))
    last = 2 * npair

    @pl.when(ngrp - 1 == last)
    def _():
        values(last, sa_ref, m_prev, m_cur)

    @pl.when(ngrp - 1 > last)
    def _():
        m_1 = logits(last + 1, sb_ref, m_cur)
        values(last, sa_ref, m_prev, m_cur)
        values(last + 1, sb_ref, m_cur, m_1)

    o_ref[...] = acc_ref[:, :dh] / acc_ref[:, dh:dh + 1]


def _attention(attn_grp, words, band):
    tq, tk = ATT_TQ, ATT_TK
    nq, nk = SEQ // tq, SEQ // tk
    dh = ATTN_HEAD_DIM
    nh = N_ATTN_HEADS
    return pl.pallas_call(
        _attn_kernel,
        grid=(nh, nq),
        in_specs=[
            pl.BlockSpec((tq, dh), lambda h, i: (i, h)),
            pl.BlockSpec((SEQ, dh), lambda h, i: (0, nh + h)),
            pl.BlockSpec((SEQ, dh), lambda h, i: (0, 2 * nh + h)),
            pl.BlockSpec((tq, tk), lambda h, i: (i, 0)),
            pl.BlockSpec((N_BAND, None, tq, tk), lambda h, i: (0, h, 0, 0)),
        ],
        out_specs=pl.BlockSpec((tq, dh), lambda h, i: (i, h)),
        out_shape=jax.ShapeDtypeStruct((SEQ, ATTN_WIDTH), F32),
        scratch_shapes=[pltpu.VMEM((ATT_UNROLL, tq, tk), F32), pltpu.VMEM((ATT_UNROLL, tq, tk), F32),
                        pltpu.VMEM((tq, dh + LANE), F32)],
        compiler_params=_cparams("parallel", "parallel"),
        name="attention",
    )(attn_grp, attn_grp, attn_grp, words, band)


POOL_HALO = 16
CONV_HALO = 8


def _block_diag_dot(x, w_ref):
    outs = []
    for g in range(w_ref.shape[0]):
        xs = x[:, g * LRU_BLOCK:(g + 1) * LRU_BLOCK].astype(BF16)
        outs.append(jnp.dot(xs, w_ref[g], preferred_element_type=F32))
    return jnp.concatenate(outs, axis=1)


def _pool_lru_kernel(pin_ref, lin_ref, gate_ref, pw_ref, ps_ref, cw_ref, cb_ref, rw_ref, rb_ref,
                     iw_ref, ib_ref, lam_ref, pool_o, lru_o, pprev_ref, lprev_ref, h_ref, *, T):
    i = pl.program_id(0)

    @pl.when(i == 0)
    def _():
        pprev_ref[...] = jnp.zeros(pprev_ref.shape, F32)
        lprev_ref[...] = jnp.zeros(lprev_ref.shape, F32)
        h_ref[...] = jnp.zeros(h_ref.shape, F32)

    t_glob = i * T + lax.broadcasted_iota(jnp.int32, (T, 1), 0)

    u = pin_ref[...]
    ext = jnp.concatenate([pprev_ref[...], u], axis=0)
    pprev_ref[...] = u[T - POOL_HALO:, :]
    p2 = ext[1:] + ext[:-1]
    p4 = p2[2:, POOL_GROUP:] + p2[:-2, POOL_GROUP:]
    p8 = p4[4:, POOL_GROUP:] + p4[:-4, POOL_GROUP:]
    p16 = p8[8:, POOL_GROUP:] + p8[:-8, POOL_GROUP:]
    wsums = (p2[15:15 + T, :POOL_GROUP], p4[13:13 + T, :POOL_GROUP],
             p8[9:9 + T, :POOL_GROUP], p16[1:1 + T, :])
    for g, win in enumerate(POOL_WINDOWS):
        gs = slice(g * POOL_GROUP, (g + 1) * POOL_GROUP)
        count = jnp.minimum(t_glob + 1, win).astype(F32)
        dlt = wsums[g] / count - u[:, gs]
        y = jnp.dot(dlt.astype(BF16), pw_ref[g], preferred_element_type=F32)
        pool_o[:, gs] = y * ps_ref[:, gs]

    x = lin_ref[...]
    lext = jnp.concatenate([lprev_ref[...], x], axis=0)
    lprev_ref[...] = x[T - CONV_HALO:, :]
    xc = cb_ref[...]
    for jj in range(CONV_WIDTH):
        off = CONV_HALO - (CONV_WIDTH - 1) + jj
        xc = xc + cw_ref[jj:jj + 1, :] * lext[off:off + T, :]
    r = jax.nn.sigmoid(_block_diag_dot(xc, rw_ref) + rb_ref[...])
    ig = jax.nn.sigmoid(_block_diag_dot(xc, iw_ref) + ib_ref[...])
    z = -lam_ref[...]
    softplus = jnp.maximum(z, 0.0) + jnp.log1p(jnp.exp(-jnp.abs(z)))
    log_a = (-LRU_C * r) * softplus
    a = jnp.exp(log_a)
    b = jnp.sqrt(-jnp.tanh(log_a) * (a * a + 1.0)) * (ig * xc)

    sh = 1
    while sh < T:
        a_sh = jnp.concatenate([jnp.ones((sh, LRU_WIDTH), F32), a[:T - sh]], axis=0)
        b_sh = jnp.concatenate([jnp.zeros((sh, LRU_WIDTH), F32), b[:T - sh]], axis=0)
        b = a * b_sh + b
        a = a * a_sh
        sh *= 2
    h = a * h_ref[...] + b
    h_ref[...] = h[T - 1:, :]

    gt = gate_ref[...]
    cdf = 0.5 * (1.0 + jnp.tanh(math.sqrt(2.0 / math.pi) * (gt + 0.044715 * (gt * gt * gt))))
    lru_o[...] = h * (gt * cdf)


def _pool_lru(aux, pool_w, pool_scale, conv_w, conv_b, r_w, r_b, i_w, i_b, lam):
    T = 256
    full = lambda a: pl.BlockSpec(a.shape, lambda i: (0,) * a.ndim)
    row = lambda v: v.reshape(1, -1)
    args = (pool_w.astype(BF16), row(pool_scale), conv_w, row(conv_b), r_w.astype(BF16), row(r_b),
            i_w.astype(BF16), row(i_b), row(lam))
    return pl.pallas_call(
        functools.partial(_pool_lru_kernel, T=T),
        grid=(SEQ // T,),
        in_specs=[
            pl.BlockSpec((T, POOL_WIDTH), lambda i: (i, AUX_POOL // POOL_WIDTH)),
            pl.BlockSpec((T, LRU_WIDTH), lambda i: (i, AUX_LRU_IN // LRU_WIDTH)),
            pl.BlockSpec((T, LRU_WIDTH), lambda i: (i, AUX_LRU_GATE // LRU_WIDTH)),
        ] + [full(a) for a in args],
        out_specs=[pl.BlockSpec((T, POOL_WIDTH), lambda i: (i, 0)),
                   pl.BlockSpec((T, LRU_WIDTH), lambda i: (i, 0))],
        out_shape=[jax.ShapeDtypeStruct((SEQ, POOL_WIDTH), F32),
                   jax.ShapeDtypeStruct((SEQ, LRU_WIDTH), F32)],
        scratch_shapes=[pltpu.VMEM((POOL_HALO, POOL_WIDTH), F32),
                        pltpu.VMEM((CONV_HALO, LRU_WIDTH), F32),
                        pltpu.VMEM((1, LRU_WIDTH), F32)],
        compiler_params=_cparams("arbitrary"),
        name="pool_lru",
    )(aux, aux, aux, *args)


def _mix_out_kernel(a_ref, p_ref, l_ref, x_ref, ga_ref, gp_ref, gl_ref, w_ref, gpost_ref, gate_ref,
                    o_ref):
    na = _rms(a_ref[...], ga_ref[...]).astype(BF16)
    npool = _rms(p_ref[...], gp_ref[...]).astype(BF16)
    nl = _rms(l_ref[...], gl_ref[...]).astype(BF16)
    p0, p1 = ATTN_WIDTH, ATTN_WIDTH + POOL_WIDTH
    y = jnp.dot(na, w_ref[0:p0, :], preferred_element_type=F32)
    y = y + jnp.dot(npool, w_ref[p0:p1, :], preferred_element_type=F32)
    y = y + jnp.dot(nl, w_ref[p1:, :], preferred_element_type=F32)
    o_ref[...] = x_ref[...] + gate_ref[...] * _rms(y, gpost_ref[...])


def _mix_out(attn, pool, lru, x, g_attn, g_pool, g_lru, w_out, layer, g_post, gate):
    tm = 256
    vec = lambda n: pl.BlockSpec((1, n), lambda i: (0, 0))
    rows = lambda n: pl.BlockSpec((tm, n), lambda i: (i, 0))
    return pl.pallas_call(
        _mix_out_kernel,
        grid=(SEQ // tm,),
        in_specs=[rows(ATTN_WIDTH), rows(POOL_WIDTH), rows(LRU_WIDTH), rows(D_MODEL),
                  vec(ATTN_WIDTH), vec(POOL_WIDTH), vec(LRU_WIDTH),
                  pl.BlockSpec((None, D_MODEL, D_MODEL), lambda i: (layer, 0, 0)),
                  vec(D_MODEL), vec(D_MODEL)],
        out_specs=rows(D_MODEL),
        out_shape=jax.ShapeDtypeStruct((SEQ, D_MODEL), F32),
        compiler_params=_cparams("parallel"),
        name="mix_out",
    )(attn, pool, lru, x, g_attn, g_pool, g_lru, w_out, g_post, gate)


def _mlp_down_kernel(u_ref, w_ref, x_ref, gpost_ref, gate_ref, o_ref, acc_ref):
    k = pl.program_id(1)

    @pl.when(k == 0)
    def _():
        acc_ref[...] = jnp.zeros(acc_ref.shape, F32)

    acc_ref[...] += jnp.dot(u_ref[...], w_ref[...], preferred_element_type=F32)

    @pl.when(k == pl.num_programs(1) - 1)
    def _():
        o_ref[...] = x_ref[...] + gate_ref[...] * _rms(acc_ref[...], gpost_ref[...])


def _mlp_down(u2, w_down, layer, x, g_post, gate):
    tm, tk = 512, 2048
    vec = pl.BlockSpec((1, D_MODEL), lambda i, k: (0, 0))
    return pl.pallas_call(
        _mlp_down_kernel,
        grid=(SEQ // tm, MLP_HIDDEN // tk),
        in_specs=[pl.BlockSpec((tm, tk), lambda i, k: (i, k)),
                  pl.BlockSpec((None, tk, D_MODEL), lambda i, k: (layer, k, 0)),
                  pl.BlockSpec((tm, D_MODEL), lambda i, k: (i, 0)),
                  vec, vec],
        out_specs=pl.BlockSpec((tm, D_MODEL), lambda i, k: (i, 0)),
        out_shape=jax.ShapeDtypeStruct((SEQ, D_MODEL), F32),
        scratch_shapes=[pltpu.VMEM((tm, D_MODEL), F32)],
        compiler_params=_cparams("parallel", "arbitrary"),
        name="mlp_down",
    )(u2, w_down, x, g_post, gate)


def kernel(x, c, positions, rel_bias, w_mod, b_mod, g_pre_mix, g_post_mix, g_pre_mlp, g_post_mlp, w_in, pool_w, pool_scale, conv_w, conv_b, gate_r_w, gate_r_b, gate_i_w, gate_i_b, lru_lambda, g_attn_out, g_pool_out, g_lru_out, w_out, w_mlp_up, w_mlp_down):
    assert x.shape == (1, SEQ, D_MODEL) and positions.shape == (1, SEQ)
    del positions
    xs = x[0]
    mod = _modulation(c, w_mod, b_mod)
    band = _bias_band(rel_bias)
    row = lambda v: v.reshape(1, -1)
    n_attn_cols = 4 * ATTN_WIDTH
    aux_lo = n_attn_cols + IDX_DIM + N_IDX_HEADS
    w_aux = jnp.concatenate(
        [w_in[:, :, aux_lo:], w_in[:, :, n_attn_cols:aux_lo],
         jnp.zeros((DEPTH, D_MODEL, AUX_WIDTH - (AUX_KIDX + IDX_DIM + N_IDX_HEADS)), F32)],
        axis=2).astype(BF16)
    w_out_b = w_out.astype(BF16)
    w_down_b = w_mlp_down.astype(BF16)
    for l in range(DEPTH):
        shift1, scale1, gate1, shift2, scale2, gate2 = [
            mod[l, :, n * D_MODEL:(n + 1) * D_MODEL] for n in range(6)]
        g_pre = row(g_pre_mix[l])
        attn_grp = _adaln_matmul(xs, g_pre, scale1, shift1, w_in, l, n_attn_cols, BF16,
                                 tm=1024, tn=1024, name="proj_attn")
        aux = _adaln_matmul(xs, g_pre, scale1, shift1, w_aux, l, AUX_WIDTH, F32,
                            tm=512, tn=AUX_WIDTH, name="proj_aux")
        words = _select(attn_grp, aux)
        attn = _attention(attn_grp, words, band)
        pool, lru = _pool_lru(aux, pool_w[l], pool_scale[l], conv_w[l], conv_b[l], gate_r_w[l],
                              gate_r_b[l], gate_i_w[l], gate_i_b[l], lru_lambda[l])
        xs = _mix_out(attn, pool, lru, xs, row(g_attn_out[l]), row(g_pool_out[l]),
                      row(g_lru_out[l]), w_out_b, l, row(g_post_mix[l]), gate1)

        u2 = _adaln_matmul(xs, row(g_pre_mlp[l]), scale2, shift2, w_mlp_up, l, MLP_HIDDEN, BF16,
                           tm=1024, tn=1024, relu2=True, name="mlp_up")
        xs = _mlp_down(u2, w_down_b, l, xs, row(g_post_mlp[l]), gate2)
    return xs[None]
```

```python
import functools
import math

import jax
import jax.numpy as jnp
from jax import lax
from jax.experimental import pallas as pl
from jax.experimental.pallas import tpu as pltpu

D_MODEL = 2048
SEQ = 8192
DEPTH = 2
ATTN_HEAD_DIM = 128
ATTN_WIDTH = 1024
N_ATTN_HEADS = 8
POOL_WIDTH = 512
POOL_WINDOWS = (2, 4, 8, 16)
POOL_GROUP = 128
LRU_WIDTH = 512
N_LRU_BLOCKS = 4
LRU_BLOCK = 128
LRU_C = 8.0
CONV_WIDTH = 4
N_IDX_HEADS = 16
IDX_DIM = 64
TOP_K = 256
N_BUCKETS = 32
MAX_DISTANCE = 128
MLP_HIDDEN = 4 * D_MODEL
NORM_EPS = 1e-6

AUX_POOL = 0
AUX_LRU_IN = 512
AUX_LRU_GATE = 1024
AUX_KIDX = 1536
AUX_WIDTH = 1664
LANE = 128

SEL_T = 256
SEL_GROUP = 2
ATT_TQ = 512
ATT_TK = SEL_T
ATT_UNROLL = 4
N_BAND = ATT_TQ // ATT_TK + 2
assert MAX_DISTANCE <= ATT_TK // 2 and SEQ // SEL_T <= 32 and SEL_T >= TOP_K
MASK_NEG = -1e30
LOG2E = math.log2(math.e)
VMEM_LIMIT = 56 * 1024 * 1024

F32 = jnp.float32
BF16 = jnp.bfloat16
_NT = (((1,), (1,)), ((), ()))


def _cparams(*sem):
    return pltpu.CompilerParams(dimension_semantics=sem, vmem_limit_bytes=VMEM_LIMIT)


def _rms(x, g):
    ms = jnp.mean(x * x, axis=-1, keepdims=True)
    return (x * lax.rsqrt(ms + NORM_EPS)) * g


def _mod_kernel(c_ref, w_ref, b_ref, o_ref):
    c = c_ref[...]
    c_act = c * jax.nn.sigmoid(c)
    o_ref[0] = jnp.sum(c_act * w_ref[0], axis=0, keepdims=True) + b_ref[0]


def _modulation(c, w_mod, b_mod):
    tn = 1024
    n = w_mod.shape[-1]
    return pl.pallas_call(
        _mod_kernel,
        grid=(DEPTH, n // tn),
        in_specs=[
            pl.BlockSpec((D_MODEL, 1), lambda l, j: (0, 0)),
            pl.BlockSpec((1, D_MODEL, tn), lambda l, j: (l, 0, j)),
            pl.BlockSpec((1, 1, tn), lambda l, j: (l, 0, j)),
        ],
        out_specs=pl.BlockSpec((1, 1, tn), lambda l, j: (l, 0, j)),
        out_shape=jax.ShapeDtypeStruct((DEPTH, 1, n), F32),
        compiler_params=_cparams("parallel", "parallel"),
        name="modulation",
    )(c.reshape(D_MODEL, 1), w_mod, b_mod.reshape(DEPTH, 1, n))


def _adaln_mm_kernel(x_ref, g_ref, sc_ref, sh_ref, w_ref, o_ref, h_ref, *, tm, relu2):
    rc = 64

    @pl.when(pl.program_id(1) == 0)
    def _():
        g = g_ref[...]
        sc1 = 1.0 + sc_ref[...]
        sh = sh_ref[...]

        def body(r, carry):
            rows = pl.ds(pl.multiple_of(r * rc, rc), rc)
            h = _rms(x_ref[rows, :], g) * sc1 + sh
            h_ref[rows, :] = h.astype(BF16)
            return carry

        lax.fori_loop(0, tm // rc, body, 0)

    acc = jnp.dot(h_ref[...], w_ref[...].astype(BF16), preferred_element_type=F32)
    if relu2:
        acc = jnp.maximum(acc, 0.0)
        acc = acc * acc
    o_ref[...] = acc.astype(o_ref.dtype)


def _adaln_matmul(x, g, scale, shift, w, layer, n, out_dtype, *, tm, tn, relu2=False, name):
    m, k = x.shape
    vec = pl.BlockSpec((1, k), lambda i, j: (0, 0))
    return pl.pallas_call(
        functools.partial(_adaln_mm_kernel, tm=tm, relu2=relu2),
        grid=(m // tm, n // tn),
        in_specs=[
            pl.BlockSpec((tm, k), lambda i, j: (i, 0)),
            vec, vec, vec,
            pl.BlockSpec((None, k, tn), lambda i, j: (layer, 0, j)),
        ],
        out_specs=pl.BlockSpec((tm, tn), lambda i, j: (i, j)),
        out_shape=jax.ShapeDtypeStruct((m, n), out_dtype),
        scratch_shapes=[pltpu.VMEM((tm, k), BF16)],
        compiler_params=_cparams("parallel", "arbitrary"),
        name=name,
    )(x, g, scale, shift, w)


def _band_kernel(rb_ref, o_ref):
    d = pl.program_id(0)
    t = lax.broadcasted_iota(jnp.int32, (ATT_TQ, ATT_TK), 0)
    s = lax.broadcasted_iota(jnp.int32, (ATT_TQ, ATT_TK), 1)
    rel = jnp.maximum((d - (ATT_TQ // ATT_TK - 1)) * ATT_TK + t - s, 0)
    max_exact = N_BUCKETS // 2
    relf = jnp.maximum(rel.astype(F32), 1.0)
    large = max_exact + (jnp.log(relf / max_exact) / math.log(MAX_DISTANCE / max_exact)
                         * (N_BUCKETS - max_exact)).astype(jnp.int32)
    large = jnp.minimum(large, N_BUCKETS - 1)
    bucket = jnp.where(rel < max_exact, rel, large)
    for h in range(N_ATTN_HEADS):
        val = jnp.zeros((ATT_TQ, ATT_TK), F32)
        for b in range(N_BUCKETS):
            val = jnp.where(bucket == b, rb_ref[b, h], val)
        o_ref[0, h] = val * LOG2E


def _bias_band(rel_bias):
    return pl.pallas_call(
        _band_kernel,
        grid=(N_BAND,),
        in_specs=[pl.BlockSpec(memory_space=pltpu.SMEM)],
        out_specs=pl.BlockSpec((1, N_ATTN_HEADS, ATT_TQ, ATT_TK), lambda d: (d, 0, 0, 0)),
        out_shape=jax.ShapeDtypeStruct((N_BAND, N_ATTN_HEADS, ATT_TQ, ATT_TK), F32),
        compiler_params=_cparams("parallel"),
        name="bias_band",
    )(rel_bias)


def _float_to_key(x):
    bits = lax.bitcast_convert_type(x, jnp.int32)
    return jnp.where(bits >= 0, bits, jnp.bitwise_xor(bits, jnp.int32(0x7FFFFFFF)))


_KEY_POS_INF = 0x7F800000
_KEY_NEG_INF = 0x807FFFFF - (1 << 32)


def _key_to_float(key):
    bits = jnp.where(key >= 0, key, jnp.bitwise_xor(key, jnp.int32(0x7FFFFFFF)))
    return lax.bitcast_convert_type(bits, F32)


def _select_kernel(qi_ref, kw_ref, o_ref, kb_ref, sc_ref, hi_ref, lo_ref):
    i = pl.program_id(0)
    tq = tk = SEL_T
    idx_scale = (N_IDX_HEADS * IDX_DIM) ** -0.5

    @pl.when(i == 0)
    def _():
        kb_ref[...] = kw_ref[:, 0:IDX_DIM].astype(BF16)

    w_t = jnp.transpose(kw_ref[pl.ds(pl.multiple_of(i * tq, tq), tq), :])

    nchunks = i + 1
    key_pos = lax.broadcasted_iota(jnp.int32, (tk, tq), 0)
    qry_pos = i * tq + lax.broadcasted_iota(jnp.int32, (tk, tq), 1)
    grp = SEL_GROUP

    def high_half(key):
        return lax.shift_right_arithmetic(key, jnp.int32(16))

    def score_body(jg, gmax):
        kc = kb_ref[pl.ds(pl.multiple_of(jg * (grp * tk), grp * tk), grp * tk), :]
        acc = jnp.zeros((grp * tk, tq), F32)
        for h in range(N_IDX_HEADS):
            d = lax.dot_general(kc, qi_ref[:, h * IDX_DIM:(h + 1) * IDX_DIM], _NT,
                                preferred_element_type=F32)
            acc = acc + w_t[IDX_DIM + h:IDX_DIM + h + 1, :] * jnp.maximum(d, 0.0)
        for u in range(grp):
            j = grp * jg + u
            score = jnp.where(key_pos + j * tk <= qry_pos, acc[u * tk:(u + 1) * tk] * idx_scale, -jnp.inf)
            sc_ref[j] = score
            gmax = jnp.maximum(gmax, score)
            key = _float_to_key(score)
            hi_ref[j] = high_half(key).astype(jnp.int16)
            lo_ref[j] = (jnp.bitwise_and(key, jnp.int32(0xFFFF)) - 32768).astype(jnp.int16)
        return gmax

    ngroups = (nchunks + grp - 1) // grp
    gmax = lax.fori_loop(0, ngroups, score_body, jnp.full((tk, tq), -jnp.inf, F32))
    t_min = high_half(_float_to_key(jnp.min(gmax, axis=0, keepdims=True)))
    t_max = high_half(_float_to_key(jnp.max(gmax, axis=0, keepdims=True)))

    i16 = jnp.int16
    pack = 16
    one16 = jnp.ones((pack, tq), i16)
    zero16 = jnp.zeros((pack, tq), i16)

    def count_ge(ref, cand):
        cand_b = jnp.broadcast_to(cand, (pack, tq)).astype(i16)

        def body(jg, acc):
            for u in range(grp):
                x = ref[grp * jg + u]
                for g in range(tk // pack):
                    acc = acc + jnp.where(x[g * pack:(g + 1) * pack, :] >= cand_b, one16, zero16)
            return acc

        acc = lax.fori_loop(0, ngroups, body, zero16)
        return jnp.sum(acc.astype(jnp.int32).astype(F32), axis=0, keepdims=True)

    def kth_largest(ref, rank, lo0, hi0, stop_on_hit):
        def unsettled(carry):
            lo, hi, _ = carry
            return jnp.max((hi - lo).astype(F32)) > 0.0

        def halve(carry):
            lo, hi, above = carry
            mid = lo + lax.shift_right_logical(hi - lo + 1, jnp.int32(1))
            cnt = count_ge(ref, mid)
            ok = cnt >= rank
            lo, hi, above = jnp.where(ok, mid, lo), jnp.where(ok, hi, mid - 1), jnp.where(ok, above, cnt)
            if stop_on_hit:
                hit = cnt == rank
                lo, hi = jnp.where(hit, mid, lo), jnp.where(hit, mid, hi)
            return lo, hi, above

        def body(carry):
            return halve(halve(carry))

        lo, _, above = lax.while_loop(unsettled, body, (lo0, hi0, jnp.zeros((1, tq), F32)))
        return lo, above

    t_hi, above = kth_largest(hi_ref, jnp.full((1, tq), float(TOP_K), F32), t_min, t_max, False)
    t_hi_b = jnp.broadcast_to(t_hi, (tk, tq)).astype(i16)

    def narrow_body(j, carry):
        lo_ref[j] = jnp.where(hi_ref[j] == t_hi_b, lo_ref[j], jnp.full((tk, tq), -32768, i16))
        return carry

    lax.fori_loop(0, grp * ngroups, narrow_body, 0)
    t_lo, _ = kth_largest(lo_ref, float(TOP_K) - above, jnp.full((1, tq), -32768, jnp.int32),
                          jnp.full((1, tq), 32767, jnp.int32), True)
    key = jnp.bitwise_or(jnp.left_shift(t_hi, 16), t_lo + 32768)
    key = jnp.clip(key, _KEY_NEG_INF, _KEY_POS_INF)
    thr = jnp.broadcast_to(_key_to_float(key), (tk, tq))

    def word_body(j, word):
        keep = jnp.logical_and(sc_ref[j] >= thr, key_pos + j * tk <= qry_pos)
        return jnp.bitwise_or(word, jnp.where(keep, jnp.left_shift(jnp.int32(1), j), 0))

    word_t = lax.fori_loop(0, nchunks, word_body, jnp.zeros((tk, tq), jnp.int32))
    o_ref[...] = jnp.transpose(word_t)


def _select(attn_grp, aux):
    tq = tk = SEL_T
    nq, nk = SEQ // tq, SEQ // tk
    return pl.pallas_call(
        _select_kernel,
        grid=(nq,),
        in_specs=[
            pl.BlockSpec((tq, N_IDX_HEADS * IDX_DIM), lambda i: (i, 3)),
            pl.BlockSpec((SEQ, LANE), lambda i: (0, AUX_KIDX // LANE)),
        ],
        out_specs=pl.BlockSpec((tq, tk), lambda i: (i, 0)),
        out_shape=jax.ShapeDtypeStruct((SEQ, tk), jnp.int32),
        scratch_shapes=[
            pltpu.VMEM((SEQ, IDX_DIM), BF16),
            pltpu.VMEM((nk, tk, tq), F32),
            pltpu.VMEM((nk, tk, tq), jnp.int16),
            pltpu.VMEM((nk, tk, tq), jnp.int16),
        ],
        compiler_params=_cparams("arbitrary"),
        name="select",
    )(attn_grp, aux)


def _attn_kernel(q_ref, k_ref, v_ref, w_ref, band_ref, o_ref, mask_ref, sa_ref, sb_ref, acc_ref):
    i = pl.program_id(0)
    tq, tk, unroll = ATT_TQ, ATT_TK, ATT_UNROLL
    dh = ATTN_HEAD_DIM
    c = (ATTN_HEAD_DIM ** -0.5) * LOG2E
    r = tq // tk
    ngrp = (r * (i + 1) + unroll - 1) // unroll
    ones = jnp.ones((unroll * tk, LANE), BF16)

    @pl.when(pl.program_id(1) == 0)
    def _():
        def expand(j, carry):
            keep = jnp.bitwise_and(w_ref[...], jnp.left_shift(jnp.int32(1), j)) != 0
            mask_ref[j] = jnp.where(keep, 0.0, MASK_NEG)
            return carry

        lax.fori_loop(0, unroll * ngrp, expand, 0)

    def logits(g, s_ref, m_prev):
        mx = jnp.full((tq, LANE), MASK_NEG, F32)
        for u in range(unroll):
            j = unroll * g + u
            kt = k_ref[pl.ds(pl.multiple_of(j * tk, tk), tk), :]
            s2 = lax.dot_general(q_ref[...], kt, _NT, preferred_element_type=F32) * c
            s2 = s2 + band_ref[jnp.clip(r * i + (r - 1) - j, 0, N_BAND - 1)] + mask_ref[j]
            s_ref[u] = s2
            for cc in range(tk // LANE):
                mx = jnp.maximum(mx, s2[:, cc * LANE:(cc + 1) * LANE])
        return jnp.maximum(m_prev, jnp.max(mx, axis=1, keepdims=True))

    def values(g, s_ref, m_prev, m_cur):
        m_b = jnp.broadcast_to(m_cur, (tq, tk))
        p = jnp.concatenate([jnp.exp2(s_ref[u] - m_b).astype(BF16) for u in range(unroll)], axis=1)
        vt = v_ref[pl.ds(pl.multiple_of(g * (unroll * tk), unroll * tk), unroll * tk), :]
        pv = jnp.dot(p, jnp.concatenate([vt, ones], axis=1), preferred_element_type=F32)
        acc_ref[...] = acc_ref[...] * jnp.exp2(m_prev - m_cur) + pv

    acc_ref[...] = jnp.zeros(acc_ref.shape, F32)
    m_none = jnp.full((tq, 1), MASK_NEG, F32)
    m_0 = logits(0, sa_ref, m_none)
    npair = (ngrp - 1) // 2

    def pair(t, carry):
        m_prev, m_cur = carry
        m_1 = logits(2 * t + 1, sb_ref, m_cur)
        values(2 * t, sa_ref, m_prev, m_cur)
        m_2 = logits(2 * t + 2, sa_ref, m_1)
        values(2 * t + 1, sb_ref, m_cur, m_1)
        return m_1, m_2

    m_prev, m_cur = lax.fori_loop(0, npair, pair, (m_none, m_0))
    last = 2 * npair

    @pl.when(ngrp - 1 == last)
    def _():
        values(last, sa_ref, m_prev, m_cur)

    @pl.when(ngrp - 1 > last)
    def _():
        m_1 = logits(last + 1, sb_ref, m_cur)
        values(last, sa_ref, m_prev, m_cur)
        values(last + 1, sb_ref, m_cur, m_1)

    o_ref[...] = acc_ref[:, :dh] / acc_ref[:, dh:dh + 1]


def _attention(attn_grp, words, band):
    tq, tk = ATT_TQ, ATT_TK
    nq, nk = SEQ // tq, SEQ // tk
    dh = ATTN_HEAD_DIM
    nh = N_ATTN_HEADS
    return pl.pallas_call(
        _attn_kernel,
        grid=(nq, nh),
        in_specs=[
            pl.BlockSpec((tq, dh), lambda i, h: (i, h)),
            pl.BlockSpec((SEQ, dh), lambda i, h: (0, nh + h)),
            pl.BlockSpec((SEQ, dh), lambda i, h: (0, 2 * nh + h)),
            pl.BlockSpec((tq, tk), lambda i, h: (i, 0)),
            pl.BlockSpec((N_BAND, None, tq, tk), lambda i, h: (0, h, 0, 0)),
        ],
        out_specs=pl.BlockSpec((tq, dh), lambda i, h: (i, h)),
        out_shape=jax.ShapeDtypeStruct((SEQ, ATTN_WIDTH), F32),
        scratch_shapes=[pltpu.VMEM((nk, tq, tk), F32),
                        pltpu.VMEM((ATT_UNROLL, tq, tk), F32), pltpu.VMEM((ATT_UNROLL, tq, tk), F32),
                        pltpu.VMEM((tq, dh + LANE), F32)],
        compiler_params=_cparams("parallel", "arbitrary"),
        name="attention",
    )(attn_grp, attn_grp, attn_grp, words, band)


POOL_HALO = 16
CONV_HALO = 8


def _block_diag_dot(x, w_ref):
    outs = []
    for g in range(w_ref.shape[0]):
        xs = x[:, g * LRU_BLOCK:(g + 1) * LRU_BLOCK].astype(BF16)
        outs.append(jnp.dot(xs, w_ref[g], preferred_element_type=F32))
    return jnp.concatenate(outs, axis=1)


def _pool_lru_kernel(pin_ref, lin_ref, gate_ref, pw_ref, ps_ref, cw_ref, cb_ref, rw_ref, rb_ref,
                     iw_ref, ib_ref, lam_ref, pool_o, lru_o, pprev_ref, lprev_ref, h_ref, *, T):
    i = pl.program_id(0)

    @pl.when(i == 0)
    def _():
        pprev_ref[...] = jnp.zeros(pprev_ref.shape, F32)
        lprev_ref[...] = jnp.zeros(lprev_ref.shape, F32)
        h_ref[...] = jnp.zeros(h_ref.shape, F32)

    t_glob = i * T + lax.broadcasted_iota(jnp.int32, (T, 1), 0)

    u = pin_ref[...]
    ext = jnp.concatenate([pprev_ref[...], u], axis=0)
    pprev_ref[...] = u[T - POOL_HALO:, :]
    p2 = ext[1:] + ext[:-1]
    p4 = p2[2:, POOL_GROUP:] + p2[:-2, POOL_GROUP:]
    p8 = p4[4:, POOL_GROUP:] + p4[:-4, POOL_GROUP:]
    p16 = p8[8:, POOL_GROUP:] + p8[:-8, POOL_GROUP:]
    wsums = (p2[15:15 + T, :POOL_GROUP], p4[13:13 + T, :POOL_GROUP],
             p8[9:9 + T, :POOL_GROUP], p16[1:1 + T, :])
    for g, win in enumerate(POOL_WINDOWS):
        gs = slice(g * POOL_GROUP, (g + 1) * POOL_GROUP)
        count = jnp.minimum(t_glob + 1, win).astype(F32)
        dlt = wsums[g] / count - u[:, gs]
        y = jnp.dot(dlt.astype(BF16), pw_ref[g], preferred_element_type=F32)
        pool_o[:, gs] = y * ps_ref[:, gs]

    x = lin_ref[...]
    lext = jnp.concatenate([lprev_ref[...], x], axis=0)
    lprev_ref[...] = x[T - CONV_HALO:, :]
    xc = cb_ref[...]
    for jj in range(CONV_WIDTH):
        off = CONV_HALO - (CONV_WIDTH - 1) + jj
        xc = xc + cw_ref[jj:jj + 1, :] * lext[off:off + T, :]
    r = jax.nn.sigmoid(_block_diag_dot(xc, rw_ref) + rb_ref[...])
    ig = jax.nn.sigmoid(_block_diag_dot(xc, iw_ref) + ib_ref[...])
    z = -lam_ref[...]
    softplus = jnp.maximum(z, 0.0) + jnp.log1p(jnp.exp(-jnp.abs(z)))
    log_a = (-LRU_C * r) * softplus
    a = jnp.exp(log_a)
    b = jnp.sqrt(-jnp.tanh(log_a) * (a * a + 1.0)) * (ig * xc)

    sh = 1
    while sh < T:
        a_sh = jnp.concatenate([jnp.ones((sh, LRU_WIDTH), F32), a[:T - sh]], axis=0)
        b_sh = jnp.concatenate([jnp.zeros((sh, LRU_WIDTH), F32), b[:T - sh]], axis=0)
        b = a * b_sh + b
        a = a * a_sh
        sh *= 2
    h = a * h_ref[...] + b
    h_ref[...] = h[T - 1:, :]

    gt = gate_ref[...]
    cdf = 0.5 * (1.0 + jnp.tanh(math.sqrt(2.0 / math.pi) * (gt + 0.044715 * (gt * gt * gt))))
    lru_o[...] = h * (gt * cdf)


def _pool_lru(aux, pool_w, pool_scale, conv_w, conv_b, r_w, r_b, i_w, i_b, lam):
    T = 256
    full = lambda a: pl.BlockSpec(a.shape, lambda i: (0,) * a.ndim)
    row = lambda v: v.reshape(1, -1)
    args = (pool_w.astype(BF16), row(pool_scale), conv_w, row(conv_b), r_w.astype(BF16), row(r_b),
            i_w.astype(BF16), row(i_b), row(lam))
    return pl.pallas_call(
        functools.partial(_pool_lru_kernel, T=T),
        grid=(SEQ // T,),
        in_specs=[
            pl.BlockSpec((T, POOL_WIDTH), lambda i: (i, AUX_POOL // POOL_WIDTH)),
            pl.BlockSpec((T, LRU_WIDTH), lambda i: (i, AUX_LRU_IN // LRU_WIDTH)),
            pl.BlockSpec((T, LRU_WIDTH), lambda i: (i, AUX_LRU_GATE // LRU_WIDTH)),
        ] + [full(a) for a in args],
        out_specs=[pl.BlockSpec((T, POOL_WIDTH), lambda i: (i, 0)),
                   pl.BlockSpec((T, LRU_WIDTH), lambda i: (i, 0))],
        out_shape=[jax.ShapeDtypeStruct((SEQ, POOL_WIDTH), F32),
                   jax.ShapeDtypeStruct((SEQ, LRU_WIDTH), F32)],
        scratch_shapes=[pltpu.VMEM((POOL_HALO, POOL_WIDTH), F32),
                        pltpu.VMEM((CONV_HALO, LRU_WIDTH), F32),
                        pltpu.VMEM((1, LRU_WIDTH), F32)],
        compiler_params=_cparams("arbitrary"),
        name="pool_lru",
    )(aux, aux, aux, *args)


def _mix_out_kernel(a_ref, p_ref, l_ref, x_ref, ga_ref, gp_ref, gl_ref, w_ref, gpost_ref, gate_ref,
                    o_ref):
    na = _rms(a_ref[...], ga_ref[...]).astype(BF16)
    npool = _rms(p_ref[...], gp_ref[...]).astype(BF16)
    nl = _rms(l_ref[...], gl_ref[...]).astype(BF16)
    p0, p1 = ATTN_WIDTH, ATTN_WIDTH + POOL_WIDTH
    y = jnp.dot(na, w_ref[0:p0, :], preferred_element_type=F32)
    y = y + jnp.dot(npool, w_ref[p0:p1, :], preferred_element_type=F32)
    y = y + jnp.dot(nl, w_ref[p1:, :], preferred_element_type=F32)
    o_ref[...] = x_ref[...] + gate_ref[...] * _rms(y, gpost_ref[...])


def _mix_out(attn, pool, lru, x, g_attn, g_pool, g_lru, w_out, layer, g_post, gate):
    tm = 256
    vec = lambda n: pl.BlockSpec((1, n), lambda i: (0, 0))
    rows = lambda n: pl.BlockSpec((tm, n), lambda i: (i, 0))
    return pl.pallas_call(
        _mix_out_kernel,
        grid=(SEQ // tm,),
        in_specs=[rows(ATTN_WIDTH), rows(POOL_WIDTH), rows(LRU_WIDTH), rows(D_MODEL),
                  vec(ATTN_WIDTH), vec(POOL_WIDTH), vec(LRU_WIDTH),
                  pl.BlockSpec((None, D_MODEL, D_MODEL), lambda i: (layer, 0, 0)),
                  vec(D_MODEL), vec(D_MODEL)],
        out_specs=rows(D_MODEL),
        out_shape=jax.ShapeDtypeStruct((SEQ, D_MODEL), F32),
        compiler_params=_cparams("parallel"),
        name="mix_out",
    )(attn, pool, lru, x, g_attn, g_pool, g_lru, w_out, g_post, gate)


def _mlp_down_kernel(u_ref, w_ref, x_ref, gpost_ref, gate_ref, o_ref, acc_ref):
    k = pl.program_id(1)

    @pl.when(k == 0)
    def _():
        acc_ref[...] = jnp.zeros(acc_ref.shape, F32)

    acc_ref[...] += jnp.dot(u_ref[...], w_ref[...], preferred_element_type=F32)

    @pl.when(k == pl.num_programs(1) - 1)
    def _():
        o_ref[...] = x_ref[...] + gate_ref[...] * _rms(acc_ref[...], gpost_ref[...])


def _mlp_down(u2, w_down, layer, x, g_post, gate):
    tm, tk = 512, 2048
    vec = pl.BlockSpec((1, D_MODEL), lambda i, k: (0, 0))
    return pl.pallas_call(
        _mlp_down_kernel,
        grid=(SEQ // tm, MLP_HIDDEN // tk),
        in_specs=[pl.BlockSpec((tm, tk), lambda i, k: (i, k)),
                  pl.BlockSpec((None, tk, D_MODEL), lambda i, k: (layer, k, 0)),
                  pl.BlockSpec((tm, D_MODEL), lambda i, k: (i, 0)),
                  vec, vec],
        out_specs=pl.BlockSpec((tm, D_MODEL), lambda i, k: (i, 0)),
        out_shape=jax.ShapeDtypeStruct((SEQ, D_MODEL), F32),
        scratch_shapes=[pltpu.VMEM((tm, D_MODEL), F32)],
        compiler_params=_cparams("parallel", "arbitrary"),
        name="mlp_down",
    )(u2, w_down, x, g_post, gate)


def kernel(x, c, positions, rel_bias, w_mod, b_mod, g_pre_mix, g_post_mix, g_pre_mlp, g_post_mlp, w_in, pool_w, pool_scale, conv_w, conv_b, gate_r_w, gate_r_b, gate_i_w, gate_i_b, lru_lambda, g_attn_out, g_pool_out, g_lru_out, w_out, w_mlp_up, w_mlp_down):
    assert x.shape == (1, SEQ, D_MODEL) and positions.shape == (1, SEQ)
    del positions
    xs = x[0]
    mod = _modulation(c, w_mod, b_mod)
    band = _bias_band(rel_bias)
    row = lambda v: v.reshape(1, -1)
    n_attn_cols = 4 * ATTN_WIDTH
    aux_lo = n_attn_cols + IDX_DIM + N_IDX_HEADS
    w_aux = jnp.concatenate(
        [w_in[:, :, aux_lo:], w_in[:, :, n_attn_cols:aux_lo],
         jnp.zeros((DEPTH, D_MODEL, AUX_WIDTH - (AUX_KIDX + IDX_DIM + N_IDX_HEADS)), F32)],
        axis=2).astype(BF16)
    w_out_b = w_out.astype(BF16)
    w_down_b = w_mlp_down.astype(BF16)
    for l in range(DEPTH):
        shift1, scale1, gate1, shift2, scale2, gate2 = [
            mod[l, :, n * D_MODEL:(n + 1) * D_MODEL] for n in range(6)]
        g_pre = row(g_pre_mix[l])
        attn_grp = _adaln_matmul(xs, g_pre, scale1, shift1, w_in, l, n_attn_cols, BF16,
                                 tm=1024, tn=1024, name="proj_attn")
        aux = _adaln_matmul(xs, g_pre, scale1, shift1, w_aux, l, AUX_WIDTH, F32,
                            tm=512, tn=AUX_WIDTH, name="proj_aux")
        words = _select(attn_grp, aux)
        attn = _attention(attn_grp, words, band)
        pool, lru = _pool_lru(aux, pool_w[l], pool_scale[l], conv_w[l], conv_b[l], gate_r_w[l],
                              gate_r_b[l], gate_i_w[l], gate_i_b[l], lru_lambda[l])
        xs = _mix_out(attn, pool, lru, xs, row(g_attn_out[l]), row(g_pool_out[l]),
                      row(g_lru_out[l]), w_out_b, l, row(g_post_mix[l]), gate1)

        u2 = _adaln_matmul(xs, row(g_pre_mlp[l]), scale2, shift2, w_mlp_up, l, MLP_HIDDEN, BF16,
                           tm=1024, tn=1024, relu2=True, name="mlp_up")
        xs = _mlp_down(u2, w_down_b, l, xs, row(g_post_mlp[l]), gate2)
    return xs[None]
```

```python
import functools
import math

import jax
import jax.numpy as jnp
from jax import lax
from jax.experimental import pallas as pl
from jax.experimental.pallas import tpu as pltpu

D_MODEL = 2048
SEQ = 8192
DEPTH = 2
ATTN_HEAD_DIM = 128
ATTN_WIDTH = 1024
N_ATTN_HEADS = 8
POOL_WIDTH = 512
POOL_WINDOWS = (2, 4, 8, 16)
POOL_GROUP = 128
LRU_WIDTH = 512
N_LRU_BLOCKS = 4
LRU_BLOCK = 128
LRU_C = 8.0
CONV_WIDTH = 4
N_IDX_HEADS = 16
IDX_DIM = 64
TOP_K = 256
N_BUCKETS = 32
MAX_DISTANCE = 128
MLP_HIDDEN = 4 * D_MODEL
NORM_EPS = 1e-6

AUX_POOL = 0
AUX_LRU_IN = 512
AUX_LRU_GATE = 1024
AUX_KIDX = 1536
AUX_WIDTH = 1664
LANE = 128

SEL_T = 256
SEL_GROUP = 2
ATT_TQ = 512
ATT_TK = SEL_T
ATT_UNROLL = 4
N_BAND = ATT_TQ // ATT_TK + 2
assert MAX_DISTANCE <= ATT_TK // 2 and SEQ // SEL_T <= 32 and SEL_T >= TOP_K
MASK_NEG = -1e30
LOG2E = math.log2(math.e)
VMEM_LIMIT = 56 * 1024 * 1024

F32 = jnp.float32
BF16 = jnp.bfloat16
_NT = (((1,), (1,)), ((), ()))


def _cparams(*sem):
    return pltpu.CompilerParams(dimension_semantics=sem, vmem_limit_bytes=VMEM_LIMIT)


def _rms(x, g):
    ms = jnp.mean(x * x, axis=-1, keepdims=True)
    return (x * lax.rsqrt(ms + NORM_EPS)) * g


def _mod_kernel(c_ref, w_ref, b_ref, o_ref):
    c = c_ref[...]
    c_act = c * jax.nn.sigmoid(c)
    o_ref[0] = jnp.sum(c_act * w_ref[0], axis=0, keepdims=True) + b_ref[0]


def _modulation(c, w_mod, b_mod):
    tn = 1024
    n = w_mod.shape[-1]
    return pl.pallas_call(
        _mod_kernel,
        grid=(DEPTH, n // tn),
        in_specs=[
            pl.BlockSpec((D_MODEL, 1), lambda l, j: (0, 0)),
            pl.BlockSpec((1, D_MODEL, tn), lambda l, j: (l, 0, j)),
            pl.BlockSpec((1, 1, tn), lambda l, j: (l, 0, j)),
        ],
        out_specs=pl.BlockSpec((1, 1, tn), lambda l, j: (l, 0, j)),
        out_shape=jax.ShapeDtypeStruct((DEPTH, 1, n), F32),
        compiler_params=_cparams("parallel", "parallel"),
        name="modulation",
    )(c.reshape(D_MODEL, 1), w_mod, b_mod.reshape(DEPTH, 1, n))


def _adaln_mm_kernel(x_ref, g_ref, sc_ref, sh_ref, w_ref, o_ref, h_ref, *, tm, relu2):
    rc = 64

    @pl.when(pl.program_id(1) == 0)
    def _():
        g = g_ref[...]
        sc1 = 1.0 + sc_ref[...]
        sh = sh_ref[...]

        def body(r, carry):
            rows = pl.ds(pl.multiple_of(r * rc, rc), rc)
            h = _rms(x_ref[rows, :], g) * sc1 + sh
            h_ref[rows, :] = h.astype(BF16)
            return carry

        lax.fori_loop(0, tm // rc, body, 0)

    acc = jnp.dot(h_ref[...], w_ref[...].astype(BF16), preferred_element_type=F32)
    if relu2:
        acc = jnp.maximum(acc, 0.0)
        acc = acc * acc
    o_ref[...] = acc.astype(o_ref.dtype)


def _adaln_matmul(x, g, scale, shift, w, layer, n, out_dtype, *, tm, tn, relu2=False, name):
    m, k = x.shape
    vec = pl.BlockSpec((1, k), lambda i, j: (0, 0))
    return pl.pallas_call(
        functools.partial(_adaln_mm_kernel, tm=tm, relu2=relu2),
        grid=(m // tm, n // tn),
        in_specs=[
            pl.BlockSpec((tm, k), lambda i, j: (i, 0)),
            vec, vec, vec,
            pl.BlockSpec((None, k, tn), lambda i, j: (layer, 0, j)),
        ],
        out_specs=pl.BlockSpec((tm, tn), lambda i, j: (i, j)),
        out_shape=jax.ShapeDtypeStruct((m, n), out_dtype),
        scratch_shapes=[pltpu.VMEM((tm, k), BF16)],
        compiler_params=_cparams("parallel", "arbitrary"),
        name=name,
    )(x, g, scale, shift, w)


def _band_kernel(rb_ref, o_ref):
    d = pl.program_id(0)
    t = lax.broadcasted_iota(jnp.int32, (ATT_TQ, ATT_TK), 0)
    s = lax.broadcasted_iota(jnp.int32, (ATT_TQ, ATT_TK), 1)
    rel = jnp.maximum((d - (ATT_TQ // ATT_TK - 1)) * ATT_TK + t - s, 0)
    max_exact = N_BUCKETS // 2
    relf = jnp.maximum(rel.astype(F32), 1.0)
    large = max_exact + (jnp.log(relf / max_exact) / math.log(MAX_DISTANCE / max_exact)
                         * (N_BUCKETS - max_exact)).astype(jnp.int32)
    large = jnp.minimum(large, N_BUCKETS - 1)
    bucket = jnp.where(rel < max_exact, rel, large)
    for h in range(N_ATTN_HEADS):
        val = jnp.zeros((ATT_TQ, ATT_TK), F32)
        for b in range(N_BUCKETS):
            val = jnp.where(bucket == b, rb_ref[b, h], val)
        o_ref[0, h] = val * LOG2E


def _bias_band(rel_bias):
    return pl.pallas_call(
        _band_kernel,
        grid=(N_BAND,),
        in_specs=[pl.BlockSpec(memory_space=pltpu.SMEM)],
        out_specs=pl.BlockSpec((1, N_ATTN_HEADS, ATT_TQ, ATT_TK), lambda d: (d, 0, 0, 0)),
        out_shape=jax.ShapeDtypeStruct((N_BAND, N_ATTN_HEADS, ATT_TQ, ATT_TK), F32),
        compiler_params=_cparams("parallel"),
        name="bias_band",
    )(rel_bias)


def _float_to_key(x):
    bits = lax.bitcast_convert_type(x, jnp.int32)
    return jnp.where(bits >= 0, bits, jnp.bitwise_xor(bits, jnp.int32(0x7FFFFFFF)))


_KEY_POS_INF = 0x7F800000
_KEY_NEG_INF = 0x807FFFFF - (1 << 32)


def _key_to_float(key):
    bits = jnp.where(key >= 0, key, jnp.bitwise_xor(key, jnp.int32(0x7FFFFFFF)))
    return lax.bitcast_convert_type(bits, F32)


def _select_kernel(qi_ref, kw_ref, o_ref, kb_ref, sc_ref, hi_ref, lo_ref):
    i = pl.program_id(0)
    tq = tk = SEL_T
    idx_scale = (N_IDX_HEADS * IDX_DIM) ** -0.5

    @pl.when(i == 0)
    def _():
        kb_ref[...] = kw_ref[:, 0:IDX_DIM].astype(BF16)

    w_t = jnp.transpose(kw_ref[pl.ds(pl.multiple_of(i * tq, tq), tq), :])

    nchunks = i + 1
    key_pos = lax.broadcasted_iota(jnp.int32, (tk, tq), 0)
    qry_pos = i * tq + lax.broadcasted_iota(jnp.int32, (tk, tq), 1)
    grp = SEL_GROUP

    def high_half(key):
        return lax.shift_right_arithmetic(key, jnp.int32(16))

    def score_body(jg, gmax):
        kc = kb_ref[pl.ds(pl.multiple_of(jg * (grp * tk), grp * tk), grp * tk), :]
        acc = jnp.zeros((grp * tk, tq), F32)
        for h in range(N_IDX_HEADS):
            d = lax.dot_general(kc, qi_ref[:, h * IDX_DIM:(h + 1) * IDX_DIM], _NT,
                                preferred_element_type=F32)
            acc = acc + w_t[IDX_DIM + h:IDX_DIM + h + 1, :] * jnp.maximum(d, 0.0)
        for u in range(grp):
            j = grp * jg + u
            score = jnp.where(key_pos + j * tk <= qry_pos, acc[u * tk:(u + 1) * tk] * idx_scale, -jnp.inf)
            sc_ref[j] = score
            gmax = jnp.maximum(gmax, score)
            key = _float_to_key(score)
            hi_ref[j] = high_half(key).astype(jnp.int16)
            lo_ref[j] = (jnp.bitwise_and(key, jnp.int32(0xFFFF)) - 32768).astype(jnp.int16)
        return gmax

    ngroups = (nchunks + grp - 1) // grp
    gmax = lax.fori_loop(0, ngroups, score_body, jnp.full((tk, tq), -jnp.inf, F32))
    t_min = high_half(_float_to_key(jnp.min(gmax, axis=0, keepdims=True)))
    t_max = high_half(_float_to_key(jnp.max(gmax, axis=0, keepdims=True)))

    i16 = jnp.int16
    pack = 16
    one16 = jnp.ones((pack, tq), i16)
    zero16 = jnp.zeros((pack, tq), i16)

    def count_ge(ref, cand):
        cand_b = jnp.broadcast_to(cand, (pack, tq)).astype(i16)

        def body(jg, acc):
            parts = [acc, zero16, zero16, zero16]
            n = 0
            for u in range(grp):
                x = ref[grp * jg + u]
                for g in range(tk // pack):
                    hit = jnp.where(x[g * pack:(g + 1) * pack, :] >= cand_b, one16, zero16)
                    parts[n % 4] = parts[n % 4] + hit
                    n += 1
            return (parts[0] + parts[1]) + (parts[2] + parts[3])

        acc = lax.fori_loop(0, ngroups, body, zero16)
        return jnp.sum(acc.astype(jnp.int32).astype(F32), axis=0, keepdims=True)

    def kth_largest(ref, rank, lo0, hi0, stop_on_hit):
        def unsettled(carry):
            lo, hi, _ = carry
            return jnp.max((hi - lo).astype(F32)) > 0.0

        def halve(carry):
            lo, hi, above = carry
            mid = lo + lax.shift_right_logical(hi - lo + 1, jnp.int32(1))
            cnt = count_ge(ref, mid)
            ok = cnt >= rank
            lo, hi, above = jnp.where(ok, mid, lo), jnp.where(ok, hi, mid - 1), jnp.where(ok, above, cnt)
            if stop_on_hit:
                hit = cnt == rank
                lo, hi = jnp.where(hit, mid, lo), jnp.where(hit, mid, hi)
            return lo, hi, above

        def body(carry):
            return halve(halve(carry))

        lo, _, above = lax.while_loop(unsettled, body, (lo0, hi0, jnp.zeros((1, tq), F32)))
        return lo, above

    t_hi, above = kth_largest(hi_ref, jnp.full((1, tq), float(TOP_K), F32), t_min, t_max, False)
    t_hi_b = jnp.broadcast_to(t_hi, (tk, tq)).astype(i16)

    def narrow_body(j, carry):
        lo_ref[j] = jnp.where(hi_ref[j] == t_hi_b, lo_ref[j], jnp.full((tk, tq), -32768, i16))
        return carry

    lax.fori_loop(0, grp * ngroups, narrow_body, 0)
    t_lo, _ = kth_largest(lo_ref, float(TOP_K) - above, jnp.full((1, tq), -32768, jnp.int32),
                          jnp.full((1, tq), 32767, jnp.int32), True)
    key = jnp.bitwise_or(jnp.left_shift(t_hi, 16), t_lo + 32768)
    key = jnp.clip(key, _KEY_NEG_INF, _KEY_POS_INF)
    thr = jnp.broadcast_to(_key_to_float(key), (tk, tq))

    def word_body(j, word):
        keep = jnp.logical_and(sc_ref[j] >= thr, key_pos + j * tk <= qry_pos)
        return jnp.bitwise_or(word, jnp.where(keep, jnp.left_shift(jnp.int32(1), j), 0))

    word_t = lax.fori_loop(0, nchunks, word_body, jnp.zeros((tk, tq), jnp.int32))
    o_ref[...] = jnp.transpose(word_t)


def _select(attn_grp, aux):
    tq = tk = SEL_T
    nq, nk = SEQ // tq, SEQ // tk
    return pl.pallas_call(
        _select_kernel,
        grid=(nq,),
        in_specs=[
            pl.BlockSpec((tq, N_IDX_HEADS * IDX_DIM), lambda i: (i, 3)),
            pl.BlockSpec((SEQ, LANE), lambda i: (0, AUX_KIDX // LANE)),
        ],
        out_specs=pl.BlockSpec((tq, tk), lambda i: (i, 0)),
        out_shape=jax.ShapeDtypeStruct((SEQ, tk), jnp.int32),
        scratch_shapes=[
            pltpu.VMEM((SEQ, IDX_DIM), BF16),
            pltpu.VMEM((nk, tk, tq), F32),
            pltpu.VMEM((nk, tk, tq), jnp.int16),
            pltpu.VMEM((nk, tk, tq), jnp.int16),
        ],
        compiler_params=_cparams("arbitrary"),
        name="select",
    )(attn_grp, aux)


def _attn_kernel(q_ref, k_ref, v_ref, w_ref, band_ref, o_ref, mask_ref, sa_ref, sb_ref, acc_ref):
    i = pl.program_id(0)
    tq, tk, unroll = ATT_TQ, ATT_TK, ATT_UNROLL
    dh = ATTN_HEAD_DIM
    c = (ATTN_HEAD_DIM ** -0.5) * LOG2E
    r = tq // tk
    ngrp = (r * (i + 1) + unroll - 1) // unroll
    ones = jnp.ones((unroll * tk, LANE), BF16)

    @pl.when(pl.program_id(1) == 0)
    def _():
        def expand(j, carry):
            keep = jnp.bitwise_and(w_ref[...], jnp.left_shift(jnp.int32(1), j)) != 0
            mask_ref[j] = jnp.where(keep, 0.0, MASK_NEG)
            return carry

        lax.fori_loop(0, unroll * ngrp, expand, 0)

    def logits(g, s_ref, m_prev):
        mx = jnp.full((tq, LANE), MASK_NEG, F32)
        for u in range(unroll):
            j = unroll * g + u
            kt = k_ref[pl.ds(pl.multiple_of(j * tk, tk), tk), :]
            s2 = lax.dot_general(q_ref[...], kt, _NT, preferred_element_type=F32) * c
            s2 = s2 + band_ref[jnp.clip(r * i + (r - 1) - j, 0, N_BAND - 1)] + mask_ref[j]
            s_ref[u] = s2
            for cc in range(tk // LANE):
                mx = jnp.maximum(mx, s2[:, cc * LANE:(cc + 1) * LANE])
        return jnp.maximum(m_prev, jnp.max(mx, axis=1, keepdims=True))

    def values(g, s_ref, m_prev, m_cur):
        m_b = jnp.broadcast_to(m_cur, (tq, tk))
        p = jnp.concatenate([jnp.exp2(s_ref[u] - m_b).astype(BF16) for u in range(unroll)], axis=1)
        vt = v_ref[pl.ds(pl.multiple_of(g * (unroll * tk), unroll * tk), unroll * tk), :]
        pv = jnp.dot(p, jnp.concatenate([vt, ones], axis=1), preferred_element_type=F32)
        acc_ref[...] = acc_ref[...] * jnp.exp2(m_prev - m_cur) + pv

    acc_ref[...] = jnp.zeros(acc_ref.shape, F32)
    m_none = jnp.full((tq, 1), MASK_NEG, F32)
    m_0 = logits(0, sa_ref, m_none)
    npair = (ngrp - 1) // 2

    def pair(t, carry):
        m_prev, m_cur = carry
        m_1 = logits(2 * t + 1, sb_ref, m_cur)
        values(2 * t, sa_ref, m_prev, m_cur)
        m_2 = logits(2 * t + 2, sa_ref, m_1)
        values(2 * t + 1, sb_ref, m_cur, m_1)
        return m_1, m_2

    m_prev, m_cur = lax.fori_loop(0, npair, pair, (m_none, m_0))
    last = 2 * npair

    @pl.when(ngrp - 1 == last)
    def _():
        values(last, sa_ref, m_prev, m_cur)

    @pl.when(ngrp - 1 > last)
    def _():
        m_1 = logits(last + 1, sb_ref, m_cur)
        values(last, sa_ref, m_prev, m_cur)
        values(last + 1, sb_ref, m_cur, m_1)

    o_ref[...] = acc_ref[:, :dh] / acc_ref[:, dh:dh + 1]


def _attention(attn_grp, words, band):
    tq, tk = ATT_TQ, ATT_TK
    nq, nk = SEQ // tq, SEQ // tk
    dh = ATTN_HEAD_DIM
    nh = N_ATTN_HEADS
    return pl.pallas_call(
        _attn_kernel,
        grid=(nq, nh),
        in_specs=[
            pl.BlockSpec((tq, dh), lambda i, h: (i, h)),
            pl.BlockSpec((SEQ, dh), lambda i, h: (0, nh + h)),
            pl.BlockSpec((SEQ, dh), lambda i, h: (0, 2 * nh + h)),
            pl.BlockSpec((tq, tk), lambda i, h: (i, 0)),
            pl.BlockSpec((N_BAND, None, tq, tk), lambda i, h: (0, h, 0, 0)),
        ],
        out_specs=pl.BlockSpec((tq, dh), lambda i, h: (i, h)),
        out_shape=jax.ShapeDtypeStruct((SEQ, ATTN_WIDTH), F32),
        scratch_shapes=[pltpu.VMEM((nk, tq, tk), F32),
                        pltpu.VMEM((ATT_UNROLL, tq, tk), F32), pltpu.VMEM((ATT_UNROLL, tq, tk), F32),
                        pltpu.VMEM((tq, dh + LANE), F32)],
        compiler_params=_cparams("parallel", "arbitrary"),
        name="attention",
    )(attn_grp, attn_grp, attn_grp, words, band)


POOL_HALO = 16
CONV_HALO = 8


def _block_diag_dot(x, w_ref):
    outs = []
    for g in range(w_ref.shape[0]):
        xs = x[:, g * LRU_BLOCK:(g + 1) * LRU_BLOCK].astype(BF16)
        outs.append(jnp.dot(xs, w_ref[g], preferred_element_type=F32))
    return jnp.concatenate(outs, axis=1)


def _pool_lru_kernel(pin_ref, lin_ref, gate_ref, pw_ref, ps_ref, cw_ref, cb_ref, rw_ref, rb_ref,
                     iw_ref, ib_ref, lam_ref, pool_o, lru_o, pprev_ref, lprev_ref, h_ref, *, T):
    i = pl.program_id(0)

    @pl.when(i == 0)
    def _():
        pprev_ref[...] = jnp.zeros(pprev_ref.shape, F32)
        lprev_ref[...] = jnp.zeros(lprev_ref.shape, F32)
        h_ref[...] = jnp.zeros(h_ref.shape, F32)

    t_glob = i * T + lax.broadcasted_iota(jnp.int32, (T, 1), 0)

    u = pin_ref[...]
    ext = jnp.concatenate([pprev_ref[...], u], axis=0)
    pprev_ref[...] = u[T - POOL_HALO:, :]
    p2 = ext[1:] + ext[:-1]
    p4 = p2[2:, POOL_GROUP:] + p2[:-2, POOL_GROUP:]
    p8 = p4[4:, POOL_GROUP:] + p4[:-4, POOL_GROUP:]
    p16 = p8[8:, POOL_GROUP:] + p8[:-8, POOL_GROUP:]
    wsums = (p2[15:15 + T, :POOL_GROUP], p4[13:13 + T, :POOL_GROUP],
             p8[9:9 + T, :POOL_GROUP], p16[1:1 + T, :])
    for g, win in enumerate(POOL_WINDOWS):
        gs = slice(g * POOL_GROUP, (g + 1) * POOL_GROUP)
        count = jnp.minimum(t_glob + 1, win).astype(F32)
        dlt = wsums[g] / count - u[:, gs]
        y = jnp.dot(dlt.astype(BF16), pw_ref[g], preferred_element_type=F32)
        pool_o[:, gs] = y * ps_ref[:, gs]

    x = lin_ref[...]
    lext = jnp.concatenate([lprev_ref[...], x], axis=0)
    lprev_ref[...] = x[T - CONV_HALO:, :]
    xc = cb_ref[...]
    for jj in range(CONV_WIDTH):
        off = CONV_HALO - (CONV_WIDTH - 1) + jj
        xc = xc + cw_ref[jj:jj + 1, :] * lext[off:off + T, :]
    r = jax.nn.sigmoid(_block_diag_dot(xc, rw_ref) + rb_ref[...])
    ig = jax.nn.sigmoid(_block_diag_dot(xc, iw_ref) + ib_ref[...])
    z = -lam_ref[...]
    softplus = jnp.maximum(z, 0.0) + jnp.log1p(jnp.exp(-jnp.abs(z)))
    log_a = (-LRU_C * r) * softplus
    a = jnp.exp(log_a)
    b = jnp.sqrt(-jnp.tanh(log_a) * (a * a + 1.0)) * (ig * xc)

    sh = 1
    while sh < T:
        a_sh = jnp.concatenate([jnp.ones((sh, LRU_WIDTH), F32), a[:T - sh]], axis=0)
        b_sh = jnp.concatenate([jnp.zeros((sh, LRU_WIDTH), F32), b[:T - sh]], axis=0)
        b = a * b_sh + b
        a = a * a_sh
        sh *= 2
    h = a * h_ref[...] + b
    h_ref[...] = h[T - 1:, :]

    gt = gate_ref[...]
    cdf = 0.5 * (1.0 + jnp.tanh(math.sqrt(2.0 / math.pi) * (gt + 0.044715 * (gt * gt * gt))))
    lru_o[...] = h * (gt * cdf)


def _pool_lru(aux, pool_w, pool_scale, conv_w, conv_b, r_w, r_b, i_w, i_b, lam):
    T = 256
    full = lambda a: pl.BlockSpec(a.shape, lambda i: (0,) * a.ndim)
    row = lambda v: v.reshape(1, -1)
    args = (pool_w.astype(BF16), row(pool_scale), conv_w, row(conv_b), r_w.astype(BF16), row(r_b),
            i_w.astype(BF16), row(i_b), row(lam))
    return pl.pallas_call(
        functools.partial(_pool_lru_kernel, T=T),
        grid=(SEQ // T,),
        in_specs=[
            pl.BlockSpec((T, POOL_WIDTH), lambda i: (i, AUX_POOL // POOL_WIDTH)),
            pl.BlockSpec((T, LRU_WIDTH), lambda i: (i, AUX_LRU_IN // LRU_WIDTH)),
            pl.BlockSpec((T, LRU_WIDTH), lambda i: (i, AUX_LRU_GATE // LRU_WIDTH)),
        ] + [full(a) for a in args],
        out_specs=[pl.BlockSpec((T, POOL_WIDTH), lambda i: (i, 0)),
                   pl.BlockSpec((T, LRU_WIDTH), lambda i: (i, 0))],
        out_shape=[jax.ShapeDtypeStruct((SEQ, POOL_WIDTH), F32),
                   jax.ShapeDtypeStruct((SEQ, LRU_WIDTH), F32)],
        scratch_shapes=[pltpu.VMEM((POOL_HALO, POOL_WIDTH), F32),
                        pltpu.VMEM((CONV_HALO, LRU_WIDTH), F32),
                        pltpu.VMEM((1, LRU_WIDTH), F32)],
        compiler_params=_cparams("arbitrary"),
        name="pool_lru",
    )(aux, aux, aux, *args)


def _mix_out_kernel(a_ref, p_ref, l_ref, x_ref, ga_ref, gp_ref, gl_ref, w_ref, gpost_ref, gate_ref,
                    o_ref):
    na = _rms(a_ref[...], ga_ref[...]).astype(BF16)
    npool = _rms(p_ref[...], gp_ref[...]).astype(BF16)
    nl = _rms(l_ref[...], gl_ref[...]).astype(BF16)
    p0, p1 = ATTN_WIDTH, ATTN_WIDTH + POOL_WIDTH
    y = jnp.dot(na, w_ref[0:p0, :], preferred_element_type=F32)
    y = y + jnp.dot(npool, w_ref[p0:p1, :], preferred_element_type=F32)
    y = y + jnp.dot(nl, w_ref[p1:, :], preferred_element_type=F32)
    o_ref[...] = x_ref[...] + gate_ref[...] * _rms(y, gpost_ref[...])


def _mix_out(attn, pool, lru, x, g_attn, g_pool, g_lru, w_out, layer, g_post, gate):
    tm = 256
    vec = lambda n: pl.BlockSpec((1, n), lambda i: (0, 0))
    rows = lambda n: pl.BlockSpec((tm, n), lambda i: (i, 0))
    return pl.pallas_call(
        _mix_out_kernel,
        grid=(SEQ // tm,),
        in_specs=[rows(ATTN_WIDTH), rows(POOL_WIDTH), rows(LRU_WIDTH), rows(D_MODEL),
                  vec(ATTN_WIDTH), vec(POOL_WIDTH), vec(LRU_WIDTH),
                  pl.BlockSpec((None, D_MODEL, D_MODEL), lambda i: (layer, 0, 0)),
                  vec(D_MODEL), vec(D_MODEL)],
        out_specs=rows(D_MODEL),
        out_shape=jax.ShapeDtypeStruct((SEQ, D_MODEL), F32),
        compiler_params=_cparams("parallel"),
        name="mix_out",
    )(attn, pool, lru, x, g_attn, g_pool, g_lru, w_out, g_post, gate)


def _mlp_down_kernel(u_ref, w_ref, x_ref, gpost_ref, gate_ref, o_ref):
    y = jnp.dot(u_ref[...], w_ref[...], preferred_element_type=F32)
    o_ref[...] = x_ref[...] + gate_ref[...] * _rms(y, gpost_ref[...])


def _mlp_down(u2, w_down, layer, x, g_post, gate):
    tm = 256
    vec = pl.BlockSpec((1, D_MODEL), lambda i: (0, 0))
    return pl.pallas_call(
        _mlp_down_kernel,
        grid=(SEQ // tm,),
        in_specs=[pl.BlockSpec((tm, MLP_HIDDEN), lambda i: (i, 0)),
                  pl.BlockSpec((None, MLP_HIDDEN, D_MODEL), lambda i: (layer, 0, 0),
                               pipeline_mode=pl.Buffered(1)),
                  pl.BlockSpec((tm, D_MODEL), lambda i: (i, 0)),
                  vec, vec],
        out_specs=pl.BlockSpec((tm, D_MODEL), lambda i: (i, 0)),
        out_shape=jax.ShapeDtypeStruct((SEQ, D_MODEL), F32),
        compiler_params=_cparams("parallel"),
        name="mlp_down",
    )(u2, w_down, x, g_post, gate)


def kernel(x, c, positions, rel_bias, w_mod, b_mod, g_pre_mix, g_post_mix, g_pre_mlp, g_post_mlp, w_in, pool_w, pool_scale, conv_w, conv_b, gate_r_w, gate_r_b, gate_i_w, gate_i_b, lru_lambda, g_attn_out, g_pool_out, g_lru_out, w_out, w_mlp_up, w_mlp_down):
    assert x.shape == (1, SEQ, D_MODEL) and positions.shape == (1, SEQ)
    del positions
    xs = x[0]
    mod = _modulation(c, w_mod, b_mod)
    band = _bias_band(rel_bias)
    row = lambda v: v.reshape(1, -1)
    n_attn_cols = 4 * ATTN_WIDTH
    aux_lo = n_attn_cols + IDX_DIM + N_IDX_HEADS
    w_attn = w_in[:, :, :n_attn_cols].astype(BF16)
    w_aux = jnp.concatenate(
        [w_in[:, :, aux_lo:], w_in[:, :, n_attn_cols:aux_lo],
         jnp.zeros((DEPTH, D_MODEL, AUX_WIDTH - (AUX_KIDX + IDX_DIM + N_IDX_HEADS)), F32)],
        axis=2).astype(BF16)
    w_out_b = w_out.astype(BF16)
    w_down_b = w_mlp_down.astype(BF16)
    for l in range(DEPTH):
        shift1, scale1, gate1, shift2, scale2, gate2 = [
            mod[l, :, n * D_MODEL:(n + 1) * D_MODEL] for n in range(6)]
        g_pre = row(g_pre_mix[l])
        attn_grp = _adaln_matmul(xs, g_pre, scale1, shift1, w_attn, l, n_attn_cols, BF16,
                                 tm=1024, tn=1024, name="proj_attn")
        aux = _adaln_matmul(xs, g_pre, scale1, shift1, w_aux, l, AUX_WIDTH, F32,
                            tm=512, tn=AUX_WIDTH, name="proj_aux")
        words = _select(attn_grp, aux)
        attn = _attention(attn_grp, words, band)
        pool, lru = _pool_lru(aux, pool_w[l], pool_scale[l], conv_w[l], conv_b[l], gate_r_w[l],
                              gate_r_b[l], gate_i_w[l], gate_i_b[l], lru_lambda[l])
        xs = _mix_out(attn, pool, lru, xs, row(g_attn_out[l]), row(g_pool_out[l]),
                      row(g_lru_out[l]), w_out_b, l, row(g_post_mix[l]), gate1)

        u2 = _adaln_matmul(xs, row(g_pre_mlp[l]), scale2, shift2, w_mlp_up, l, MLP_HIDDEN, BF16,
                           tm=1024, tn=1024, relu2=True, name="mlp_up")
        xs = _mlp_down(u2, w_down_b, l, xs, row(g_post_mlp[l]), gate2)
    return xs[None]
```

```python
import functools
import math

import jax
import jax.numpy as jnp
from jax import lax
from jax.experimental import pallas as pl
from jax.experimental.pallas import tpu as pltpu

D_MODEL = 2048
SEQ = 8192
DEPTH = 2
ATTN_HEAD_DIM = 128
ATTN_WIDTH = 1024
N_ATTN_HEADS = 8
POOL_WIDTH = 512
POOL_WINDOWS = (2, 4, 8, 16)
POOL_GROUP = 128
LRU_WIDTH = 512
N_LRU_BLOCKS = 4
LRU_BLOCK = 128
LRU_C = 8.0
CONV_WIDTH = 4
N_IDX_HEADS = 16
IDX_DIM = 64
TOP_K = 256
N_BUCKETS = 32
MAX_DISTANCE = 128
MLP_HIDDEN = 4 * D_MODEL
NORM_EPS = 1e-6

AUX_POOL = 0
AUX_LRU_IN = 512
AUX_LRU_GATE = 1024
AUX_KIDX = 1536
AUX_WIDTH = 1664
LANE = 128

SEL_T = 256
SEL_GROUP = 2
ATT_TQ = 512
ATT_TK = SEL_T
ATT_UNROLL = 4
N_BAND = ATT_TQ // ATT_TK + 2
assert MAX_DISTANCE <= ATT_TK // 2 and SEQ // SEL_T <= 32 and SEL_T >= TOP_K
MASK_NEG = -1e30
LOG2E = math.log2(math.e)
VMEM_LIMIT = 56 * 1024 * 1024

F32 = jnp.float32
BF16 = jnp.bfloat16
_NT = (((1,), (1,)), ((), ()))


def _cparams(*sem):
    return pltpu.CompilerParams(dimension_semantics=sem, vmem_limit_bytes=VMEM_LIMIT)


def _rms(x, g):
    ms = jnp.mean(x * x, axis=-1, keepdims=True)
    return (x * lax.rsqrt(ms + NORM_EPS)) * g


def _mod_kernel(c_ref, w_ref, b_ref, o_ref):
    c = c_ref[...]
    c_act = c * jax.nn.sigmoid(c)
    o_ref[0] = jnp.sum(c_act * w_ref[0], axis=0, keepdims=True) + b_ref[0]


def _modulation(c, w_mod, b_mod):
    tn = 1024
    n = w_mod.shape[-1]
    return pl.pallas_call(
        _mod_kernel,
        grid=(DEPTH, n // tn),
        in_specs=[
            pl.BlockSpec((D_MODEL, 1), lambda l, j: (0, 0)),
            pl.BlockSpec((1, D_MODEL, tn), lambda l, j: (l, 0, j)),
            pl.BlockSpec((1, 1, tn), lambda l, j: (l, 0, j)),
        ],
        out_specs=pl.BlockSpec((1, 1, tn), lambda l, j: (l, 0, j)),
        out_shape=jax.ShapeDtypeStruct((DEPTH, 1, n), F32),
        compiler_params=_cparams("parallel", "parallel"),
        name="modulation",
    )(c.reshape(D_MODEL, 1), w_mod, b_mod.reshape(DEPTH, 1, n))


def _adaln_mm_kernel(x_ref, g_ref, sc_ref, sh_ref, w_ref, o_ref, h_ref, *, tm, relu2):
    rc = 64

    @pl.when(pl.program_id(1) == 0)
    def _():
        g = g_ref[...]
        sc1 = 1.0 + sc_ref[...]
        sh = sh_ref[...]

        def body(r, carry):
            rows = pl.ds(pl.multiple_of(r * rc, rc), rc)
            h = _rms(x_ref[rows, :], g) * sc1 + sh
            h_ref[rows, :] = h.astype(BF16)
            return carry

        lax.fori_loop(0, tm // rc, body, 0)

    acc = jnp.dot(h_ref[...], w_ref[...].astype(BF16), preferred_element_type=F32)
    if relu2:
        acc = jnp.maximum(acc, 0.0)
        acc = acc * acc
    o_ref[...] = acc.astype(o_ref.dtype)


def _adaln_matmul(x, g, scale, shift, w, layer, n, out_dtype, *, tm, tn, relu2=False, name):
    m, k = x.shape
    vec = pl.BlockSpec((1, k), lambda i, j: (0, 0))
    return pl.pallas_call(
        functools.partial(_adaln_mm_kernel, tm=tm, relu2=relu2),
        grid=(m // tm, n // tn),
        in_specs=[
            pl.BlockSpec((tm, k), lambda i, j: (i, 0)),
            vec, vec, vec,
            pl.BlockSpec((None, k, tn), lambda i, j: (layer, 0, j)),
        ],
        out_specs=pl.BlockSpec((tm, tn), lambda i, j: (i, j)),
        out_shape=jax.ShapeDtypeStruct((m, n), out_dtype),
        scratch_shapes=[pltpu.VMEM((tm, k), BF16)],
        compiler_params=_cparams("parallel", "arbitrary"),
        name=name,
    )(x, g, scale, shift, w)


def _proj_kernel(x_ref, g_ref, sc_ref, sh_ref, wa_ref, wx_ref, oa_ref, ox_ref):
    h = (_rms(x_ref[...], g_ref[...]) * (1.0 + sc_ref[...]) + sh_ref[...]).astype(BF16)
    oa_ref[...] = jnp.dot(h, wa_ref[...], preferred_element_type=F32).astype(oa_ref.dtype)
    ox_ref[...] = jnp.dot(h, wx_ref[...], preferred_element_type=F32)


def _in_projection(x, g, scale, shift, w_attn, w_aux, layer):
    tm = 256
    m, k = x.shape
    na, nx = w_attn.shape[2], w_aux.shape[2]
    vec = pl.BlockSpec((1, k), lambda i: (0, 0))
    resident = lambda n: pl.BlockSpec((None, k, n), lambda i: (layer, 0, 0), pipeline_mode=pl.Buffered(1))
    return pl.pallas_call(
        _proj_kernel,
        grid=(m // tm,),
        in_specs=[pl.BlockSpec((tm, k), lambda i: (i, 0)), vec, vec, vec, resident(na), resident(nx)],
        out_specs=[pl.BlockSpec((tm, na), lambda i: (i, 0)), pl.BlockSpec((tm, nx), lambda i: (i, 0))],
        out_shape=[jax.ShapeDtypeStruct((m, na), BF16), jax.ShapeDtypeStruct((m, nx), F32)],
        compiler_params=_cparams("parallel"),
        name="in_proj",
    )(x, g, scale, shift, w_attn, w_aux)


def _band_kernel(rb_ref, o_ref):
    d = pl.program_id(0)
    t = lax.broadcasted_iota(jnp.int32, (ATT_TQ, ATT_TK), 0)
    s = lax.broadcasted_iota(jnp.int32, (ATT_TQ, ATT_TK), 1)
    rel = jnp.maximum((d - (ATT_TQ // ATT_TK - 1)) * ATT_TK + t - s, 0)
    max_exact = N_BUCKETS // 2
    relf = jnp.maximum(rel.astype(F32), 1.0)
    large = max_exact + (jnp.log(relf / max_exact) / math.log(MAX_DISTANCE / max_exact)
                         * (N_BUCKETS - max_exact)).astype(jnp.int32)
    large = jnp.minimum(large, N_BUCKETS - 1)
    bucket = jnp.where(rel < max_exact, rel, large)
    for h in range(N_ATTN_HEADS):
        val = jnp.zeros((ATT_TQ, ATT_TK), F32)
        for b in range(N_BUCKETS):
            val = jnp.where(bucket == b, rb_ref[b, h], val)
        o_ref[0, h] = val * LOG2E


def _bias_band(rel_bias):
    return pl.pallas_call(
        _band_kernel,
        grid=(N_BAND,),
        in_specs=[pl.BlockSpec(memory_space=pltpu.SMEM)],
        out_specs=pl.BlockSpec((1, N_ATTN_HEADS, ATT_TQ, ATT_TK), lambda d: (d, 0, 0, 0)),
        out_shape=jax.ShapeDtypeStruct((N_BAND, N_ATTN_HEADS, ATT_TQ, ATT_TK), F32),
        compiler_params=_cparams("parallel"),
        name="bias_band",
    )(rel_bias)


def _float_to_key(x):
    bits = lax.bitcast_convert_type(x, jnp.int32)
    return jnp.where(bits >= 0, bits, jnp.bitwise_xor(bits, jnp.int32(0x7FFFFFFF)))


_KEY_POS_INF = 0x7F800000
_KEY_NEG_INF = 0x807FFFFF - (1 << 32)


def _key_to_float(key):
    bits = jnp.where(key >= 0, key, jnp.bitwise_xor(key, jnp.int32(0x7FFFFFFF)))
    return lax.bitcast_convert_type(bits, F32)


def _select_kernel(qi_ref, kw_ref, o_ref, kb_ref, sc_ref, hi_ref, lo_ref):
    i = pl.program_id(0)
    tq = tk = SEL_T
    idx_scale = (N_IDX_HEADS * IDX_DIM) ** -0.5

    @pl.when(i == 0)
    def _():
        kb_ref[...] = kw_ref[:, 0:IDX_DIM].astype(BF16)

    w_t = jnp.transpose(kw_ref[pl.ds(pl.multiple_of(i * tq, tq), tq), :])

    nchunks = i + 1
    key_pos = lax.broadcasted_iota(jnp.int32, (tk, tq), 0)
    qry_pos = i * tq + lax.broadcasted_iota(jnp.int32, (tk, tq), 1)
    grp = SEL_GROUP

    def high_half(key):
        return lax.shift_right_arithmetic(key, jnp.int32(16))

    def score_body(jg, gmax):
        kc = kb_ref[pl.ds(pl.multiple_of(jg * (grp * tk), grp * tk), grp * tk), :]
        acc = jnp.zeros((grp * tk, tq), F32)
        for h in range(N_IDX_HEADS):
            d = lax.dot_general(kc, qi_ref[:, h * IDX_DIM:(h + 1) * IDX_DIM], _NT,
                                preferred_element_type=F32)
            acc = acc + w_t[IDX_DIM + h:IDX_DIM + h + 1, :] * jnp.maximum(d, 0.0)
        for u in range(grp):
            j = grp * jg + u
            score = jnp.where(key_pos + j * tk <= qry_pos, acc[u * tk:(u + 1) * tk] * idx_scale, -jnp.inf)
            sc_ref[j] = score
            gmax = jnp.maximum(gmax, score)
            key = _float_to_key(score)
            hi_ref[j] = high_half(key).astype(jnp.int16)
            lo_ref[j] = (jnp.bitwise_and(key, jnp.int32(0xFFFF)) - 32768).astype(jnp.int16)
        return gmax

    ngroups = (nchunks + grp - 1) // grp
    gmax = lax.fori_loop(0, ngroups, score_body, jnp.full((tk, tq), -jnp.inf, F32))
    t_min = high_half(_float_to_key(jnp.min(gmax, axis=0, keepdims=True)))
    t_max = high_half(_float_to_key(jnp.max(gmax, axis=0, keepdims=True)))

    i16 = jnp.int16
    pack = 16
    one16 = jnp.ones((pack, tq), i16)
    zero16 = jnp.zeros((pack, tq), i16)

    def count_ge(ref, cand):
        cand_b = jnp.broadcast_to(cand, (pack, tq)).astype(i16)

        def body(jg, acc):
            parts = [acc, zero16, zero16, zero16]
            n = 0
            for u in range(grp):
                x = ref[grp * jg + u]
                for g in range(tk // pack):
                    hit = jnp.where(x[g * pack:(g + 1) * pack, :] >= cand_b, one16, zero16)
                    parts[n % 4] = parts[n % 4] + hit
                    n += 1
            return (parts[0] + parts[1]) + (parts[2] + parts[3])

        acc = lax.fori_loop(0, ngroups, body, zero16)
        return jnp.sum(acc.astype(jnp.int32).astype(F32), axis=0, keepdims=True)

    def kth_largest(ref, rank, lo0, hi0, stop_on_hit):
        def unsettled(carry):
            lo, hi, _ = carry
            return jnp.max((hi - lo).astype(F32)) > 0.0

        def halve(carry):
            lo, hi, above = carry
            mid = lo + lax.shift_right_logical(hi - lo + 1, jnp.int32(1))
            cnt = count_ge(ref, mid)
            ok = cnt >= rank
            lo, hi, above = jnp.where(ok, mid, lo), jnp.where(ok, hi, mid - 1), jnp.where(ok, above, cnt)
            if stop_on_hit:
                hit = cnt == rank
                lo, hi = jnp.where(hit, mid, lo), jnp.where(hit, mid, hi)
            return lo, hi, above

        def body(carry):
            return halve(halve(carry))

        lo, _, above = lax.while_loop(unsettled, body, (lo0, hi0, jnp.zeros((1, tq), F32)))
        return lo, above

    t_hi, above = kth_largest(hi_ref, jnp.full((1, tq), float(TOP_K), F32), t_min, t_max, False)
    t_hi_b = jnp.broadcast_to(t_hi, (tk, tq)).astype(i16)

    def narrow_body(j, carry):
        lo_ref[j] = jnp.where(hi_ref[j] == t_hi_b, lo_ref[j], jnp.full((tk, tq), -32768, i16))
        return carry

    lax.fori_loop(0, grp * ngroups, narrow_body, 0)
    t_lo, _ = kth_largest(lo_ref, float(TOP_K) - above, jnp.full((1, tq), -32768, jnp.int32),
                          jnp.full((1, tq), 32767, jnp.int32), True)
    key = jnp.bitwise_or(jnp.left_shift(t_hi, 16), t_lo + 32768)
    key = jnp.clip(key, _KEY_NEG_INF, _KEY_POS_INF)
    thr = jnp.broadcast_to(_key_to_float(key), (tk, tq))

    def word_body(j, word):
        keep = jnp.logical_and(sc_ref[j] >= thr, key_pos + j * tk <= qry_pos)
        return jnp.bitwise_or(word, jnp.where(keep, jnp.left_shift(jnp.int32(1), j), 0))

    word_t = lax.fori_loop(0, nchunks, word_body, jnp.zeros((tk, tq), jnp.int32))
    o_ref[...] = jnp.transpose(word_t)


def _select(attn_grp, aux):
    tq = tk = SEL_T
    nq, nk = SEQ // tq, SEQ // tk
    return pl.pallas_call(
        _select_kernel,
        grid=(nq,),
        in_specs=[
            pl.BlockSpec((tq, N_IDX_HEADS * IDX_DIM), lambda i: (i, 3)),
            pl.BlockSpec((SEQ, LANE), lambda i: (0, AUX_KIDX // LANE)),
        ],
        out_specs=pl.BlockSpec((tq, tk), lambda i: (i, 0)),
        out_shape=jax.ShapeDtypeStruct((SEQ, tk), jnp.int32),
        scratch_shapes=[
            pltpu.VMEM((SEQ, IDX_DIM), BF16),
            pltpu.VMEM((nk, tk, tq), F32),
            pltpu.VMEM((nk, tk, tq), jnp.int16),
            pltpu.VMEM((nk, tk, tq), jnp.int16),
        ],
        compiler_params=_cparams("arbitrary"),
        name="select",
    )(attn_grp, aux)


def _attn_kernel(q_ref, k_ref, v_ref, w_ref, band_ref, o_ref, mask_ref, sa_ref, sb_ref, acc_ref):
    i = pl.program_id(0)
    tq, tk, unroll = ATT_TQ, ATT_TK, ATT_UNROLL
    dh = ATTN_HEAD_DIM
    c = (ATTN_HEAD_DIM ** -0.5) * LOG2E
    r = tq // tk
    ngrp = (r * (i + 1) + unroll - 1) // unroll
    ones = jnp.ones((unroll * tk, LANE), BF16)

    @pl.when(pl.program_id(1) == 0)
    def _():
        def expand(j, carry):
            keep = jnp.bitwise_and(w_ref[...], jnp.left_shift(jnp.int32(1), j)) != 0
            mask_ref[j] = jnp.where(keep, 0.0, MASK_NEG)
            return carry

        lax.fori_loop(0, unroll * ngrp, expand, 0)

    def logits(g, s_ref, m_prev):
        mx = jnp.full((tq, LANE), MASK_NEG, F32)
        for u in range(unroll):
            j = unroll * g + u
            kt = k_ref[pl.ds(pl.multiple_of(j * tk, tk), tk), :]
            s2 = lax.dot_general(q_ref[...], kt, _NT, preferred_element_type=F32) * c
            s2 = s2 + band_ref[jnp.clip(r * i + (r - 1) - j, 0, N_BAND - 1)] + mask_ref[j]
            s_ref[u] = s2
            for cc in range(tk // LANE):
                mx = jnp.maximum(mx, s2[:, cc * LANE:(cc + 1) * LANE])
        return jnp.maximum(m_prev, jnp.max(mx, axis=1, keepdims=True))

    def values(g, s_ref, m_prev, m_cur):
        m_b = jnp.broadcast_to(m_cur, (tq, tk))
        p = jnp.concatenate([jnp.exp2(s_ref[u] - m_b).astype(BF16) for u in range(unroll)], axis=1)
        vt = v_ref[pl.ds(pl.multiple_of(g * (unroll * tk), unroll * tk), unroll * tk), :]
        pv = jnp.dot(p, jnp.concatenate([vt, ones], axis=1), preferred_element_type=F32)
        acc_ref[...] = acc_ref[...] * jnp.exp2(m_prev - m_cur) + pv

    acc_ref[...] = jnp.zeros(acc_ref.shape, F32)
    m_none = jnp.full((tq, 1), MASK_NEG, F32)
    m_0 = logits(0, sa_ref, m_none)
    npair = (ngrp - 1) // 2

    def pair(t, carry):
        m_prev, m_cur = carry
        m_1 = logits(2 * t + 1, sb_ref, m_cur)
        values(2 * t, sa_ref, m_prev, m_cur)
        m_2 = logits(2 * t + 2, sa_ref, m_1)
        values(2 * t + 1, sb_ref, m_cur, m_1)
        return m_1, m_2

    m_prev, m_cur = lax.fori_loop(0, npair, pair, (m_none, m_0))
    last = 2 * npair

    @pl.when(ngrp - 1 == last)
    def _():
        values(last, sa_ref, m_prev, m_cur)

    @pl.when(ngrp - 1 > last)
    def _():
        m_1 = logits(last + 1, sb_ref, m_cur)
        values(last, sa_ref, m_prev, m_cur)
        values(last + 1, sb_ref, m_cur, m_1)

    o_ref[...] = acc_ref[:, :dh] / acc_ref[:, dh:dh + 1]


def _attention(attn_grp, words, band):
    tq, tk = ATT_TQ, ATT_TK
    nq, nk = SEQ // tq, SEQ // tk
    dh = ATTN_HEAD_DIM
    nh = N_ATTN_HEADS
    return pl.pallas_call(
        _attn_kernel,
        grid=(nq, nh),
        in_specs=[
            pl.BlockSpec((tq, dh), lambda i, h: (i, h)),
            pl.BlockSpec((SEQ, dh), lambda i, h: (0, nh + h)),
            pl.BlockSpec((SEQ, dh), lambda i, h: (0, 2 * nh + h)),
            pl.BlockSpec((tq, tk), lambda i, h: (i, 0)),
            pl.BlockSpec((N_BAND, None, tq, tk), lambda i, h: (0, h, 0, 0)),
        ],
        out_specs=pl.BlockSpec((tq, dh), lambda i, h: (i, h)),
        out_shape=jax.ShapeDtypeStruct((SEQ, ATTN_WIDTH), F32),
        scratch_shapes=[pltpu.VMEM((nk, tq, tk), F32),
                        pltpu.VMEM((ATT_UNROLL, tq, tk), F32), pltpu.VMEM((ATT_UNROLL, tq, tk), F32),
                        pltpu.VMEM((tq, dh + LANE), F32)],
        compiler_params=_cparams("parallel", "arbitrary"),
        name="attention",
    )(attn_grp, attn_grp, attn_grp, words, band)


POOL_HALO = 16
CONV_HALO = 8


def _block_diag_dot(x, w_ref):
    outs = []
    for g in range(w_ref.shape[0]):
        xs = x[:, g * LRU_BLOCK:(g + 1) * LRU_BLOCK].astype(BF16)
        outs.append(jnp.dot(xs, w_ref[g], preferred_element_type=F32))
    return jnp.concatenate(outs, axis=1)


def _pool_lru_kernel(pin_ref, lin_ref, gate_ref, pw_ref, ps_ref, cw_ref, cb_ref, rw_ref, rb_ref,
                     iw_ref, ib_ref, lam_ref, pool_o, lru_o, pprev_ref, lprev_ref, h_ref, *, T):
    i = pl.program_id(0)

    @pl.when(i == 0)
    def _():
        pprev_ref[...] = jnp.zeros(pprev_ref.shape, F32)
        lprev_ref[...] = jnp.zeros(lprev_ref.shape, F32)
        h_ref[...] = jnp.zeros(h_ref.shape, F32)

    t_glob = i * T + lax.broadcasted_iota(jnp.int32, (T, 1), 0)

    u = pin_ref[...]
    ext = jnp.concatenate([pprev_ref[...], u], axis=0)
    pprev_ref[...] = u[T - POOL_HALO:, :]
    p2 = ext[1:] + ext[:-1]
    p4 = p2[2:, POOL_GROUP:] + p2[:-2, POOL_GROUP:]
    p8 = p4[4:, POOL_GROUP:] + p4[:-4, POOL_GROUP:]
    p16 = p8[8:, POOL_GROUP:] + p8[:-8, POOL_GROUP:]
    wsums = (p2[15:15 + T, :POOL_GROUP], p4[13:13 + T, :POOL_GROUP],
             p8[9:9 + T, :POOL_GROUP], p16[1:1 + T, :])
    for g, win in enumerate(POOL_WINDOWS):
        gs = slice(g * POOL_GROUP, (g + 1) * POOL_GROUP)
        count = jnp.minimum(t_glob + 1, win).astype(F32)
        dlt = wsums[g] / count - u[:, gs]
        y = jnp.dot(dlt.astype(BF16), pw_ref[g], preferred_element_type=F32)
        pool_o[:, gs] = y * ps_ref[:, gs]

    x = lin_ref[...]
    lext = jnp.concatenate([lprev_ref[...], x], axis=0)
    lprev_ref[...] = x[T - CONV_HALO:, :]
    xc = cb_ref[...]
    for jj in range(CONV_WIDTH):
        off = CONV_HALO - (CONV_WIDTH - 1) + jj
        xc = xc + cw_ref[jj:jj + 1, :] * lext[off:off + T, :]
    r = jax.nn.sigmoid(_block_diag_dot(xc, rw_ref) + rb_ref[...])
    ig = jax.nn.sigmoid(_block_diag_dot(xc, iw_ref) + ib_ref[...])
    z = -lam_ref[...]
    softplus = jnp.maximum(z, 0.0) + jnp.log1p(jnp.exp(-jnp.abs(z)))
    log_a = (-LRU_C * r) * softplus
    a = jnp.exp(log_a)
    b = jnp.sqrt(-jnp.tanh(log_a) * (a * a + 1.0)) * (ig * xc)

    sh = 1
    while sh < T:
        a_sh = jnp.concatenate([jnp.ones((sh, LRU_WIDTH), F32), a[:T - sh]], axis=0)
        b_sh = jnp.concatenate([jnp.zeros((sh, LRU_WIDTH), F32), b[:T - sh]], axis=0)
        b = a * b_sh + b
        a = a * a_sh
        sh *= 2
    h = a * h_ref[...] + b
    h_ref[...] = h[T - 1:, :]

    gt = gate_ref[...]
    cdf = 0.5 * (1.0 + jnp.tanh(math.sqrt(2.0 / math.pi) * (gt + 0.044715 * (gt * gt * gt))))
    lru_o[...] = h * (gt * cdf)


def _pool_lru(aux, pool_w, pool_scale, conv_w, conv_b, r_w, r_b, i_w, i_b, lam):
    T = 256
    full = lambda a: pl.BlockSpec(a.shape, lambda i: (0,) * a.ndim)
    row = lambda v: v.reshape(1, -1)
    args = (pool_w.astype(BF16), row(pool_scale), conv_w, row(conv_b), r_w.astype(BF16), row(r_b),
            i_w.astype(BF16), row(i_b), row(lam))
    return pl.pallas_call(
        functools.partial(_pool_lru_kernel, T=T),
        grid=(SEQ // T,),
        in_specs=[
            pl.BlockSpec((T, POOL_WIDTH), lambda i: (i, AUX_POOL // POOL_WIDTH)),
            pl.BlockSpec((T, LRU_WIDTH), lambda i: (i, AUX_LRU_IN // LRU_WIDTH)),
            pl.BlockSpec((T, LRU_WIDTH), lambda i: (i, AUX_LRU_GATE // LRU_WIDTH)),
        ] + [full(a) for a in args],
        out_specs=[pl.BlockSpec((T, POOL_WIDTH), lambda i: (i, 0)),
                   pl.BlockSpec((T, LRU_WIDTH), lambda i: (i, 0))],
        out_shape=[jax.ShapeDtypeStruct((SEQ, POOL_WIDTH), F32),
                   jax.ShapeDtypeStruct((SEQ, LRU_WIDTH), F32)],
        scratch_shapes=[pltpu.VMEM((POOL_HALO, POOL_WIDTH), F32),
                        pltpu.VMEM((CONV_HALO, LRU_WIDTH), F32),
                        pltpu.VMEM((1, LRU_WIDTH), F32)],
        compiler_params=_cparams("arbitrary"),
        name="pool_lru",
    )(aux, aux, aux, *args)


def _mix_out_kernel(a_ref, p_ref, l_ref, x_ref, ga_ref, gp_ref, gl_ref, w_ref, gpost_ref, gate_ref,
                    o_ref):
    na = _rms(a_ref[...], ga_ref[...]).astype(BF16)
    npool = _rms(p_ref[...], gp_ref[...]).astype(BF16)
    nl = _rms(l_ref[...], gl_ref[...]).astype(BF16)
    p0, p1 = ATTN_WIDTH, ATTN_WIDTH + POOL_WIDTH
    y = jnp.dot(na, w_ref[0:p0, :], preferred_element_type=F32)
    y = y + jnp.dot(npool, w_ref[p0:p1, :], preferred_element_type=F32)
    y = y + jnp.dot(nl, w_ref[p1:, :], preferred_element_type=F32)
    o_ref[...] = x_ref[...] + gate_ref[...] * _rms(y, gpost_ref[...])


def _mix_out(attn, pool, lru, x, g_attn, g_pool, g_lru, w_out, layer, g_post, gate):
    tm = 256
    vec = lambda n: pl.BlockSpec((1, n), lambda i: (0, 0))
    rows = lambda n: pl.BlockSpec((tm, n), lambda i: (i, 0))
    return pl.pallas_call(
        _mix_out_kernel,
        grid=(SEQ // tm,),
        in_specs=[rows(ATTN_WIDTH), rows(POOL_WIDTH), rows(LRU_WIDTH), rows(D_MODEL),
                  vec(ATTN_WIDTH), vec(POOL_WIDTH), vec(LRU_WIDTH),
                  pl.BlockSpec((None, D_MODEL, D_MODEL), lambda i: (layer, 0, 0)),
                  vec(D_MODEL), vec(D_MODEL)],
        out_specs=rows(D_MODEL),
        out_shape=jax.ShapeDtypeStruct((SEQ, D_MODEL), F32),
        compiler_params=_cparams("parallel"),
        name="mix_out",
    )(attn, pool, lru, x, g_attn, g_pool, g_lru, w_out, g_post, gate)


def _mlp_down_kernel(u_ref, w_ref, x_ref, gpost_ref, gate_ref, o_ref):
    y = jnp.dot(u_ref[...], w_ref[...], preferred_element_type=F32)
    o_ref[...] = x_ref[...] + gate_ref[...] * _rms(y, gpost_ref[...])


def _mlp_down(u2, w_down, layer, x, g_post, gate):
    tm = 256
    vec = pl.BlockSpec((1, D_MODEL), lambda i: (0, 0))
    return pl.pallas_call(
        _mlp_down_kernel,
        grid=(SEQ // tm,),
        in_specs=[pl.BlockSpec((tm, MLP_HIDDEN), lambda i: (i, 0)),
                  pl.BlockSpec((None, MLP_HIDDEN, D_MODEL), lambda i: (layer, 0, 0),
                               pipeline_mode=pl.Buffered(1)),
                  pl.BlockSpec((tm, D_MODEL), lambda i: (i, 0)),
                  vec, vec],
        out_specs=pl.BlockSpec((tm, D_MODEL), lambda i: (i, 0)),
        out_shape=jax.ShapeDtypeStruct((SEQ, D_MODEL), F32),
        compiler_params=_cparams("parallel"),
        name="mlp_down",
    )(u2, w_down, x, g_post, gate)


def kernel(x, c, positions, rel_bias, w_mod, b_mod, g_pre_mix, g_post_mix, g_pre_mlp, g_post_mlp, w_in, pool_w, pool_scale, conv_w, conv_b, gate_r_w, gate_r_b, gate_i_w, gate_i_b, lru_lambda, g_attn_out, g_pool_out, g_lru_out, w_out, w_mlp_up, w_mlp_down):
    assert x.shape == (1, SEQ, D_MODEL) and positions.shape == (1, SEQ)
    del positions
    xs = x[0]
    mod = _modulation(c, w_mod, b_mod)
    band = _bias_band(rel_bias)
    row = lambda v: v.reshape(1, -1)
    n_attn_cols = 4 * ATTN_WIDTH
    aux_lo = n_attn_cols + IDX_DIM + N_IDX_HEADS
    w_attn = w_in[:, :, :n_attn_cols].astype(BF16)
    w_aux = jnp.concatenate(
        [w_in[:, :, aux_lo:], w_in[:, :, n_attn_cols:aux_lo],
         jnp.zeros((DEPTH, D_MODEL, AUX_WIDTH - (AUX_KIDX + IDX_DIM + N_IDX_HEADS)), F32)],
        axis=2).astype(BF16)
    w_out_b = w_out.astype(BF16)
    w_down_b = w_mlp_down.astype(BF16)
    for l in range(DEPTH):
        shift1, scale1, gate1, shift2, scale2, gate2 = [
            mod[l, :, n * D_MODEL:(n + 1) * D_MODEL] for n in range(6)]
        g_pre = row(g_pre_mix[l])
        attn_grp, aux = _in_projection(xs, g_pre, scale1, shift1, w_attn, w_aux, l)
        words = _select(attn_grp, aux)
        attn = _attention(attn_grp, words, band)
        pool, lru = _pool_lru(aux, pool_w[l], pool_scale[l], conv_w[l], conv_b[l], gate_r_w[l],
                              gate_r_b[l], gate_i_w[l], gate_i_b[l], lru_lambda[l])
        xs = _mix_out(attn, pool, lru, xs, row(g_attn_out[l]), row(g_pool_out[l]),
                      row(g_lru_out[l]), w_out_b, l, row(g_post_mix[l]), gate1)

        u2 = _adaln_matmul(xs, row(g_pre_mlp[l]), scale2, shift2, w_mlp_up, l, MLP_HIDDEN, BF16,
                           tm=1024, tn=1024, relu2=True, name="mlp_up")
        xs = _mlp_down(u2, w_down_b, l, xs, row(g_post_mlp[l]), gate2)
    return xs[None]
```

```python
import functools
import math

import jax
import jax.numpy as jnp
from jax import lax
from jax.experimental import pallas as pl
from jax.experimental.pallas import tpu as pltpu

D_MODEL = 2048
SEQ = 8192
DEPTH = 2
ATTN_HEAD_DIM = 128
ATTN_WIDTH = 1024
N_ATTN_HEADS = 8
POOL_WIDTH = 512
POOL_WINDOWS = (2, 4, 8, 16)
POOL_GROUP = 128
LRU_WIDTH = 512
N_LRU_BLOCKS = 4
LRU_BLOCK = 128
LRU_C = 8.0
CONV_WIDTH = 4
N_IDX_HEADS = 16
IDX_DIM = 64
TOP_K = 256
N_BUCKETS = 32
MAX_DISTANCE = 128
MLP_HIDDEN = 4 * D_MODEL
NORM_EPS = 1e-6

AUX_POOL = 0
AUX_LRU_IN = 512
AUX_LRU_GATE = 1024
AUX_KIDX = 1536
AUX_WIDTH = 1664
LANE = 128
SUBLANE = 8

SEL_T = 256
SEL_GROUP = 2
SEL_SCORE_GROUP = 4
ATT_TQ = 512
ATT_TK = SEL_T
ATT_UNROLL = 4
N_BAND = ATT_TQ // ATT_TK + 2
assert MAX_DISTANCE <= ATT_TK // 2 and SEQ // SEL_T <= 32 and SEL_T >= TOP_K
MASK_NEG = -1e30
LOG2E = math.log2(math.e)
VMEM_LIMIT = 56 * 1024 * 1024

F32 = jnp.float32
BF16 = jnp.bfloat16
_NT = (((1,), (1,)), ((), ()))


def _cparams(*sem):
    return pltpu.CompilerParams(dimension_semantics=sem, vmem_limit_bytes=VMEM_LIMIT)


def _rms(x, g):
    ms = jnp.mean(x * x, axis=-1, keepdims=True)
    return (x * lax.rsqrt(ms + NORM_EPS)) * g


def _mod_kernel(c_ref, w_ref, b_ref, o_ref):
    c = c_ref[...]
    c_act = c * jax.nn.sigmoid(c)
    o_ref[0] = jnp.sum(c_act * w_ref[0], axis=0, keepdims=True) + b_ref[0]


def _modulation(c, w_mod, b_mod):
    tn = 1024
    n = w_mod.shape[-1]
    return pl.pallas_call(
        _mod_kernel,
        grid=(DEPTH, n // tn),
        in_specs=[
            pl.BlockSpec((D_MODEL, 1), lambda l, j: (0, 0)),
            pl.BlockSpec((1, D_MODEL, tn), lambda l, j: (l, 0, j)),
            pl.BlockSpec((1, 1, tn), lambda l, j: (l, 0, j)),
        ],
        out_specs=pl.BlockSpec((1, 1, tn), lambda l, j: (l, 0, j)),
        out_shape=jax.ShapeDtypeStruct((DEPTH, 1, n), F32),
        compiler_params=_cparams("parallel", "parallel"),
        name="modulation",
    )(c.reshape(D_MODEL, 1), w_mod, b_mod.reshape(DEPTH, 1, n))


def _mlp_up_kernel(x_ref, g_ref, sc_ref, sh_ref, w_ref, wd_ref, o_ref, wdb_ref, h_ref, *, tm):
    rc = 64

    @pl.when(pl.program_id(1) == 0)
    def _():
        g = g_ref[...]
        sc1 = 1.0 + sc_ref[...]
        sh = sh_ref[...]

        def body(r, carry):
            rows = pl.ds(pl.multiple_of(r * rc, rc), rc)
            h = _rms(x_ref[rows, :], g) * sc1 + sh
            h_ref[rows, :] = h.astype(BF16)
            return carry

        lax.fori_loop(0, tm // rc, body, 0)

    u = jnp.maximum(jnp.dot(h_ref[...], w_ref[...].astype(BF16), preferred_element_type=F32), 0.0)
    o_ref[...] = (u * u).astype(o_ref.dtype)
    wdb_ref[...] = wd_ref[...].astype(BF16)


def _mlp_up(x, g, scale, shift, w_up, w_down, layer):
    tm = tn = 1024
    m, k = x.shape
    n = w_up.shape[2]
    steps_n = n // tn
    steps = (m // tm) * steps_n
    slab = w_down.shape[1] // steps
    vec = pl.BlockSpec((1, k), lambda i, j: (0, 0))
    return pl.pallas_call(
        functools.partial(_mlp_up_kernel, tm=tm),
        grid=(m // tm, steps_n),
        in_specs=[
            pl.BlockSpec((tm, k), lambda i, j: (i, 0)),
            vec, vec, vec,
            pl.BlockSpec((None, k, tn), lambda i, j: (layer, 0, j)),
            pl.BlockSpec((None, slab, w_down.shape[2]), lambda i, j: (layer, i * steps_n + j, 0)),
        ],
        out_specs=[pl.BlockSpec((tm, tn), lambda i, j: (i, j)),
                   pl.BlockSpec((slab, w_down.shape[2]), lambda i, j: (i * steps_n + j, 0))],
        out_shape=[jax.ShapeDtypeStruct((m, n), BF16), jax.ShapeDtypeStruct(w_down.shape[1:], BF16)],
        scratch_shapes=[pltpu.VMEM((tm, k), BF16)],
        compiler_params=_cparams("parallel", "arbitrary"),
        name="mlp_up",
    )(x, g, scale, shift, w_up, w_down)


def _proj_kernel(x_ref, g_ref, sc_ref, sh_ref, wa_ref, wx_ref, oa_ref, ox_ref):
    h = (_rms(x_ref[...], g_ref[...]) * (1.0 + sc_ref[...]) + sh_ref[...]).astype(BF16)
    oa_ref[...] = jnp.dot(h, wa_ref[...], preferred_element_type=F32).astype(oa_ref.dtype)
    ox_ref[...] = jnp.dot(h, wx_ref[...], preferred_element_type=F32)


def _in_projection(x, g, scale, shift, w_attn, w_aux, layer):
    tm = 256
    m, k = x.shape
    na, nx = w_attn.shape[2], w_aux.shape[2]
    vec = pl.BlockSpec((1, k), lambda i: (0, 0))
    resident = lambda n: pl.BlockSpec((None, k, n), lambda i: (layer, 0, 0), pipeline_mode=pl.Buffered(1))
    return pl.pallas_call(
        _proj_kernel,
        grid=(m // tm,),
        in_specs=[pl.BlockSpec((tm, k), lambda i: (i, 0)), vec, vec, vec, resident(na), resident(nx)],
        out_specs=[pl.BlockSpec((tm, na), lambda i: (i, 0)), pl.BlockSpec((tm, nx), lambda i: (i, 0))],
        out_shape=[jax.ShapeDtypeStruct((m, na), BF16), jax.ShapeDtypeStruct((m, nx), F32)],
        compiler_params=_cparams("parallel"),
        name="in_proj",
    )(x, g, scale, shift, w_attn, w_aux)


def _band_kernel(rb_ref, o_ref):
    d = pl.program_id(0)
    t = lax.broadcasted_iota(jnp.int32, (ATT_TQ, ATT_TK), 0)
    s = lax.broadcasted_iota(jnp.int32, (ATT_TQ, ATT_TK), 1)
    rel = jnp.maximum((d - (ATT_TQ // ATT_TK - 1)) * ATT_TK + t - s, 0)
    max_exact = N_BUCKETS // 2
    relf = jnp.maximum(rel.astype(F32), 1.0)
    large = max_exact + (jnp.log(relf / max_exact) / math.log(MAX_DISTANCE / max_exact)
                         * (N_BUCKETS - max_exact)).astype(jnp.int32)
    large = jnp.minimum(large, N_BUCKETS - 1)
    bucket = jnp.where(rel < max_exact, rel, large)
    for h in range(N_ATTN_HEADS):
        val = jnp.zeros((ATT_TQ, ATT_TK), F32)
        for b in range(N_BUCKETS):
            val = jnp.where(bucket == b, rb_ref[b, h], val)
        o_ref[0, h] = val * LOG2E


def _bias_band(rel_bias):
    return pl.pallas_call(
        _band_kernel,
        grid=(N_BAND,),
        in_specs=[pl.BlockSpec(memory_space=pltpu.SMEM)],
        out_specs=pl.BlockSpec((1, N_ATTN_HEADS, ATT_TQ, ATT_TK), lambda d: (d, 0, 0, 0)),
        out_shape=jax.ShapeDtypeStruct((N_BAND, N_ATTN_HEADS, ATT_TQ, ATT_TK), F32),
        compiler_params=_cparams("parallel"),
        name="bias_band",
    )(rel_bias)


def _float_to_key(x):
    bits = lax.bitcast_convert_type(x, jnp.int32)
    return jnp.where(bits >= 0, bits, jnp.bitwise_xor(bits, jnp.int32(0x7FFFFFFF)))


_KEY_POS_INF = 0x7F800000
_KEY_NEG_INF = 0x807FFFFF - (1 << 32)


def _key_to_float(key):
    bits = jnp.where(key >= 0, key, jnp.bitwise_xor(key, jnp.int32(0x7FFFFFFF)))
    return lax.bitcast_convert_type(bits, F32)


def _select_kernel(qi_ref, kw_ref, o_ref, kb_ref, sc_ref, hi_ref, lo_ref):
    i = pl.program_id(0)
    tq = tk = SEL_T
    idx_scale = (N_IDX_HEADS * IDX_DIM) ** -0.5

    @pl.when(i == 0)
    def _():
        kb_ref[...] = kw_ref[:, 0:IDX_DIM].astype(BF16)

    w_t = jnp.transpose(kw_ref[pl.ds(pl.multiple_of(i * tq, tq), tq), :])

    nchunks = i + 1
    key_pos = lax.broadcasted_iota(jnp.int32, (tk, tq), 0)
    qry_pos = i * tq + lax.broadcasted_iota(jnp.int32, (tk, tq), 1)
    grp = SEL_GROUP

    def high_half(key):
        return lax.shift_right_arithmetic(key, jnp.int32(16))

    def score_tiles(j0, ntiles, gmax):
        kc = kb_ref[pl.ds(pl.multiple_of(j0 * tk, tk), ntiles * tk), :]
        acc = jnp.zeros((ntiles * tk, tq), F32)
        for h in range(N_IDX_HEADS):
            d = lax.dot_general(kc, qi_ref[:, h * IDX_DIM:(h + 1) * IDX_DIM], _NT,
                                preferred_element_type=F32)
            acc = acc + w_t[IDX_DIM + h:IDX_DIM + h + 1, :] * jnp.maximum(d, 0.0)
        for u in range(ntiles):
            j = j0 + u
            score = jnp.where(key_pos + j * tk <= qry_pos, acc[u * tk:(u + 1) * tk] * idx_scale, -jnp.inf)
            sc_ref[j] = score
            gmax = jnp.maximum(gmax, score)
            key = _float_to_key(score)
            hi_ref[j] = high_half(key).astype(jnp.int16)
            lo_ref[j] = (jnp.bitwise_and(key, jnp.int32(0xFFFF)) - 32768).astype(jnp.int16)
        return gmax

    ngroups = (nchunks + grp - 1) // grp
    big = SEL_SCORE_GROUP
    nbig = nchunks // big
    gmax = lax.fori_loop(0, nbig, lambda jg, gm: score_tiles(big * jg, big, gm),
                         jnp.full((tk, tq), -jnp.inf, F32))
    gmax = lax.fori_loop(nbig * (big // grp), ngroups, lambda jg, gm: score_tiles(grp * jg, grp, gm), gmax)
    t_min = high_half(_float_to_key(jnp.min(gmax, axis=0, keepdims=True)))
    t_max = high_half(_float_to_key(jnp.max(gmax, axis=0, keepdims=True)))

    i16 = jnp.int16
    pack = 16
    one16 = jnp.ones((pack, tq), i16)
    zero16 = jnp.zeros((pack, tq), i16)

    def count_ge(ref, cand):
        cand_b = jnp.broadcast_to(cand, (pack, tq)).astype(i16)

        def body(jg, acc):
            parts = [acc, zero16, zero16, zero16]
            n = 0
            for u in range(grp):
                x = ref[grp * jg + u]
                for g in range(tk // pack):
                    hit = jnp.where(x[g * pack:(g + 1) * pack, :] >= cand_b, one16, zero16)
                    parts[n % 4] = parts[n % 4] + hit
                    n += 1
            return (parts[0] + parts[1]) + (parts[2] + parts[3])

        acc = lax.fori_loop(0, ngroups, body, zero16)
        return jnp.sum(acc.astype(jnp.int32).astype(F32), axis=0, keepdims=True)

    def kth_largest(ref, rank, lo0, hi0, stop_on_hit):
        def unsettled(carry):
            lo, hi, _ = carry
            return jnp.max((hi - lo).astype(F32)) > 0.0

        def halve(carry):
            lo, hi, above = carry
            mid = lo + lax.shift_right_logical(hi - lo + 1, jnp.int32(1))
            cnt = count_ge(ref, mid)
            ok = cnt >= rank
            lo, hi, above = jnp.where(ok, mid, lo), jnp.where(ok, hi, mid - 1), jnp.where(ok, above, cnt)
            if stop_on_hit:
                hit = cnt == rank
                lo, hi = jnp.where(hit, mid, lo), jnp.where(hit, mid, hi)
            return lo, hi, above

        def body(carry):
            return halve(halve(carry))

        lo, _, above = lax.while_loop(unsettled, body, (lo0, hi0, jnp.zeros((1, tq), F32)))
        return lo, above

    t_hi, above = kth_largest(hi_ref, jnp.full((1, tq), float(TOP_K), F32), t_min, t_max, False)
    t_hi_b = jnp.broadcast_to(t_hi, (tk, tq)).astype(i16)

    def narrow_body(j, carry):
        lo_ref[j] = jnp.where(hi_ref[j] == t_hi_b, lo_ref[j], jnp.full((tk, tq), -32768, i16))
        return carry

    lax.fori_loop(0, grp * ngroups, narrow_body, 0)
    t_lo, _ = kth_largest(lo_ref, float(TOP_K) - above, jnp.full((1, tq), -32768, jnp.int32),
                          jnp.full((1, tq), 32767, jnp.int32), True)
    key = jnp.bitwise_or(jnp.left_shift(t_hi, 16), t_lo + 32768)
    key = jnp.clip(key, _KEY_NEG_INF, _KEY_POS_INF)
    thr = jnp.broadcast_to(_key_to_float(key), (tk, tq))

    def word_body(j, word):
        keep = jnp.logical_and(sc_ref[j] >= thr, key_pos + j * tk <= qry_pos)
        return jnp.bitwise_or(word, jnp.where(keep, jnp.left_shift(jnp.int32(1), j), 0))

    word_t = lax.fori_loop(0, nchunks, word_body, jnp.zeros((tk, tq), jnp.int32))
    o_ref[...] = jnp.transpose(word_t)


def _select(attn_grp, aux):
    tq = tk = SEL_T
    nq, nk = SEQ // tq, SEQ // tk
    return pl.pallas_call(
        _select_kernel,
        grid=(nq,),
        in_specs=[
            pl.BlockSpec((tq, N_IDX_HEADS * IDX_DIM), lambda i: (i, 3)),
            pl.BlockSpec((SEQ, LANE), lambda i: (0, AUX_KIDX // LANE)),
        ],
        out_specs=pl.BlockSpec((tq, tk), lambda i: (i, 0)),
        out_shape=jax.ShapeDtypeStruct((SEQ, tk), jnp.int32),
        scratch_shapes=[
            pltpu.VMEM((SEQ, IDX_DIM), BF16),
            pltpu.VMEM((nk, tk, tq), F32),
            pltpu.VMEM((nk, tk, tq), jnp.int16),
            pltpu.VMEM((nk, tk, tq), jnp.int16),
        ],
        compiler_params=_cparams("arbitrary"),
        name="select",
    )(attn_grp, aux)


def _attn_kernel(q_ref, k_ref, v_ref, w_ref, band_ref, o_ref, mask_ref, sa_ref, sb_ref, acc_ref):
    i = pl.program_id(0)
    tq, tk, unroll = ATT_TQ, ATT_TK, ATT_UNROLL
    dh = ATTN_HEAD_DIM
    c = (ATTN_HEAD_DIM ** -0.5) * LOG2E
    r = tq // tk
    ngrp = (r * (i + 1) + unroll - 1) // unroll
    ones = jnp.ones((unroll * tk, LANE), BF16)

    @pl.when(pl.program_id(1) == 0)
    def _():
        def expand(j, carry):
            keep = jnp.bitwise_and(w_ref[...], jnp.left_shift(jnp.int32(1), j)) != 0
            mask_ref[j] = jnp.where(keep, 0.0, MASK_NEG)
            return carry

        lax.fori_loop(0, unroll * ngrp, expand, 0)

    def logits(g, s_ref, m_prev):
        mx = jnp.full((tq, LANE), MASK_NEG, F32)
        for u in range(unroll):
            j = unroll * g + u
            kt = k_ref[pl.ds(pl.multiple_of(j * tk, tk), tk), :]
            s2 = lax.dot_general(q_ref[...], kt, _NT, preferred_element_type=F32) * c
            s2 = s2 + band_ref[jnp.clip(r * i + (r - 1) - j, 0, N_BAND - 1)] + mask_ref[j]
            s_ref[u] = s2
            for cc in range(tk // LANE):
                mx = jnp.maximum(mx, s2[:, cc * LANE:(cc + 1) * LANE])
        return jnp.maximum(m_prev, jnp.max(mx, axis=1, keepdims=True))

    def values(g, s_ref, m_prev, m_cur):
        m_b = jnp.broadcast_to(m_cur, (tq, tk))
        p = jnp.concatenate([jnp.exp2(s_ref[u] - m_b).astype(BF16) for u in range(unroll)], axis=1)
        vt = v_ref[pl.ds(pl.multiple_of(g * (unroll * tk), unroll * tk), unroll * tk), :]
        pv = jnp.dot(p, jnp.concatenate([vt, ones], axis=1), preferred_element_type=F32)
        acc_ref[...] = acc_ref[...] * jnp.exp2(m_prev - m_cur) + pv

    acc_ref[...] = jnp.zeros(acc_ref.shape, F32)
    m_none = jnp.full((tq, 1), MASK_NEG, F32)
    m_0 = logits(0, sa_ref, m_none)
    npair = (ngrp - 1) // 2

    def pair(t, carry):
        m_prev, m_cur = carry
        m_1 = logits(2 * t + 1, sb_ref, m_cur)
        values(2 * t, sa_ref, m_prev, m_cur)
        m_2 = logits(2 * t + 2, sa_ref, m_1)
        values(2 * t + 1, sb_ref, m_cur, m_1)
        return m_1, m_2

    m_prev, m_cur = lax.fori_loop(0, npair, pair, (m_none, m_0))
    last = 2 * npair

    @pl.when(ngrp - 1 == last)
    def _():
        values(last, sa_ref, m_prev, m_cur)

    @pl.when(ngrp - 1 > last)
    def _():
        m_1 = logits(last + 1, sb_ref, m_cur)
        values(last, sa_ref, m_prev, m_cur)
        values(last + 1, sb_ref, m_cur, m_1)

    o_ref[...] = acc_ref[:, :dh] / acc_ref[:, dh:dh + 1]


def _attention(attn_grp, words, band):
    tq, tk = ATT_TQ, ATT_TK
    nq, nk = SEQ // tq, SEQ // tk
    dh = ATTN_HEAD_DIM
    nh = N_ATTN_HEADS
    return pl.pallas_call(
        _attn_kernel,
        grid=(nq, nh),
        in_specs=[
            pl.BlockSpec((tq, dh), lambda i, h: (i, h)),
            pl.BlockSpec((SEQ, dh), lambda i, h: (0, nh + h)),
            pl.BlockSpec((SEQ, dh), lambda i, h: (0, 2 * nh + h)),
            pl.BlockSpec((tq, tk), lambda i, h: (i, 0)),
            pl.BlockSpec((N_BAND, None, tq, tk), lambda i, h: (0, h, 0, 0)),
        ],
        out_specs=pl.BlockSpec((tq, dh), lambda i, h: (i, h)),
        out_shape=jax.ShapeDtypeStruct((SEQ, ATTN_WIDTH), F32),
        scratch_shapes=[pltpu.VMEM((nk, tq, tk), F32),
                        pltpu.VMEM((ATT_UNROLL, tq, tk), F32), pltpu.VMEM((ATT_UNROLL, tq, tk), F32),
                        pltpu.VMEM((tq, dh + LANE), F32)],
        compiler_params=_cparams("parallel", "arbitrary"),
        name="attention",
    )(attn_grp, attn_grp, attn_grp, words, band)


POOL_HALO = 16
CONV_HALO = 8


def _block_diag_dot(x, w_ref):
    outs = []
    for g in range(w_ref.shape[0]):
        xs = x[:, g * LRU_BLOCK:(g + 1) * LRU_BLOCK].astype(BF16)
        outs.append(jnp.dot(xs, w_ref[g], preferred_element_type=F32))
    return jnp.concatenate(outs, axis=1)


def _pool_lru_kernel(pin_ref, lin_ref, gate_ref, pw_ref, ps_ref, cw_ref, cb_ref, rw_ref, rb_ref,
                     iw_ref, ib_ref, lam_ref, pool_o, lru_o, pprev_ref, lprev_ref, h_ref, *, T):
    i = pl.program_id(0)

    @pl.when(i == 0)
    def _():
        pprev_ref[...] = jnp.zeros(pprev_ref.shape, F32)
        lprev_ref[...] = jnp.zeros(lprev_ref.shape, F32)
        h_ref[...] = jnp.zeros(h_ref.shape, F32)

    t_glob = i * T + lax.broadcasted_iota(jnp.int32, (T, 1), 0)

    u = pin_ref[...]
    ext = jnp.concatenate([pprev_ref[...], u], axis=0)
    pprev_ref[...] = u[T - POOL_HALO:, :]
    p2 = ext[1:] + ext[:-1]
    p4 = p2[2:, POOL_GROUP:] + p2[:-2, POOL_GROUP:]
    p8 = p4[4:, POOL_GROUP:] + p4[:-4, POOL_GROUP:]
    p16 = p8[8:, POOL_GROUP:] + p8[:-8, POOL_GROUP:]
    wsums = (p2[15:15 + T, :POOL_GROUP], p4[13:13 + T, :POOL_GROUP],
             p8[9:9 + T, :POOL_GROUP], p16[1:1 + T, :])
    for g, win in enumerate(POOL_WINDOWS):
        gs = slice(g * POOL_GROUP, (g + 1) * POOL_GROUP)
        count = jnp.minimum(t_glob + 1, win).astype(F32)
        dlt = wsums[g] / count - u[:, gs]
        y = jnp.dot(dlt.astype(BF16), pw_ref[g], preferred_element_type=F32)
        pool_o[:, gs] = y * ps_ref[:, gs]

    x = lin_ref[...]
    lext = jnp.concatenate([lprev_ref[...], x], axis=0)
    lprev_ref[...] = x[T - CONV_HALO:, :]
    xc = cb_ref[...]
    for jj in range(CONV_WIDTH):
        off = CONV_HALO - (CONV_WIDTH - 1) + jj
        xc = xc + cw_ref[jj:jj + 1, :] * lext[off:off + T, :]
    r = jax.nn.sigmoid(_block_diag_dot(xc, rw_ref) + rb_ref[...])
    ig = jax.nn.sigmoid(_block_diag_dot(xc, iw_ref) + ib_ref[...])
    z = -lam_ref[...]
    softplus = jnp.maximum(z, 0.0) + jnp.log1p(jnp.exp(-jnp.abs(z)))
    log_a = (-LRU_C * r) * softplus
    a = jnp.exp(log_a)
    b = jnp.sqrt(-jnp.tanh(log_a) * (a * a + 1.0)) * (ig * xc)

    ngrp = T // SUBLANE
    a3 = a.reshape(ngrp, SUBLANE, LRU_WIDTH)
    b3 = b.reshape(ngrp, SUBLANE, LRU_WIDTH)
    sub = lax.broadcasted_iota(jnp.int32, a3.shape, 1)
    sh = 1
    while sh < SUBLANE:
        a_sh = jnp.where(sub >= sh, pltpu.roll(a3, sh, axis=1), 1.0)
        b_sh = jnp.where(sub >= sh, pltpu.roll(b3, sh, axis=1), 0.0)
        b3 = a3 * b_sh + b3
        a3 = a3 * a_sh
        sh *= 2
    carry = h_ref[...]
    rows = []
    for grp in range(ngrp):
        hg = a3[grp] * carry + b3[grp]
        rows.append(hg)
        carry = hg[SUBLANE - 1:, :]
    h = jnp.concatenate(rows, axis=0)
    h_ref[...] = carry

    gt = gate_ref[...]
    cdf = 0.5 * (1.0 + jnp.tanh(math.sqrt(2.0 / math.pi) * (gt + 0.044715 * (gt * gt * gt))))
    lru_o[...] = h * (gt * cdf)


def _pool_lru(aux, pool_w, pool_scale, conv_w, conv_b, r_w, r_b, i_w, i_b, lam):
    T = 256
    full = lambda a: pl.BlockSpec(a.shape, lambda i: (0,) * a.ndim)
    row = lambda v: v.reshape(1, -1)
    args = (pool_w.astype(BF16), row(pool_scale), conv_w, row(conv_b), r_w.astype(BF16), row(r_b),
            i_w.astype(BF16), row(i_b), row(lam))
    return pl.pallas_call(
        functools.partial(_pool_lru_kernel, T=T),
        grid=(SEQ // T,),
        in_specs=[
            pl.BlockSpec((T, POOL_WIDTH), lambda i: (i, AUX_POOL // POOL_WIDTH)),
            pl.BlockSpec((T, LRU_WIDTH), lambda i: (i, AUX_LRU_IN // LRU_WIDTH)),
            pl.BlockSpec((T, LRU_WIDTH), lambda i: (i, AUX_LRU_GATE // LRU_WIDTH)),
        ] + [full(a) for a in args],
        out_specs=[pl.BlockSpec((T, POOL_WIDTH), lambda i: (i, 0)),
                   pl.BlockSpec((T, LRU_WIDTH), lambda i: (i, 0))],
        out_shape=[jax.ShapeDtypeStruct((SEQ, POOL_WIDTH), F32),
                   jax.ShapeDtypeStruct((SEQ, LRU_WIDTH), F32)],
        scratch_shapes=[pltpu.VMEM((POOL_HALO, POOL_WIDTH), F32),
                        pltpu.VMEM((CONV_HALO, LRU_WIDTH), F32),
                        pltpu.VMEM((1, LRU_WIDTH), F32)],
        compiler_params=_cparams("arbitrary"),
        name="pool_lru",
    )(aux, aux, aux, *args)


def _mix_out_kernel(a_ref, p_ref, l_ref, x_ref, ga_ref, gp_ref, gl_ref, w_ref, gpost_ref, gate_ref,
                    o_ref):
    na = _rms(a_ref[...], ga_ref[...]).astype(BF16)
    npool = _rms(p_ref[...], gp_ref[...]).astype(BF16)
    nl = _rms(l_ref[...], gl_ref[...]).astype(BF16)
    p0, p1 = ATTN_WIDTH, ATTN_WIDTH + POOL_WIDTH
    y = jnp.dot(na, w_ref[0:p0, :], preferred_element_type=F32)
    y = y + jnp.dot(npool, w_ref[p0:p1, :], preferred_element_type=F32)
    y = y + jnp.dot(nl, w_ref[p1:, :], preferred_element_type=F32)
    o_ref[...] = x_ref[...] + gate_ref[...] * _rms(y, gpost_ref[...])


def _mix_out(attn, pool, lru, x, g_attn, g_pool, g_lru, w_out, layer, g_post, gate):
    tm = 256
    vec = lambda n: pl.BlockSpec((1, n), lambda i: (0, 0))
    rows = lambda n: pl.BlockSpec((tm, n), lambda i: (i, 0))
    return pl.pallas_call(
        _mix_out_kernel,
        grid=(SEQ // tm,),
        in_specs=[rows(ATTN_WIDTH), rows(POOL_WIDTH), rows(LRU_WIDTH), rows(D_MODEL),
                  vec(ATTN_WIDTH), vec(POOL_WIDTH), vec(LRU_WIDTH),
                  pl.BlockSpec((None, D_MODEL, D_MODEL), lambda i: (layer, 0, 0)),
                  vec(D_MODEL), vec(D_MODEL)],
        out_specs=rows(D_MODEL),
        out_shape=jax.ShapeDtypeStruct((SEQ, D_MODEL), F32),
        compiler_params=_cparams("parallel"),
        name="mix_out",
    )(attn, pool, lru, x, g_attn, g_pool, g_lru, w_out, g_post, gate)


def _mlp_down_kernel(u_ref, w_ref, x_ref, gpost_ref, gate_ref, o_ref):
    y = jnp.dot(u_ref[...], w_ref[...], preferred_element_type=F32)
    o_ref[...] = x_ref[...] + gate_ref[...] * _rms(y, gpost_ref[...])


def _mlp_down(u2, w_down, x, g_post, gate):
    tm = 256
    vec = pl.BlockSpec((1, D_MODEL), lambda i: (0, 0))
    return pl.pallas_call(
        _mlp_down_kernel,
        grid=(SEQ // tm,),
        in_specs=[pl.BlockSpec((tm, MLP_HIDDEN), lambda i: (i, 0)),
                  pl.BlockSpec((MLP_HIDDEN, D_MODEL), lambda i: (0, 0), pipeline_mode=pl.Buffered(1)),
                  pl.BlockSpec((tm, D_MODEL), lambda i: (i, 0)),
                  vec, vec],
        out_specs=pl.BlockSpec((tm, D_MODEL), lambda i: (i, 0)),
        out_shape=jax.ShapeDtypeStruct((SEQ, D_MODEL), F32),
        compiler_params=_cparams("parallel"),
        name="mlp_down",
    )(u2, w_down, x, g_post, gate)


def kernel(x, c, positions, rel_bias, w_mod, b_mod, g_pre_mix, g_post_mix, g_pre_mlp, g_post_mlp, w_in, pool_w, pool_scale, conv_w, conv_b, gate_r_w, gate_r_b, gate_i_w, gate_i_b, lru_lambda, g_attn_out, g_pool_out, g_lru_out, w_out, w_mlp_up, w_mlp_down):
    assert x.shape == (1, SEQ, D_MODEL) and positions.shape == (1, SEQ)
    del positions
    xs = x[0]
    mod = _modulation(c, w_mod, b_mod)
    band = _bias_band(rel_bias)
    row = lambda v: v.reshape(1, -1)
    n_attn_cols = 4 * ATTN_WIDTH
    aux_lo = n_attn_cols + IDX_DIM + N_IDX_HEADS
    w_attn = w_in[:, :, :n_attn_cols].astype(BF16)
    w_aux = jnp.concatenate(
        [w_in[:, :, aux_lo:], w_in[:, :, n_attn_cols:aux_lo],
         jnp.zeros((DEPTH, D_MODEL, AUX_WIDTH - (AUX_KIDX + IDX_DIM + N_IDX_HEADS)), F32)],
        axis=2).astype(BF16)
    w_out_b = w_out.astype(BF16)
    for l in range(DEPTH):
        shift1, scale1, gate1, shift2, scale2, gate2 = [
            mod[l, :, n * D_MODEL:(n + 1) * D_MODEL] for n in range(6)]
        g_pre = row(g_pre_mix[l])
        attn_grp, aux = _in_projection(xs, g_pre, scale1, shift1, w_attn, w_aux, l)
        words = _select(attn_grp, aux)
        attn = _attention(attn_grp, words, band)
        pool, lru = _pool_lru(aux, pool_w[l], pool_scale[l], conv_w[l], conv_b[l], gate_r_w[l],
                              gate_r_b[l], gate_i_w[l], gate_i_b[l], lru_lambda[l])
        xs = _mix_out(attn, pool, lru, xs, row(g_attn_out[l]), row(g_pool_out[l]),
                      row(g_lru_out[l]), w_out_b, l, row(g_post_mix[l]), gate1)

        u2, w_down_b = _mlp_up(xs, row(g_pre_mlp[l]), scale2, shift2, w_mlp_up, w_mlp_down, l)
        xs = _mlp_down(u2, w_down_b, xs, row(g_post_mlp[l]), gate2)
    return xs[None]
```

```python
import functools
import math

import jax
import jax.numpy as jnp
from jax import lax
from jax.experimental import pallas as pl
from jax.experimental.pallas import tpu as pltpu

D_MODEL = 2048
SEQ = 8192
DEPTH = 2
ATTN_HEAD_DIM = 128
ATTN_WIDTH = 1024
N_ATTN_HEADS = 8
POOL_WIDTH = 512
POOL_WINDOWS = (2, 4, 8, 16)
POOL_GROUP = 128
LRU_WIDTH = 512
N_LRU_BLOCKS = 4
LRU_BLOCK = 128
LRU_C = 8.0
CONV_WIDTH = 4
N_IDX_HEADS = 16
IDX_DIM = 64
TOP_K = 256
N_BUCKETS = 32
MAX_DISTANCE = 128
MLP_HIDDEN = 4 * D_MODEL
NORM_EPS = 1e-6

AUX_POOL = 0
AUX_LRU_IN = 512
AUX_LRU_GATE = 1024
AUX_KIDX = 1536
AUX_WIDTH = 1664
LANE = 128
SUBLANE = 8

SEL_T = 256
SEL_GROUP = 2
SEL_SCORE_GROUP = 4
ATT_TQ = 512
ATT_TK = SEL_T
ATT_UNROLL = 4
N_BAND = ATT_TQ // ATT_TK + 2
assert MAX_DISTANCE <= ATT_TK // 2 and SEQ // SEL_T <= 32 and SEL_T >= TOP_K
MASK_NEG = -1e30
LOG2E = math.log2(math.e)
VMEM_LIMIT = 56 * 1024 * 1024

F32 = jnp.float32
BF16 = jnp.bfloat16
_NT = (((1,), (1,)), ((), ()))


def _cparams(*sem):
    return pltpu.CompilerParams(dimension_semantics=sem, vmem_limit_bytes=VMEM_LIMIT)


def _rms(x, g):
    ms = jnp.mean(x * x, axis=-1, keepdims=True)
    return (x * lax.rsqrt(ms + NORM_EPS)) * g


def _mod_kernel(c_ref, w_ref, b_ref, o_ref):
    c = c_ref[...]
    c_act = c * jax.nn.sigmoid(c)
    o_ref[0] = jnp.sum(c_act * w_ref[0], axis=0, keepdims=True) + b_ref[0]


def _modulation(c, w_mod, b_mod):
    tn = 1024
    n = w_mod.shape[-1]
    return pl.pallas_call(
        _mod_kernel,
        grid=(DEPTH, n // tn),
        in_specs=[
            pl.BlockSpec((D_MODEL, 1), lambda l, j: (0, 0)),
            pl.BlockSpec((1, D_MODEL, tn), lambda l, j: (l, 0, j)),
            pl.BlockSpec((1, 1, tn), lambda l, j: (l, 0, j)),
        ],
        out_specs=pl.BlockSpec((1, 1, tn), lambda l, j: (l, 0, j)),
        out_shape=jax.ShapeDtypeStruct((DEPTH, 1, n), F32),
        compiler_params=_cparams("parallel", "parallel"),
        name="modulation",
    )(c.reshape(D_MODEL, 1), w_mod, b_mod.reshape(DEPTH, 1, n))


def _mlp_up_kernel(x_ref, g_ref, sc_ref, sh_ref, w_ref, wd_ref, o_ref, wdb_ref, h_ref, *, tm):
    rc = 64

    @pl.when(pl.program_id(1) == 0)
    def _():
        g = g_ref[...]
        sc1 = 1.0 + sc_ref[...]
        sh = sh_ref[...]

        def body(r, carry):
            rows = pl.ds(pl.multiple_of(r * rc, rc), rc)
            h = _rms(x_ref[rows, :], g) * sc1 + sh
            h_ref[rows, :] = h.astype(BF16)
            return carry

        lax.fori_loop(0, tm // rc, body, 0)

    u = jnp.maximum(jnp.dot(h_ref[...], w_ref[...].astype(BF16), preferred_element_type=F32), 0.0)
    o_ref[...] = (u * u).astype(o_ref.dtype)
    wdb_ref[...] = wd_ref[...].astype(BF16)


def _mlp_up(x, g, scale, shift, w_up, w_down, layer):
    tm = tn = 1024
    m, k = x.shape
    n = w_up.shape[2]
    steps_n = n // tn
    steps = (m // tm) * steps_n
    slab = w_down.shape[1] // steps
    vec = pl.BlockSpec((1, k), lambda i, j: (0, 0))
    return pl.pallas_call(
        functools.partial(_mlp_up_kernel, tm=tm),
        grid=(m // tm, steps_n),
        in_specs=[
            pl.BlockSpec((tm, k), lambda i, j: (i, 0)),
            vec, vec, vec,
            pl.BlockSpec((None, k, tn), lambda i, j: (layer, 0, j)),
            pl.BlockSpec((None, slab, w_down.shape[2]), lambda i, j: (layer, i * steps_n + j, 0)),
        ],
        out_specs=[pl.BlockSpec((tm, tn), lambda i, j: (i, j)),
                   pl.BlockSpec((slab, w_down.shape[2]), lambda i, j: (i * steps_n + j, 0))],
        out_shape=[jax.ShapeDtypeStruct((m, n), BF16), jax.ShapeDtypeStruct(w_down.shape[1:], BF16)],
        scratch_shapes=[pltpu.VMEM((tm, k), BF16)],
        compiler_params=_cparams("parallel", "arbitrary"),
        name="mlp_up",
    )(x, g, scale, shift, w_up, w_down)


def _proj_kernel(x_ref, g_ref, sc_ref, sh_ref, wa_ref, wx_ref, oa_ref, ox_ref):
    h = (_rms(x_ref[...], g_ref[...]) * (1.0 + sc_ref[...]) + sh_ref[...]).astype(BF16)
    oa_ref[...] = lax.dot_general(h, wa_ref[...], _NT, preferred_element_type=F32).astype(oa_ref.dtype)
    ox_ref[...] = lax.dot_general(h, wx_ref[...], _NT, preferred_element_type=F32)


def _in_projection(x, g, scale, shift, w_attn_t, na, w_aux_t, layer):
    tm = 512
    m, k = x.shape
    nx = w_aux_t.shape[1]
    vec = pl.BlockSpec((1, k), lambda i: (0, 0))
    resident = lambda n: pl.BlockSpec((None, n, k), lambda i: (layer, 0, 0), pipeline_mode=pl.Buffered(1))
    return pl.pallas_call(
        _proj_kernel,
        grid=(m // tm,),
        in_specs=[pl.BlockSpec((tm, k), lambda i: (i, 0)), vec, vec, vec, resident(na), resident(nx)],
        out_specs=[pl.BlockSpec((tm, na), lambda i: (i, 0)), pl.BlockSpec((tm, nx), lambda i: (i, 0))],
        out_shape=[jax.ShapeDtypeStruct((m, na), BF16), jax.ShapeDtypeStruct((m, nx), F32)],
        compiler_params=_cparams("parallel"),
        name="in_proj",
    )(x, g, scale, shift, w_attn_t, w_aux_t)


def _band_kernel(rb_ref, o_ref):
    d = pl.program_id(0)
    t = lax.broadcasted_iota(jnp.int32, (ATT_TQ, ATT_TK), 0)
    s = lax.broadcasted_iota(jnp.int32, (ATT_TQ, ATT_TK), 1)
    rel = jnp.maximum((d - (ATT_TQ // ATT_TK - 1)) * ATT_TK + t - s, 0)
    max_exact = N_BUCKETS // 2
    relf = jnp.maximum(rel.astype(F32), 1.0)
    large = max_exact + (jnp.log(relf / max_exact) / math.log(MAX_DISTANCE / max_exact)
                         * (N_BUCKETS - max_exact)).astype(jnp.int32)
    large = jnp.minimum(large, N_BUCKETS - 1)
    bucket = jnp.where(rel < max_exact, rel, large)
    for h in range(N_ATTN_HEADS):
        val = jnp.zeros((ATT_TQ, ATT_TK), F32)
        for b in range(N_BUCKETS):
            val = jnp.where(bucket == b, rb_ref[b, h], val)
        o_ref[0, h] = val * LOG2E


def _bias_band(rel_bias):
    return pl.pallas_call(
        _band_kernel,
        grid=(N_BAND,),
        in_specs=[pl.BlockSpec(memory_space=pltpu.SMEM)],
        out_specs=pl.BlockSpec((1, N_ATTN_HEADS, ATT_TQ, ATT_TK), lambda d: (d, 0, 0, 0)),
        out_shape=jax.ShapeDtypeStruct((N_BAND, N_ATTN_HEADS, ATT_TQ, ATT_TK), F32),
        compiler_params=_cparams("parallel"),
        name="bias_band",
    )(rel_bias)


def _float_to_key(x):
    bits = lax.bitcast_convert_type(x, jnp.int32)
    return jnp.where(bits >= 0, bits, jnp.bitwise_xor(bits, jnp.int32(0x7FFFFFFF)))


_KEY_POS_INF = 0x7F800000
_KEY_NEG_INF = 0x807FFFFF - (1 << 32)


def _key_to_float(key):
    bits = jnp.where(key >= 0, key, jnp.bitwise_xor(key, jnp.int32(0x7FFFFFFF)))
    return lax.bitcast_convert_type(bits, F32)


def _select_kernel(qi_ref, kw_ref, o_ref, kb_ref, sc_ref, hi_ref, lo_ref):
    i = pl.program_id(0)
    tq = tk = SEL_T
    idx_scale = (N_IDX_HEADS * IDX_DIM) ** -0.5

    @pl.when(i == 0)
    def _():
        kb_ref[...] = kw_ref[:, 0:IDX_DIM].astype(BF16)

    w_t = jnp.transpose(kw_ref[pl.ds(pl.multiple_of(i * tq, tq), tq), :])

    nchunks = i + 1
    key_pos = lax.broadcasted_iota(jnp.int32, (tk, tq), 0)
    qry_pos = i * tq + lax.broadcasted_iota(jnp.int32, (tk, tq), 1)
    grp = SEL_GROUP

    def high_half(key):
        return lax.shift_right_arithmetic(key, jnp.int32(16))

    def score_tiles(j0, ntiles, gmax):
        kc = kb_ref[pl.ds(pl.multiple_of(j0 * tk, tk), ntiles * tk), :]
        acc = jnp.zeros((ntiles * tk, tq), F32)
        for h in range(N_IDX_HEADS):
            d = lax.dot_general(kc, qi_ref[:, h * IDX_DIM:(h + 1) * IDX_DIM], _NT,
                                preferred_element_type=F32)
            acc = acc + w_t[IDX_DIM + h:IDX_DIM + h + 1, :] * jnp.maximum(d, 0.0)
        for u in range(ntiles):
            j = j0 + u
            score = jnp.where(key_pos + j * tk <= qry_pos, acc[u * tk:(u + 1) * tk] * idx_scale, -jnp.inf)
            sc_ref[j] = score
            gmax = jnp.maximum(gmax, score)
            key = _float_to_key(score)
            hi_ref[j] = high_half(key).astype(jnp.int16)
            lo_ref[j] = (jnp.bitwise_and(key, jnp.int32(0xFFFF)) - 32768).astype(jnp.int16)
        return gmax

    ngroups = (nchunks + grp - 1) // grp
    big = SEL_SCORE_GROUP
    nbig = nchunks // big
    gmax = lax.fori_loop(0, nbig, lambda jg, gm: score_tiles(big * jg, big, gm),
                         jnp.full((tk, tq), -jnp.inf, F32))
    gmax = lax.fori_loop(nbig * (big // grp), ngroups, lambda jg, gm: score_tiles(grp * jg, grp, gm), gmax)
    t_min = high_half(_float_to_key(jnp.min(gmax, axis=0, keepdims=True)))
    t_max = high_half(_float_to_key(jnp.max(gmax, axis=0, keepdims=True)))

    i16 = jnp.int16
    pack = 16
    one16 = jnp.ones((pack, tq), i16)
    zero16 = jnp.zeros((pack, tq), i16)

    def count_ge(ref, cand):
        cand_b = jnp.broadcast_to(cand, (pack, tq)).astype(i16)

        def body(jg, acc):
            parts = [acc, zero16, zero16, zero16]
            n = 0
            for u in range(grp):
                x = ref[grp * jg + u]
                for g in range(tk // pack):
                    hit = jnp.where(x[g * pack:(g + 1) * pack, :] >= cand_b, one16, zero16)
                    parts[n % 4] = parts[n % 4] + hit
                    n += 1
            return (parts[0] + parts[1]) + (parts[2] + parts[3])

        acc = lax.fori_loop(0, ngroups, body, zero16)
        return jnp.sum(acc.astype(jnp.int32).astype(F32), axis=0, keepdims=True)

    def kth_largest(ref, rank, lo0, hi0, stop_on_hit):
        def unsettled(carry):
            lo, hi, _ = carry
            return jnp.max((hi - lo).astype(F32)) > 0.0

        def halve(carry):
            lo, hi, above = carry
            mid = lo + lax.shift_right_logical(hi - lo + 1, jnp.int32(1))
            cnt = count_ge(ref, mid)
            ok = cnt >= rank
            lo, hi, above = jnp.where(ok, mid, lo), jnp.where(ok, hi, mid - 1), jnp.where(ok, above, cnt)
            if stop_on_hit:
                hit = cnt == rank
                lo, hi = jnp.where(hit, mid, lo), jnp.where(hit, mid, hi)
            return lo, hi, above

        def body(carry):
            return halve(halve(carry))

        lo, _, above = lax.while_loop(unsettled, body, (lo0, hi0, jnp.zeros((1, tq), F32)))
        return lo, above

    t_hi, above = kth_largest(hi_ref, jnp.full((1, tq), float(TOP_K), F32), t_min, t_max, False)
    t_hi_b = jnp.broadcast_to(t_hi, (tk, tq)).astype(i16)

    def narrow_body(j, carry):
        lo_ref[j] = jnp.where(hi_ref[j] == t_hi_b, lo_ref[j], jnp.full((tk, tq), -32768, i16))
        return carry

    lax.fori_loop(0, grp * ngroups, narrow_body, 0)
    t_lo, _ = kth_largest(lo_ref, float(TOP_K) - above, jnp.full((1, tq), -32768, jnp.int32),
                          jnp.full((1, tq), 32767, jnp.int32), True)
    key = jnp.bitwise_or(jnp.left_shift(t_hi, 16), t_lo + 32768)
    key = jnp.clip(key, _KEY_NEG_INF, _KEY_POS_INF)
    thr = jnp.broadcast_to(_key_to_float(key), (tk, tq))

    def word_body(j, word):
        keep = jnp.logical_and(sc_ref[j] >= thr, key_pos + j * tk <= qry_pos)
        return jnp.bitwise_or(word, jnp.where(keep, jnp.left_shift(jnp.int32(1), j), 0))

    word_t = lax.fori_loop(0, nchunks, word_body, jnp.zeros((tk, tq), jnp.int32))
    o_ref[...] = jnp.transpose(word_t)


def _select(attn_grp, aux):
    tq = tk = SEL_T
    nq, nk = SEQ // tq, SEQ // tk
    return pl.pallas_call(
        _select_kernel,
        grid=(nq,),
        in_specs=[
            pl.BlockSpec((tq, N_IDX_HEADS * IDX_DIM), lambda i: (i, 3)),
            pl.BlockSpec((SEQ, LANE), lambda i: (0, AUX_KIDX // LANE)),
        ],
        out_specs=pl.BlockSpec((tq, tk), lambda i: (i, 0)),
        out_shape=jax.ShapeDtypeStruct((SEQ, tk), jnp.int32),
        scratch_shapes=[
            pltpu.VMEM((SEQ, IDX_DIM), BF16),
            pltpu.VMEM((nk, tk, tq), F32),
            pltpu.VMEM((nk, tk, tq), jnp.int16),
            pltpu.VMEM((nk, tk, tq), jnp.int16),
        ],
        compiler_params=_cparams("arbitrary"),
        name="select",
    )(attn_grp, aux)


def _attn_kernel(q_ref, k_ref, v_ref, w_ref, band_ref, o_ref, mask_ref, sa_ref, sb_ref, acc_ref):
    i = pl.program_id(0)
    tq, tk, unroll = ATT_TQ, ATT_TK, ATT_UNROLL
    dh = ATTN_HEAD_DIM
    c = (ATTN_HEAD_DIM ** -0.5) * LOG2E
    r = tq // tk
    ngrp = (r * (i + 1) + unroll - 1) // unroll
    ones = jnp.ones((unroll * tk, LANE), BF16)

    @pl.when(pl.program_id(1) == 0)
    def _():
        def expand(j, carry):
            keep = jnp.bitwise_and(w_ref[...], jnp.left_shift(jnp.int32(1), j)) != 0
            mask_ref[j] = jnp.where(keep, 0.0, MASK_NEG)
            return carry

        lax.fori_loop(0, unroll * ngrp, expand, 0)

    def logits(g, s_ref, m_prev):
        mx = jnp.full((tq, LANE), MASK_NEG, F32)
        for u in range(unroll):
            j = unroll * g + u
            kt = k_ref[pl.ds(pl.multiple_of(j * tk, tk), tk), :]
            s2 = lax.dot_general(q_ref[...], kt, _NT, preferred_element_type=F32) * c
            s2 = s2 + band_ref[jnp.clip(r * i + (r - 1) - j, 0, N_BAND - 1)] + mask_ref[j]
            s_ref[u] = s2
            for cc in range(tk // LANE):
                mx = jnp.maximum(mx, s2[:, cc * LANE:(cc + 1) * LANE])
        return jnp.maximum(m_prev, jnp.max(mx, axis=1, keepdims=True))

    def values(g, s_ref, m_prev, m_cur):
        m_b = jnp.broadcast_to(m_cur, (tq, tk))
        p = jnp.concatenate([jnp.exp2(s_ref[u] - m_b).astype(BF16) for u in range(unroll)], axis=1)
        vt = v_ref[pl.ds(pl.multiple_of(g * (unroll * tk), unroll * tk), unroll * tk), :]
        pv = jnp.dot(p, jnp.concatenate([vt, ones], axis=1), preferred_element_type=F32)
        acc_ref[...] = acc_ref[...] * jnp.exp2(m_prev - m_cur) + pv

    acc_ref[...] = jnp.zeros(acc_ref.shape, F32)
    m_none = jnp.full((tq, 1), MASK_NEG, F32)
    m_0 = logits(0, sa_ref, m_none)
    npair = (ngrp - 1) // 2

    def pair(t, carry):
        m_prev, m_cur = carry
        m_1 = logits(2 * t + 1, sb_ref, m_cur)
        values(2 * t, sa_ref, m_prev, m_cur)
        m_2 = logits(2 * t + 2, sa_ref, m_1)
        values(2 * t + 1, sb_ref, m_cur, m_1)
        return m_1, m_2

    m_prev, m_cur = lax.fori_loop(0, npair, pair, (m_none, m_0))
    last = 2 * npair

    @pl.when(ngrp - 1 == last)
    def _():
        values(last, sa_ref, m_prev, m_cur)

    @pl.when(ngrp - 1 > last)
    def _():
        m_1 = logits(last + 1, sb_ref, m_cur)
        values(last, sa_ref, m_prev, m_cur)
        values(last + 1, sb_ref, m_cur, m_1)

    o_ref[...] = acc_ref[:, :dh] / acc_ref[:, dh:dh + 1]


def _attention(attn_grp, words, band):
    tq, tk = ATT_TQ, ATT_TK
    nq, nk = SEQ // tq, SEQ // tk
    dh = ATTN_HEAD_DIM
    nh = N_ATTN_HEADS
    return pl.pallas_call(
        _attn_kernel,
        grid=(nq, nh),
        in_specs=[
            pl.BlockSpec((tq, dh), lambda i, h: (i, h)),
            pl.BlockSpec((SEQ, dh), lambda i, h: (0, nh + h)),
            pl.BlockSpec((SEQ, dh), lambda i, h: (0, 2 * nh + h)),
            pl.BlockSpec((tq, tk), lambda i, h: (i, 0)),
            pl.BlockSpec((N_BAND, None, tq, tk), lambda i, h: (0, h, 0, 0)),
        ],
        out_specs=pl.BlockSpec((tq, dh), lambda i, h: (i, h)),
        out_shape=jax.ShapeDtypeStruct((SEQ, ATTN_WIDTH), F32),
        scratch_shapes=[pltpu.VMEM((nk, tq, tk), F32),
                        pltpu.VMEM((ATT_UNROLL, tq, tk), F32), pltpu.VMEM((ATT_UNROLL, tq, tk), F32),
                        pltpu.VMEM((tq, dh + LANE), F32)],
        compiler_params=_cparams("parallel", "arbitrary"),
        name="attention",
    )(attn_grp, attn_grp, attn_grp, words, band)


POOL_HALO = 16
CONV_HALO = 8


def _block_diag_dot(x, w_ref):
    outs = []
    for g in range(w_ref.shape[0]):
        xs = x[:, g * LRU_BLOCK:(g + 1) * LRU_BLOCK].astype(BF16)
        outs.append(jnp.dot(xs, w_ref[g], preferred_element_type=F32))
    return jnp.concatenate(outs, axis=1)


def _pool_lru_kernel(pin_ref, lin_ref, gate_ref, pw_ref, ps_ref, cw_ref, cb_ref, rw_ref, rb_ref,
                     iw_ref, ib_ref, lam_ref, pool_o, lru_o, pprev_ref, lprev_ref, h_ref, *, T):
    i = pl.program_id(0)

    @pl.when(i == 0)
    def _():
        pprev_ref[...] = jnp.zeros(pprev_ref.shape, F32)
        lprev_ref[...] = jnp.zeros(lprev_ref.shape, F32)
        h_ref[...] = jnp.zeros(h_ref.shape, F32)

    t_glob = i * T + lax.broadcasted_iota(jnp.int32, (T, 1), 0)

    u = pin_ref[...]
    ext = jnp.concatenate([pprev_ref[...], u], axis=0)
    pprev_ref[...] = u[T - POOL_HALO:, :]
    p2 = ext[1:] + ext[:-1]
    p4 = p2[2:, POOL_GROUP:] + p2[:-2, POOL_GROUP:]
    p8 = p4[4:, POOL_GROUP:] + p4[:-4, POOL_GROUP:]
    p16 = p8[8:, POOL_GROUP:] + p8[:-8, POOL_GROUP:]
    wsums = (p2[15:15 + T, :POOL_GROUP], p4[13:13 + T, :POOL_GROUP],
             p8[9:9 + T, :POOL_GROUP], p16[1:1 + T, :])
    for g, win in enumerate(POOL_WINDOWS):
        gs = slice(g * POOL_GROUP, (g + 1) * POOL_GROUP)
        count = jnp.minimum(t_glob + 1, win).astype(F32)
        dlt = wsums[g] / count - u[:, gs]
        y = jnp.dot(dlt.astype(BF16), pw_ref[g], preferred_element_type=F32)
        pool_o[:, gs] = y * ps_ref[:, gs]

    x = lin_ref[...]
    lext = jnp.concatenate([lprev_ref[...], x], axis=0)
    lprev_ref[...] = x[T - CONV_HALO:, :]
    xc = cb_ref[...]
    for jj in range(CONV_WIDTH):
        off = CONV_HALO - (CONV_WIDTH - 1) + jj
        xc = xc + cw_ref[jj:jj + 1, :] * lext[off:off + T, :]
    r = jax.nn.sigmoid(_block_diag_dot(xc, rw_ref) + rb_ref[...])
    ig = jax.nn.sigmoid(_block_diag_dot(xc, iw_ref) + ib_ref[...])
    z = -lam_ref[...]
    softplus = jnp.maximum(z, 0.0) + jnp.log1p(jnp.exp(-jnp.abs(z)))
    log_a = (-LRU_C * r) * softplus
    a = jnp.exp(log_a)
    b = jnp.sqrt(-jnp.tanh(log_a) * (a * a + 1.0)) * (ig * xc)

    ngrp = T // SUBLANE
    a3 = a.reshape(ngrp, SUBLANE, LRU_WIDTH)
    b3 = b.reshape(ngrp, SUBLANE, LRU_WIDTH)
    sub = lax.broadcasted_iota(jnp.int32, a3.shape, 1)
    sh = 1
    while sh < SUBLANE:
        a_sh = jnp.where(sub >= sh, pltpu.roll(a3, sh, axis=1), 1.0)
        b_sh = jnp.where(sub >= sh, pltpu.roll(b3, sh, axis=1), 0.0)
        b3 = a3 * b_sh + b3
        a3 = a3 * a_sh
        sh *= 2
    carry = h_ref[...]
    rows = []
    for grp in range(ngrp):
        hg = a3[grp] * carry + b3[grp]
        rows.append(hg)
        carry = hg[SUBLANE - 1:, :]
    h = jnp.concatenate(rows, axis=0)
    h_ref[...] = carry

    gt = gate_ref[...]
    cdf = 0.5 * (1.0 + jnp.tanh(math.sqrt(2.0 / math.pi) * (gt + 0.044715 * (gt * gt * gt))))
    lru_o[...] = h * (gt * cdf)


def _pool_lru(aux, pool_w, pool_scale, conv_w, conv_b, r_w, r_b, i_w, i_b, lam):
    T = 256
    full = lambda a: pl.BlockSpec(a.shape, lambda i: (0,) * a.ndim)
    row = lambda v: v.reshape(1, -1)
    args = (pool_w.astype(BF16), row(pool_scale), conv_w, row(conv_b), r_w.astype(BF16), row(r_b),
            i_w.astype(BF16), row(i_b), row(lam))
    return pl.pallas_call(
        functools.partial(_pool_lru_kernel, T=T),
        grid=(SEQ // T,),
        in_specs=[
            pl.BlockSpec((T, POOL_WIDTH), lambda i: (i, AUX_POOL // POOL_WIDTH)),
            pl.BlockSpec((T, LRU_WIDTH), lambda i: (i, AUX_LRU_IN // LRU_WIDTH)),
            pl.BlockSpec((T, LRU_WIDTH), lambda i: (i, AUX_LRU_GATE // LRU_WIDTH)),
        ] + [full(a) for a in args],
        out_specs=[pl.BlockSpec((T, POOL_WIDTH), lambda i: (i, 0)),
                   pl.BlockSpec((T, LRU_WIDTH), lambda i: (i, 0))],
        out_shape=[jax.ShapeDtypeStruct((SEQ, POOL_WIDTH), F32),
                   jax.ShapeDtypeStruct((SEQ, LRU_WIDTH), F32)],
        scratch_shapes=[pltpu.VMEM((POOL_HALO, POOL_WIDTH), F32),
                        pltpu.VMEM((CONV_HALO, LRU_WIDTH), F32),
                        pltpu.VMEM((1, LRU_WIDTH), F32)],
        compiler_params=_cparams("arbitrary"),
        name="pool_lru",
    )(aux, aux, aux, *args)


def _mix_out_kernel(a_ref, p_ref, l_ref, x_ref, ga_ref, gp_ref, gl_ref, w_ref, gpost_ref, gate_ref,
                    o_ref):
    na = _rms(a_ref[...], ga_ref[...]).astype(BF16)
    npool = _rms(p_ref[...], gp_ref[...]).astype(BF16)
    nl = _rms(l_ref[...], gl_ref[...]).astype(BF16)
    p0, p1 = ATTN_WIDTH, ATTN_WIDTH + POOL_WIDTH
    y = jnp.dot(na, w_ref[0:p0, :], preferred_element_type=F32)
    y = y + jnp.dot(npool, w_ref[p0:p1, :], preferred_element_type=F32)
    y = y + jnp.dot(nl, w_ref[p1:, :], preferred_element_type=F32)
    o_ref[...] = x_ref[...] + gate_ref[...] * _rms(y, gpost_ref[...])


def _mix_out(attn, pool, lru, x, g_attn, g_pool, g_lru, w_out, layer, g_post, gate):
    tm = 256
    vec = lambda n: pl.BlockSpec((1, n), lambda i: (0, 0))
    rows = lambda n: pl.BlockSpec((tm, n), lambda i: (i, 0))
    return pl.pallas_call(
        _mix_out_kernel,
        grid=(SEQ // tm,),
        in_specs=[rows(ATTN_WIDTH), rows(POOL_WIDTH), rows(LRU_WIDTH), rows(D_MODEL),
                  vec(ATTN_WIDTH), vec(POOL_WIDTH), vec(LRU_WIDTH),
                  pl.BlockSpec((None, D_MODEL, D_MODEL), lambda i: (layer, 0, 0)),
                  vec(D_MODEL), vec(D_MODEL)],
        out_specs=rows(D_MODEL),
        out_shape=jax.ShapeDtypeStruct((SEQ, D_MODEL), F32),
        compiler_params=_cparams("parallel"),
        name="mix_out",
    )(attn, pool, lru, x, g_attn, g_pool, g_lru, w_out, g_post, gate)


def _mlp_down_kernel(u_ref, w_ref, x_ref, gpost_ref, gate_ref, o_ref):
    y = jnp.dot(u_ref[...], w_ref[...], preferred_element_type=F32)
    o_ref[...] = x_ref[...] + gate_ref[...] * _rms(y, gpost_ref[...])


def _mlp_down(u2, w_down, x, g_post, gate):
    tm = 256
    vec = pl.BlockSpec((1, D_MODEL), lambda i: (0, 0))
    return pl.pallas_call(
        _mlp_down_kernel,
        grid=(SEQ // tm,),
        in_specs=[pl.BlockSpec((tm, MLP_HIDDEN), lambda i: (i, 0)),
                  pl.BlockSpec((MLP_HIDDEN, D_MODEL), lambda i: (0, 0), pipeline_mode=pl.Buffered(1)),
                  pl.BlockSpec((tm, D_MODEL), lambda i: (i, 0)),
                  vec, vec],
        out_specs=pl.BlockSpec((tm, D_MODEL), lambda i: (i, 0)),
        out_shape=jax.ShapeDtypeStruct((SEQ, D_MODEL), F32),
        compiler_params=_cparams("parallel"),
        name="mlp_down",
    )(u2, w_down, x, g_post, gate)


def kernel(x, c, positions, rel_bias, w_mod, b_mod, g_pre_mix, g_post_mix, g_pre_mlp, g_post_mlp, w_in, pool_w, pool_scale, conv_w, conv_b, gate_r_w, gate_r_b, gate_i_w, gate_i_b, lru_lambda, g_attn_out, g_pool_out, g_lru_out, w_out, w_mlp_up, w_mlp_down):
    assert x.shape == (1, SEQ, D_MODEL) and positions.shape == (1, SEQ)
    del positions
    xs = x[0]
    mod = _modulation(c, w_mod, b_mod)
    band = _bias_band(rel_bias)
    row = lambda v: v.reshape(1, -1)
    n_attn_cols = 4 * ATTN_WIDTH
    aux_lo = n_attn_cols + IDX_DIM + N_IDX_HEADS
    w_in_t = jnp.swapaxes(w_in, 1, 2).astype(BF16)
    w_aux = jnp.concatenate(
        [w_in_t[:, aux_lo:, :], w_in_t[:, n_attn_cols:aux_lo, :],
         jnp.zeros((DEPTH, AUX_WIDTH - (AUX_KIDX + IDX_DIM + N_IDX_HEADS), D_MODEL), BF16)], axis=1)
    w_out_b = w_out.astype(BF16)
    for l in range(DEPTH):
        shift1, scale1, gate1, shift2, scale2, gate2 = [
            mod[l, :, n * D_MODEL:(n + 1) * D_MODEL] for n in range(6)]
        g_pre = row(g_pre_mix[l])
        attn_grp, aux = _in_projection(xs, g_pre, scale1, shift1, w_in_t, n_attn_cols, w_aux, l)
        words = _select(attn_grp, aux)
        attn = _attention(attn_grp, words, band)
        pool, lru = _pool_lru(aux, pool_w[l], pool_scale[l], conv_w[l], conv_b[l], gate_r_w[l],
                              gate_r_b[l], gate_i_w[l], gate_i_b[l], lru_lambda[l])
        xs = _mix_out(attn, pool, lru, xs, row(g_attn_out[l]), row(g_pool_out[l]),
                      row(g_lru_out[l]), w_out_b, l, row(g_post_mix[l]), gate1)

        u2, w_down_b = _mlp_up(xs, row(g_pre_mlp[l]), scale2, shift2, w_mlp_up, w_mlp_down, l)
        xs = _mlp_down(u2, w_down_b, xs, row(g_post_mlp[l]), gate2)
    return xs[None]
```

```python
import functools
import math

import jax
import jax.numpy as jnp
from jax import lax
from jax.experimental import pallas as pl
from jax.experimental.pallas import tpu as pltpu

D_MODEL = 2048
SEQ = 8192
DEPTH = 2
ATTN_HEAD_DIM = 128
ATTN_WIDTH = 1024
N_ATTN_HEADS = 8
POOL_WIDTH = 512
POOL_WINDOWS = (2, 4, 8, 16)
POOL_GROUP = 128
LRU_WIDTH = 512
N_LRU_BLOCKS = 4
LRU_BLOCK = 128
LRU_C = 8.0
CONV_WIDTH = 4
N_IDX_HEADS = 16
IDX_DIM = 64
TOP_K = 256
N_BUCKETS = 32
MAX_DISTANCE = 128
MLP_HIDDEN = 4 * D_MODEL
NORM_EPS = 1e-6

AUX_POOL = 0
AUX_LRU_IN = 512
AUX_LRU_GATE = 1024
AUX_KIDX = 1536
AUX_WIDTH = 1664
LANE = 128
SUBLANE = 8

SEL_T = 256
SEL_GROUP = 2
SEL_SCORE_GROUP = 4
ATT_TQ = 512
ATT_TK = SEL_T
ATT_UNROLL = 4
N_BAND = ATT_TQ // ATT_TK + 2
assert MAX_DISTANCE <= ATT_TK // 2 and SEQ // SEL_T <= 32 and SEL_T >= TOP_K
MASK_NEG = -1e30
LOG2E = math.log2(math.e)
VMEM_LIMIT = 56 * 1024 * 1024

F32 = jnp.float32
BF16 = jnp.bfloat16
_NT = (((1,), (1,)), ((), ()))


def _cparams(*sem):
    return pltpu.CompilerParams(dimension_semantics=sem, vmem_limit_bytes=VMEM_LIMIT)


def _rms(x, g):
    ms = jnp.mean(x * x, axis=-1, keepdims=True)
    return (x * lax.rsqrt(ms + NORM_EPS)) * g


def _mod_kernel(c_ref, w_ref, b_ref, o_ref):
    c = c_ref[...]
    c_act = c * jax.nn.sigmoid(c)
    o_ref[0] = jnp.sum(c_act * w_ref[0], axis=0, keepdims=True) + b_ref[0]


def _modulation(c, w_mod, b_mod):
    tn = 1024
    n = w_mod.shape[-1]
    return pl.pallas_call(
        _mod_kernel,
        grid=(DEPTH, n // tn),
        in_specs=[
            pl.BlockSpec((D_MODEL, 1), lambda l, j: (0, 0)),
            pl.BlockSpec((1, D_MODEL, tn), lambda l, j: (l, 0, j)),
            pl.BlockSpec((1, 1, tn), lambda l, j: (l, 0, j)),
        ],
        out_specs=pl.BlockSpec((1, 1, tn), lambda l, j: (l, 0, j)),
        out_shape=jax.ShapeDtypeStruct((DEPTH, 1, n), F32),
        compiler_params=_cparams("parallel", "parallel"),
        name="modulation",
    )(c.reshape(D_MODEL, 1), w_mod, b_mod.reshape(DEPTH, 1, n))


def _mlp_up_kernel(x_ref, g_ref, sc_ref, sh_ref, w_ref, wd_ref, o_ref, wdb_ref, h_ref, *, tm):
    rc = 64

    @pl.when(pl.program_id(1) == 0)
    def _():
        g = g_ref[...]
        sc1 = 1.0 + sc_ref[...]
        sh = sh_ref[...]

        def body(r, carry):
            rows = pl.ds(pl.multiple_of(r * rc, rc), rc)
            h = _rms(x_ref[rows, :], g) * sc1 + sh
            h_ref[rows, :] = h.astype(BF16)
            return carry

        lax.fori_loop(0, tm // rc, body, 0)

    u = jnp.maximum(jnp.dot(h_ref[...], w_ref[...].astype(BF16), preferred_element_type=F32), 0.0)
    o_ref[...] = (u * u).astype(o_ref.dtype)
    wdb_ref[...] = wd_ref[...].astype(BF16)


def _mlp_up(x, g, scale, shift, w_up, w_down, layer):
    tm = tn = 1024
    m, k = x.shape
    n = w_up.shape[2]
    steps_n = n // tn
    steps = (m // tm) * steps_n
    slab = w_down.shape[1] // steps
    vec = pl.BlockSpec((1, k), lambda i, j: (0, 0))
    return pl.pallas_call(
        functools.partial(_mlp_up_kernel, tm=tm),
        grid=(m // tm, steps_n),
        in_specs=[
            pl.BlockSpec((tm, k), lambda i, j: (i, 0)),
            vec, vec, vec,
            pl.BlockSpec((None, k, tn), lambda i, j: (layer, 0, j)),
            pl.BlockSpec((None, slab, w_down.shape[2]), lambda i, j: (layer, i * steps_n + j, 0)),
        ],
        out_specs=[pl.BlockSpec((tm, tn), lambda i, j: (i, j)),
                   pl.BlockSpec((slab, w_down.shape[2]), lambda i, j: (i * steps_n + j, 0))],
        out_shape=[jax.ShapeDtypeStruct((m, n), BF16), jax.ShapeDtypeStruct(w_down.shape[1:], BF16)],
        scratch_shapes=[pltpu.VMEM((tm, k), BF16)],
        compiler_params=_cparams("parallel", "arbitrary"),
        name="mlp_up",
    )(x, g, scale, shift, w_up, w_down)


def _proj_kernel(x_ref, g_ref, sc_ref, sh_ref, wa_ref, wx_ref, oa_ref, ox_ref):
    h = (_rms(x_ref[...], g_ref[...]) * (1.0 + sc_ref[...]) + sh_ref[...]).astype(BF16)
    oa_ref[...] = lax.dot_general(h, wa_ref[...], _NT, preferred_element_type=F32).astype(oa_ref.dtype)
    ox_ref[...] = lax.dot_general(h, wx_ref[...], _NT, preferred_element_type=F32)


def _in_projection(x, g, scale, shift, w_attn_t, na, w_aux_t, layer):
    tm = 512
    m, k = x.shape
    nx = w_aux_t.shape[1]
    vec = pl.BlockSpec((1, k), lambda i: (0, 0))
    resident = lambda n: pl.BlockSpec((None, n, k), lambda i: (layer, 0, 0), pipeline_mode=pl.Buffered(1))
    return pl.pallas_call(
        _proj_kernel,
        grid=(m // tm,),
        in_specs=[pl.BlockSpec((tm, k), lambda i: (i, 0)), vec, vec, vec, resident(na), resident(nx)],
        out_specs=[pl.BlockSpec((tm, na), lambda i: (i, 0)), pl.BlockSpec((tm, nx), lambda i: (i, 0))],
        out_shape=[jax.ShapeDtypeStruct((m, na), BF16), jax.ShapeDtypeStruct((m, nx), F32)],
        compiler_params=_cparams("parallel"),
        name="in_proj",
    )(x, g, scale, shift, w_attn_t, w_aux_t)


def _band_kernel(rb_ref, o_ref):
    d = pl.program_id(0)
    t = lax.broadcasted_iota(jnp.int32, (ATT_TQ, ATT_TK), 0)
    s = lax.broadcasted_iota(jnp.int32, (ATT_TQ, ATT_TK), 1)
    rel = jnp.maximum((d - (ATT_TQ // ATT_TK - 1)) * ATT_TK + t - s, 0)
    max_exact = N_BUCKETS // 2
    relf = jnp.maximum(rel.astype(F32), 1.0)
    large = max_exact + (jnp.log(relf / max_exact) / math.log(MAX_DISTANCE / max_exact)
                         * (N_BUCKETS - max_exact)).astype(jnp.int32)
    large = jnp.minimum(large, N_BUCKETS - 1)
    bucket = jnp.where(rel < max_exact, rel, large)
    for h in range(N_ATTN_HEADS):
        val = jnp.zeros((ATT_TQ, ATT_TK), F32)
        for b in range(N_BUCKETS):
            val = jnp.where(bucket == b, rb_ref[b, h], val)
        o_ref[0, h] = val * LOG2E


def _bias_band(rel_bias):
    return pl.pallas_call(
        _band_kernel,
        grid=(N_BAND,),
        in_specs=[pl.BlockSpec(memory_space=pltpu.SMEM)],
        out_specs=pl.BlockSpec((1, N_ATTN_HEADS, ATT_TQ, ATT_TK), lambda d: (d, 0, 0, 0)),
        out_shape=jax.ShapeDtypeStruct((N_BAND, N_ATTN_HEADS, ATT_TQ, ATT_TK), F32),
        compiler_params=_cparams("parallel"),
        name="bias_band",
    )(rel_bias)


def _float_to_key(x):
    bits = lax.bitcast_convert_type(x, jnp.int32)
    return jnp.where(bits >= 0, bits, jnp.bitwise_xor(bits, jnp.int32(0x7FFFFFFF)))


_KEY_POS_INF = 0x7F800000
_KEY_NEG_INF = 0x807FFFFF - (1 << 32)


def _key_to_float(key):
    bits = jnp.where(key >= 0, key, jnp.bitwise_xor(key, jnp.int32(0x7FFFFFFF)))
    return lax.bitcast_convert_type(bits, F32)


def _select_kernel(qi_ref, kw_ref, o_ref, kb_ref, sc_ref, hi_ref, lo_ref):
    i = pl.program_id(0)
    tq = tk = SEL_T
    idx_scale = (N_IDX_HEADS * IDX_DIM) ** -0.5

    @pl.when(i == 0)
    def _():
        kb_ref[...] = kw_ref[:, 0:IDX_DIM].astype(BF16)

    w_t = jnp.transpose(kw_ref[pl.ds(pl.multiple_of(i * tq, tq), tq), :])

    nchunks = i + 1
    key_pos = lax.broadcasted_iota(jnp.int32, (tk, tq), 0)
    qry_pos = i * tq + lax.broadcasted_iota(jnp.int32, (tk, tq), 1)
    grp = SEL_GROUP

    def high_half(key):
        return lax.shift_right_arithmetic(key, jnp.int32(16))

    def score_tiles(j0, ntiles, below_diagonal, gmax):
        kc = kb_ref[pl.ds(pl.multiple_of(j0 * tk, tk), ntiles * tk), :]
        acc = jnp.zeros((ntiles * tk, tq), F32)
        for h in range(N_IDX_HEADS):
            d = lax.dot_general(kc, qi_ref[:, h * IDX_DIM:(h + 1) * IDX_DIM], _NT,
                                preferred_element_type=F32)
            acc = acc + w_t[IDX_DIM + h:IDX_DIM + h + 1, :] * jnp.maximum(d, 0.0)
        for u in range(ntiles):
            j = j0 + u
            score = acc[u * tk:(u + 1) * tk] * idx_scale
            if not below_diagonal:
                score = jnp.where(key_pos + j * tk <= qry_pos, score, -jnp.inf)
            sc_ref[j] = score
            gmax = jnp.maximum(gmax, score)
            key = _float_to_key(score)
            hi_ref[j] = high_half(key).astype(jnp.int16)
            lo_ref[j] = (jnp.bitwise_and(key, jnp.int32(0xFFFF)) - 32768).astype(jnp.int16)
        return gmax

    ngroups = (nchunks + grp - 1) // grp
    big = SEL_SCORE_GROUP
    nbig = i // big
    gmax = lax.fori_loop(0, nbig, lambda jg, gm: score_tiles(big * jg, big, True, gm),
                         jnp.full((tk, tq), -jnp.inf, F32))
    gmax = lax.fori_loop(nbig * (big // grp), ngroups,
                         lambda jg, gm: score_tiles(grp * jg, grp, False, gm), gmax)
    t_min = high_half(_float_to_key(jnp.min(gmax, axis=0, keepdims=True)))
    t_max = high_half(_float_to_key(jnp.max(gmax, axis=0, keepdims=True)))

    i16 = jnp.int16
    pack = 16
    one16 = jnp.ones((pack, tq), i16)
    zero16 = jnp.zeros((pack, tq), i16)

    def count_ge(ref, cand):
        cand_b = jnp.broadcast_to(cand, (pack, tq)).astype(i16)

        def body(jg, acc):
            parts = [acc, zero16, zero16, zero16]
            n = 0
            for u in range(grp):
                x = ref[grp * jg + u]
                for g in range(tk // pack):
                    hit = jnp.where(x[g * pack:(g + 1) * pack, :] >= cand_b, one16, zero16)
                    parts[n % 4] = parts[n % 4] + hit
                    n += 1
            return (parts[0] + parts[1]) + (parts[2] + parts[3])

        acc = lax.fori_loop(0, ngroups, body, zero16)
        return jnp.sum(acc.astype(jnp.int32).astype(F32), axis=0, keepdims=True)

    def kth_largest(ref, rank, lo0, hi0, stop_on_hit):
        def unsettled(carry):
            lo, hi, _ = carry
            return jnp.max((hi - lo).astype(F32)) > 0.0

        def halve(carry):
            lo, hi, above = carry
            mid = lo + lax.shift_right_logical(hi - lo + 1, jnp.int32(1))
            cnt = count_ge(ref, mid)
            ok = cnt >= rank
            lo, hi, above = jnp.where(ok, mid, lo), jnp.where(ok, hi, mid - 1), jnp.where(ok, above, cnt)
            if stop_on_hit:
                hit = cnt == rank
                lo, hi = jnp.where(hit, mid, lo), jnp.where(hit, mid, hi)
            return lo, hi, above

        def body(carry):
            return halve(halve(carry))

        lo, _, above = lax.while_loop(unsettled, body, (lo0, hi0, jnp.zeros((1, tq), F32)))
        return lo, above

    t_hi, above = kth_largest(hi_ref, jnp.full((1, tq), float(TOP_K), F32), t_min, t_max, False)
    t_hi_b = jnp.broadcast_to(t_hi, (tk, tq)).astype(i16)

    def narrow_body(j, carry):
        lo_ref[j] = jnp.where(hi_ref[j] == t_hi_b, lo_ref[j], jnp.full((tk, tq), -32768, i16))
        return carry

    lax.fori_loop(0, grp * ngroups, narrow_body, 0)
    t_lo, _ = kth_largest(lo_ref, float(TOP_K) - above, jnp.full((1, tq), -32768, jnp.int32),
                          jnp.full((1, tq), 32767, jnp.int32), True)
    key = jnp.bitwise_or(jnp.left_shift(t_hi, 16), t_lo + 32768)
    key = jnp.clip(key, _KEY_NEG_INF, _KEY_POS_INF)
    thr = jnp.broadcast_to(_key_to_float(key), (tk, tq))

    def word_body(j, word):
        return jnp.bitwise_or(word, jnp.where(sc_ref[j] >= thr, jnp.left_shift(jnp.int32(1), j), 0))

    word_t = lax.fori_loop(0, i, word_body, jnp.zeros((tk, tq), jnp.int32))
    keep = jnp.logical_and(sc_ref[i] >= thr, key_pos + i * tk <= qry_pos)
    word_t = jnp.bitwise_or(word_t, jnp.where(keep, jnp.left_shift(jnp.int32(1), i), 0))
    o_ref[...] = jnp.transpose(word_t)


def _select(attn_grp, aux):
    tq = tk = SEL_T
    nq, nk = SEQ // tq, SEQ // tk
    return pl.pallas_call(
        _select_kernel,
        grid=(nq,),
        in_specs=[
            pl.BlockSpec((tq, N_IDX_HEADS * IDX_DIM), lambda i: (i, 3)),
            pl.BlockSpec((SEQ, LANE), lambda i: (0, AUX_KIDX // LANE)),
        ],
        out_specs=pl.BlockSpec((tq, tk), lambda i: (i, 0)),
        out_shape=jax.ShapeDtypeStruct((SEQ, tk), jnp.int32),
        scratch_shapes=[
            pltpu.VMEM((SEQ, IDX_DIM), BF16),
            pltpu.VMEM((nk, tk, tq), F32),
            pltpu.VMEM((nk, tk, tq), jnp.int16),
            pltpu.VMEM((nk, tk, tq), jnp.int16),
        ],
        compiler_params=_cparams("arbitrary"),
        name="select",
    )(attn_grp, aux)


def _attn_kernel(q_ref, k_ref, v_ref, w_ref, band_ref, o_ref, mask_ref, sa_ref, sb_ref, acc_ref):
    i = pl.program_id(0)
    tq, tk, unroll = ATT_TQ, ATT_TK, ATT_UNROLL
    dh = ATTN_HEAD_DIM
    c = (ATTN_HEAD_DIM ** -0.5) * LOG2E
    r = tq // tk
    ngrp = (r * (i + 1) + unroll - 1) // unroll
    ones = jnp.ones((unroll * tk, LANE), BF16)

    @pl.when(pl.program_id(1) == 0)
    def _():
        def expand(j, carry):
            keep = jnp.bitwise_and(w_ref[...], jnp.left_shift(jnp.int32(1), j)) != 0
            mask_ref[j] = jnp.where(keep, 0.0, MASK_NEG)
            return carry

        lax.fori_loop(0, unroll * ngrp, expand, 0)

    def logits(g, s_ref, m_prev):
        mx = jnp.full((tq, LANE), MASK_NEG, F32)
        for u in range(unroll):
            j = unroll * g + u
            kt = k_ref[pl.ds(pl.multiple_of(j * tk, tk), tk), :]
            s2 = lax.dot_general(q_ref[...], kt, _NT, preferred_element_type=F32) * c
            s2 = s2 + band_ref[jnp.clip(r * i + (r - 1) - j, 0, N_BAND - 1)] + mask_ref[j]
            s_ref[u] = s2
            for cc in range(tk // LANE):
                mx = jnp.maximum(mx, s2[:, cc * LANE:(cc + 1) * LANE])
        return jnp.maximum(m_prev, jnp.max(mx, axis=1, keepdims=True))

    def values(g, s_ref, m_prev, m_cur):
        m_b = jnp.broadcast_to(m_cur, (tq, tk))
        p = jnp.concatenate([jnp.exp2(s_ref[u] - m_b).astype(BF16) for u in range(unroll)], axis=1)
        vt = v_ref[pl.ds(pl.multiple_of(g * (unroll * tk), unroll * tk), unroll * tk), :]
        pv = jnp.dot(p, jnp.concatenate([vt, ones], axis=1), preferred_element_type=F32)
        acc_ref[...] = acc_ref[...] * jnp.exp2(m_prev - m_cur) + pv

    acc_ref[...] = jnp.zeros(acc_ref.shape, F32)
    m_none = jnp.full((tq, 1), MASK_NEG, F32)
    m_0 = logits(0, sa_ref, m_none)
    npair = (ngrp - 1) // 2

    def pair(t, carry):
        m_prev, m_cur = carry
        m_1 = logits(2 * t + 1, sb_ref, m_cur)
        values(2 * t, sa_ref, m_prev, m_cur)
        m_2 = logits(2 * t + 2, sa_ref, m_1)
        values(2 * t + 1, sb_ref, m_cur, m_1)
        return m_1, m_2

    m_prev, m_cur = lax.fori_loop(0, npair, pair, (m_none, m_0))
    last = 2 * npair

    @pl.when(ngrp - 1 == last)
    def _():
        values(last, sa_ref, m_prev, m_cur)

    @pl.when(ngrp - 1 > last)
    def _():
        m_1 = logits(last + 1, sb_ref, m_cur)
        values(last, sa_ref, m_prev, m_cur)
        values(last + 1, sb_ref, m_cur, m_1)

    o_ref[...] = acc_ref[:, :dh] / acc_ref[:, dh:dh + 1]


def _attention(attn_grp, words, band):
    tq, tk = ATT_TQ, ATT_TK
    nq, nk = SEQ // tq, SEQ // tk
    dh = ATTN_HEAD_DIM
    nh = N_ATTN_HEADS
    return pl.pallas_call(
        _attn_kernel,
        grid=(nq, nh),
        in_specs=[
            pl.BlockSpec((tq, dh), lambda i, h: (i, h)),
            pl.BlockSpec((SEQ, dh), lambda i, h: (0, nh + h)),
            pl.BlockSpec((SEQ, dh), lambda i, h: (0, 2 * nh + h)),
            pl.BlockSpec((tq, tk), lambda i, h: (i, 0)),
            pl.BlockSpec((N_BAND, None, tq, tk), lambda i, h: (0, h, 0, 0)),
        ],
        out_specs=pl.BlockSpec((tq, dh), lambda i, h: (i, h)),
        out_shape=jax.ShapeDtypeStruct((SEQ, ATTN_WIDTH), F32),
        scratch_shapes=[pltpu.VMEM((nk, tq, tk), F32),
                        pltpu.VMEM((ATT_UNROLL, tq, tk), F32), pltpu.VMEM((ATT_UNROLL, tq, tk), F32),
                        pltpu.VMEM((tq, dh + LANE), F32)],
        compiler_params=_cparams("parallel", "arbitrary"),
        name="attention",
    )(attn_grp, attn_grp, attn_grp, words, band)


POOL_HALO = 16
CONV_HALO = 8


def _block_diag_dot(x, w_ref):
    outs = []
    for g in range(w_ref.shape[0]):
        xs = x[:, g * LRU_BLOCK:(g + 1) * LRU_BLOCK].astype(BF16)
        outs.append(jnp.dot(xs, w_ref[g], preferred_element_type=F32))
    return jnp.concatenate(outs, axis=1)


def _pool_lru_kernel(pin_ref, lin_ref, gate_ref, pw_ref, ps_ref, cw_ref, cb_ref, rw_ref, rb_ref,
                     iw_ref, ib_ref, lam_ref, pool_o, lru_o, pprev_ref, lprev_ref, h_ref, *, T):
    i = pl.program_id(0)

    @pl.when(i == 0)
    def _():
        pprev_ref[...] = jnp.zeros(pprev_ref.shape, F32)
        lprev_ref[...] = jnp.zeros(lprev_ref.shape, F32)
        h_ref[...] = jnp.zeros(h_ref.shape, F32)

    t_glob = i * T + lax.broadcasted_iota(jnp.int32, (T, 1), 0)

    u = pin_ref[...]
    ext = jnp.concatenate([pprev_ref[...], u], axis=0)
    pprev_ref[...] = u[T - POOL_HALO:, :]
    p2 = ext[1:] + ext[:-1]
    p4 = p2[2:, POOL_GROUP:] + p2[:-2, POOL_GROUP:]
    p8 = p4[4:, POOL_GROUP:] + p4[:-4, POOL_GROUP:]
    p16 = p8[8:, POOL_GROUP:] + p8[:-8, POOL_GROUP:]
    wsums = (p2[15:15 + T, :POOL_GROUP], p4[13:13 + T, :POOL_GROUP],
             p8[9:9 + T, :POOL_GROUP], p16[1:1 + T, :])
    for g, win in enumerate(POOL_WINDOWS):
        gs = slice(g * POOL_GROUP, (g + 1) * POOL_GROUP)
        count = jnp.minimum(t_glob + 1, win).astype(F32)
        dlt = wsums[g] / count - u[:, gs]
        y = jnp.dot(dlt.astype(BF16), pw_ref[g], preferred_element_type=F32)
        pool_o[:, gs] = y * ps_ref[:, gs]

    x = lin_ref[...]
    lext = jnp.concatenate([lprev_ref[...], x], axis=0)
    lprev_ref[...] = x[T - CONV_HALO:, :]
    xc = cb_ref[...]
    for jj in range(CONV_WIDTH):
        off = CONV_HALO - (CONV_WIDTH - 1) + jj
        xc = xc + cw_ref[jj:jj + 1, :] * lext[off:off + T, :]
    r = jax.nn.sigmoid(_block_diag_dot(xc, rw_ref) + rb_ref[...])
    ig = jax.nn.sigmoid(_block_diag_dot(xc, iw_ref) + ib_ref[...])
    z = -lam_ref[...]
    softplus = jnp.maximum(z, 0.0) + jnp.log1p(jnp.exp(-jnp.abs(z)))
    log_a = (-LRU_C * r) * softplus
    a = jnp.exp(log_a)
    b = jnp.sqrt(-jnp.tanh(log_a) * (a * a + 1.0)) * (ig * xc)

    ngrp = T // SUBLANE
    a3 = a.reshape(ngrp, SUBLANE, LRU_WIDTH)
    b3 = b.reshape(ngrp, SUBLANE, LRU_WIDTH)
    sub = lax.broadcasted_iota(jnp.int32, a3.shape, 1)
    sh = 1
    while sh < SUBLANE:
        a_sh = jnp.where(sub >= sh, pltpu.roll(a3, sh, axis=1), 1.0)
        b_sh = jnp.where(sub >= sh, pltpu.roll(b3, sh, axis=1), 0.0)
        b3 = a3 * b_sh + b3
        a3 = a3 * a_sh
        sh *= 2
    carry = h_ref[...]
    rows = []
    for grp in range(ngrp):
        hg = a3[grp] * carry + b3[grp]
        rows.append(hg)
        carry = hg[SUBLANE - 1:, :]
    h = jnp.concatenate(rows, axis=0)
    h_ref[...] = carry

    gt = gate_ref[...]
    cdf = 0.5 * (1.0 + jnp.tanh(math.sqrt(2.0 / math.pi) * (gt + 0.044715 * (gt * gt * gt))))
    lru_o[...] = h * (gt * cdf)


def _pool_lru(aux, pool_w, pool_scale, conv_w, conv_b, r_w, r_b, i_w, i_b, lam):
    T = 256
    full = lambda a: pl.BlockSpec(a.shape, lambda i: (0,) * a.ndim)
    row = lambda v: v.reshape(1, -1)
    args = (pool_w.astype(BF16), row(pool_scale), conv_w, row(conv_b), r_w.astype(BF16), row(r_b),
            i_w.astype(BF16), row(i_b), row(lam))
    return pl.pallas_call(
        functools.partial(_pool_lru_kernel, T=T),
        grid=(SEQ // T,),
        in_specs=[
            pl.BlockSpec((T, POOL_WIDTH), lambda i: (i, AUX_POOL // POOL_WIDTH)),
            pl.BlockSpec((T, LRU_WIDTH), lambda i: (i, AUX_LRU_IN // LRU_WIDTH)),
            pl.BlockSpec((T, LRU_WIDTH), lambda i: (i, AUX_LRU_GATE // LRU_WIDTH)),
        ] + [full(a) for a in args],
        out_specs=[pl.BlockSpec((T, POOL_WIDTH), lambda i: (i, 0)),
                   pl.BlockSpec((T, LRU_WIDTH), lambda i: (i, 0))],
        out_shape=[jax.ShapeDtypeStruct((SEQ, POOL_WIDTH), F32),
                   jax.ShapeDtypeStruct((SEQ, LRU_WIDTH), F32)],
        scratch_shapes=[pltpu.VMEM((POOL_HALO, POOL_WIDTH), F32),
                        pltpu.VMEM((CONV_HALO, LRU_WIDTH), F32),
                        pltpu.VMEM((1, LRU_WIDTH), F32)],
        compiler_params=_cparams("arbitrary"),
        name="pool_lru",
    )(aux, aux, aux, *args)


def _mix_out_kernel(a_ref, p_ref, l_ref, x_ref, ga_ref, gp_ref, gl_ref, w_ref, gpost_ref, gate_ref,
                    o_ref):
    na = _rms(a_ref[...], ga_ref[...]).astype(BF16)
    npool = _rms(p_ref[...], gp_ref[...]).astype(BF16)
    nl = _rms(l_ref[...], gl_ref[...]).astype(BF16)
    p0, p1 = ATTN_WIDTH, ATTN_WIDTH + POOL_WIDTH
    y = jnp.dot(na, w_ref[0:p0, :], preferred_element_type=F32)
    y = y + jnp.dot(npool, w_ref[p0:p1, :], preferred_element_type=F32)
    y = y + jnp.dot(nl, w_ref[p1:, :], preferred_element_type=F32)
    o_ref[...] = x_ref[...] + gate_ref[...] * _rms(y, gpost_ref[...])


def _mix_out(attn, pool, lru, x, g_attn, g_pool, g_lru, w_out, layer, g_post, gate):
    tm = 512
    vec = lambda n: pl.BlockSpec((1, n), lambda i: (0, 0))
    rows = lambda n: pl.BlockSpec((tm, n), lambda i: (i, 0))
    return pl.pallas_call(
        _mix_out_kernel,
        grid=(SEQ // tm,),
        in_specs=[rows(ATTN_WIDTH), rows(POOL_WIDTH), rows(LRU_WIDTH), rows(D_MODEL),
                  vec(ATTN_WIDTH), vec(POOL_WIDTH), vec(LRU_WIDTH),
                  pl.BlockSpec((None, D_MODEL, D_MODEL), lambda i: (layer, 0, 0)),
                  vec(D_MODEL), vec(D_MODEL)],
        out_specs=rows(D_MODEL),
        out_shape=jax.ShapeDtypeStruct((SEQ, D_MODEL), F32),
        compiler_params=_cparams("parallel"),
        name="mix_out",
    )(attn, pool, lru, x, g_attn, g_pool, g_lru, w_out, g_post, gate)


def _mlp_down_kernel(u_ref, w_ref, x_ref, gpost_ref, gate_ref, o_ref):
    y = jnp.dot(u_ref[...], w_ref[...], preferred_element_type=F32)
    o_ref[...] = x_ref[...] + gate_ref[...] * _rms(y, gpost_ref[...])


def _mlp_down(u2, w_down, x, g_post, gate):
    tm = 256
    vec = pl.BlockSpec((1, D_MODEL), lambda i: (0, 0))
    return pl.pallas_call(
        _mlp_down_kernel,
        grid=(SEQ // tm,),
        in_specs=[pl.BlockSpec((tm, MLP_HIDDEN), lambda i: (i, 0)),
                  pl.BlockSpec((MLP_HIDDEN, D_MODEL), lambda i: (0, 0), pipeline_mode=pl.Buffered(1)),
                  pl.BlockSpec((tm, D_MODEL), lambda i: (i, 0)),
                  vec, vec],
        out_specs=pl.BlockSpec((tm, D_MODEL), lambda i: (i, 0)),
        out_shape=jax.ShapeDtypeStruct((SEQ, D_MODEL), F32),
        compiler_params=_cparams("parallel"),
        name="mlp_down",
    )(u2, w_down, x, g_post, gate)


def kernel(x, c, positions, rel_bias, w_mod, b_mod, g_pre_mix, g_post_mix, g_pre_mlp, g_post_mlp, w_in, pool_w, pool_scale, conv_w, conv_b, gate_r_w, gate_r_b, gate_i_w, gate_i_b, lru_lambda, g_attn_out, g_pool_out, g_lru_out, w_out, w_mlp_up, w_mlp_down):
    assert x.shape == (1, SEQ, D_MODEL) and positions.shape == (1, SEQ)
    del positions
    xs = x[0]
    mod = _modulation(c, w_mod, b_mod)
    band = _bias_band(rel_bias)
    row = lambda v: v.reshape(1, -1)
    n_attn_cols = 4 * ATTN_WIDTH
    aux_lo = n_attn_cols + IDX_DIM + N_IDX_HEADS
    w_in_t = jnp.swapaxes(w_in, 1, 2).astype(BF16)
    w_aux = jnp.concatenate(
        [w_in_t[:, aux_lo:, :], w_in_t[:, n_attn_cols:aux_lo, :],
         jnp.zeros((DEPTH, AUX_WIDTH - (AUX_KIDX + IDX_DIM + N_IDX_HEADS), D_MODEL), BF16)], axis=1)
    w_out_b = w_out.astype(BF16)
    for l in range(DEPTH):
        shift1, scale1, gate1, shift2, scale2, gate2 = [
            mod[l, :, n * D_MODEL:(n + 1) * D_MODEL] for n in range(6)]
        g_pre = row(g_pre_mix[l])
        attn_grp, aux = _in_projection(xs, g_pre, scale1, shift1, w_in_t, n_attn_cols, w_aux, l)
        words = _select(attn_grp, aux)
        attn = _attention(attn_grp, words, band)
        pool, lru = _pool_lru(aux, pool_w[l], pool_scale[l], conv_w[l], conv_b[l], gate_r_w[l],
                              gate_r_b[l], gate_i_w[l], gate_i_b[l], lru_lambda[l])
        xs = _mix_out(attn, pool, lru, xs, row(g_attn_out[l]), row(g_pool_out[l]),
                      row(g_lru_out[l]), w_out_b, l, row(g_post_mix[l]), gate1)

        u2, w_down_b = _mlp_up(xs, row(g_pre_mlp[l]), scale2, shift2, w_mlp_up, w_mlp_down, l)
        xs = _mlp_down(u2, w_down_b, xs, row(g_post_mlp[l]), gate2)
    return xs[None]
```

```python
import functools
import math

import jax
import jax.numpy as jnp
from jax import lax
from jax.experimental import pallas as pl
from jax.experimental.pallas import tpu as pltpu

D_MODEL = 2048
SEQ = 8192
DEPTH = 2
ATTN_HEAD_DIM = 128
ATTN_WIDTH = 1024
N_ATTN_HEADS = 8
POOL_WIDTH = 512
POOL_WINDOWS = (2, 4, 8, 16)
POOL_GROUP = 128
LRU_WIDTH = 512
N_LRU_BLOCKS = 4
LRU_BLOCK = 128
LRU_C = 8.0
CONV_WIDTH = 4
N_IDX_HEADS = 16
IDX_DIM = 64
TOP_K = 256
N_BUCKETS = 32
MAX_DISTANCE = 128
MLP_HIDDEN = 4 * D_MODEL
NORM_EPS = 1e-6

AUX_POOL = 0
AUX_LRU_IN = 512
AUX_LRU_GATE = 1024
AUX_KIDX = 1536
AUX_WIDTH = 1664
LANE = 128
SUBLANE = 8

SEL_T = 256
SEL_GROUP = 2
SEL_SCORE_GROUP = 4
ATT_TQ = 512
ATT_TK = SEL_T
ATT_UNROLL = 4
N_BAND = ATT_TQ // ATT_TK + 2
assert MAX_DISTANCE <= ATT_TK // 2 and SEQ // SEL_T <= 32 and SEL_T >= TOP_K
MASK_NEG = -1e30
LOG2E = math.log2(math.e)
VMEM_LIMIT = 56 * 1024 * 1024

F32 = jnp.float32
BF16 = jnp.bfloat16
_NT = (((1,), (1,)), ((), ()))


def _cparams(*sem):
    return pltpu.CompilerParams(dimension_semantics=sem, vmem_limit_bytes=VMEM_LIMIT)


def _rms(x, g):
    ms = jnp.mean(x * x, axis=-1, keepdims=True)
    return (x * lax.rsqrt(ms + NORM_EPS)) * g


def _mod_kernel(c_ref, w_ref, b_ref, o_ref):
    c = c_ref[...]
    c_act = c * jax.nn.sigmoid(c)
    o_ref[0] = jnp.sum(c_act * w_ref[0], axis=0, keepdims=True) + b_ref[0]


def _modulation(c, w_mod, b_mod):
    tn = 1024
    n = w_mod.shape[-1]
    return pl.pallas_call(
        _mod_kernel,
        grid=(DEPTH, n // tn),
        in_specs=[
            pl.BlockSpec((D_MODEL, 1), lambda l, j: (0, 0)),
            pl.BlockSpec((1, D_MODEL, tn), lambda l, j: (l, 0, j)),
            pl.BlockSpec((1, 1, tn), lambda l, j: (l, 0, j)),
        ],
        out_specs=pl.BlockSpec((1, 1, tn), lambda l, j: (l, 0, j)),
        out_shape=jax.ShapeDtypeStruct((DEPTH, 1, n), F32),
        compiler_params=_cparams("parallel", "parallel"),
        name="modulation",
    )(c.reshape(D_MODEL, 1), w_mod, b_mod.reshape(DEPTH, 1, n))


def _mlp_up_kernel(h_ref, w_ref, wd_ref, o_ref, wdb_ref):
    u = jnp.maximum(jnp.dot(h_ref[...], w_ref[...], preferred_element_type=F32), 0.0)
    o_ref[...] = (u * u).astype(o_ref.dtype)
    wdb_ref[...] = wd_ref[...].astype(BF16)


def _mlp_up(h, w_up_b, w_down, layer):
    tm = 512
    m, k = h.shape
    n = w_up_b.shape[1]
    halves = 2
    tn = n // halves
    steps_m = m // tm
    slab = w_down.shape[1] // (halves * steps_m)
    return pl.pallas_call(
        _mlp_up_kernel,
        grid=(halves, steps_m),
        in_specs=[
            pl.BlockSpec((tm, k), lambda j, i: (i, 0)),
            pl.BlockSpec((k, tn), lambda j, i: (0, j), pipeline_mode=pl.Buffered(1)),
            pl.BlockSpec((None, slab, w_down.shape[2]), lambda j, i: (layer, j * steps_m + i, 0)),
        ],
        out_specs=[pl.BlockSpec((tm, tn), lambda j, i: (i, j)),
                   pl.BlockSpec((slab, w_down.shape[2]), lambda j, i: (j * steps_m + i, 0))],
        out_shape=[jax.ShapeDtypeStruct((m, n), BF16), jax.ShapeDtypeStruct(w_down.shape[1:], BF16)],
        compiler_params=_cparams("parallel", "parallel"),
        name="mlp_up",
    )(h, w_up_b, w_down)


def _proj_kernel(x_ref, g_ref, sc_ref, sh_ref, wa_ref, wx_ref, oa_ref, ox_ref):
    h = (_rms(x_ref[...], g_ref[...]) * (1.0 + sc_ref[...]) + sh_ref[...]).astype(BF16)
    oa_ref[...] = lax.dot_general(h, wa_ref[...], _NT, preferred_element_type=F32).astype(oa_ref.dtype)
    ox_ref[...] = lax.dot_general(h, wx_ref[...], _NT, preferred_element_type=F32)


def _in_projection(x, g, scale, shift, w_attn_t, na, w_aux_t, layer):
    tm = 512
    m, k = x.shape
    nx = w_aux_t.shape[1]
    vec = pl.BlockSpec((1, k), lambda i: (0, 0))
    resident = lambda n: pl.BlockSpec((None, n, k), lambda i: (layer, 0, 0), pipeline_mode=pl.Buffered(1))
    return pl.pallas_call(
        _proj_kernel,
        grid=(m // tm,),
        in_specs=[pl.BlockSpec((tm, k), lambda i: (i, 0)), vec, vec, vec, resident(na), resident(nx)],
        out_specs=[pl.BlockSpec((tm, na), lambda i: (i, 0)), pl.BlockSpec((tm, nx), lambda i: (i, 0))],
        out_shape=[jax.ShapeDtypeStruct((m, na), BF16), jax.ShapeDtypeStruct((m, nx), F32)],
        compiler_params=_cparams("parallel"),
        name="in_proj",
    )(x, g, scale, shift, w_attn_t, w_aux_t)


def _band_kernel(rb_ref, o_ref):
    d = pl.program_id(0)
    t = lax.broadcasted_iota(jnp.int32, (ATT_TQ, ATT_TK), 0)
    s = lax.broadcasted_iota(jnp.int32, (ATT_TQ, ATT_TK), 1)
    rel = jnp.maximum((d - (ATT_TQ // ATT_TK - 1)) * ATT_TK + t - s, 0)
    max_exact = N_BUCKETS // 2
    relf = jnp.maximum(rel.astype(F32), 1.0)
    large = max_exact + (jnp.log(relf / max_exact) / math.log(MAX_DISTANCE / max_exact)
                         * (N_BUCKETS - max_exact)).astype(jnp.int32)
    large = jnp.minimum(large, N_BUCKETS - 1)
    bucket = jnp.where(rel < max_exact, rel, large)
    for h in range(N_ATTN_HEADS):
        val = jnp.zeros((ATT_TQ, ATT_TK), F32)
        for b in range(N_BUCKETS):
            val = jnp.where(bucket == b, rb_ref[b, h], val)
        o_ref[0, h] = val * LOG2E


def _bias_band(rel_bias):
    return pl.pallas_call(
        _band_kernel,
        grid=(N_BAND,),
        in_specs=[pl.BlockSpec(memory_space=pltpu.SMEM)],
        out_specs=pl.BlockSpec((1, N_ATTN_HEADS, ATT_TQ, ATT_TK), lambda d: (d, 0, 0, 0)),
        out_shape=jax.ShapeDtypeStruct((N_BAND, N_ATTN_HEADS, ATT_TQ, ATT_TK), F32),
        compiler_params=_cparams("parallel"),
        name="bias_band",
    )(rel_bias)


def _float_to_key(x):
    bits = lax.bitcast_convert_type(x, jnp.int32)
    return jnp.where(bits >= 0, bits, jnp.bitwise_xor(bits, jnp.int32(0x7FFFFFFF)))


_KEY_POS_INF = 0x7F800000
_KEY_NEG_INF = 0x807FFFFF - (1 << 32)


def _key_to_float(key):
    bits = jnp.where(key >= 0, key, jnp.bitwise_xor(key, jnp.int32(0x7FFFFFFF)))
    return lax.bitcast_convert_type(bits, F32)


def _select_kernel(qi_ref, kw_ref, o_ref, kb_ref, sc_ref, hi_ref, lo_ref):
    i = pl.program_id(0)
    tq = tk = SEL_T
    idx_scale = (N_IDX_HEADS * IDX_DIM) ** -0.5

    @pl.when(i == 0)
    def _():
        kb_ref[...] = kw_ref[:, 0:IDX_DIM].astype(BF16)

    w_t = jnp.transpose(kw_ref[pl.ds(pl.multiple_of(i * tq, tq), tq), :])

    nchunks = i + 1
    key_pos = lax.broadcasted_iota(jnp.int32, (tk, tq), 0)
    qry_pos = i * tq + lax.broadcasted_iota(jnp.int32, (tk, tq), 1)
    grp = SEL_GROUP

    def high_half(key):
        return lax.shift_right_arithmetic(key, jnp.int32(16))

    def score_tiles(j0, ntiles, below_diagonal, gmax):
        kc = kb_ref[pl.ds(pl.multiple_of(j0 * tk, tk), ntiles * tk), :]
        acc = jnp.zeros((ntiles * tk, tq), F32)
        for h in range(N_IDX_HEADS):
            d = lax.dot_general(kc, qi_ref[:, h * IDX_DIM:(h + 1) * IDX_DIM], _NT,
                                preferred_element_type=F32)
            acc = acc + w_t[IDX_DIM + h:IDX_DIM + h + 1, :] * jnp.maximum(d, 0.0)
        for u in range(ntiles):
            j = j0 + u
            score = acc[u * tk:(u + 1) * tk] * idx_scale
            if not below_diagonal:
                score = jnp.where(key_pos + j * tk <= qry_pos, score, -jnp.inf)
            sc_ref[j] = score
            gmax = jnp.maximum(gmax, score)
            key = _float_to_key(score)
            hi_ref[j] = high_half(key).astype(jnp.int16)
            lo_ref[j] = (jnp.bitwise_and(key, jnp.int32(0xFFFF)) - 32768).astype(jnp.int16)
        return gmax

    ngroups = (nchunks + grp - 1) // grp
    big = SEL_SCORE_GROUP
    nbig = i // big
    gmax = lax.fori_loop(0, nbig, lambda jg, gm: score_tiles(big * jg, big, True, gm),
                         jnp.full((tk, tq), -jnp.inf, F32))
    gmax = lax.fori_loop(nbig * (big // grp), ngroups,
                         lambda jg, gm: score_tiles(grp * jg, grp, False, gm), gmax)
    t_min = high_half(_float_to_key(jnp.min(gmax, axis=0, keepdims=True)))
    t_max = high_half(_float_to_key(jnp.max(gmax, axis=0, keepdims=True)))

    i16 = jnp.int16
    pack = 16
    one16 = jnp.ones((pack, tq), i16)
    zero16 = jnp.zeros((pack, tq), i16)

    def count_ge(ref, cand):
        cand_b = jnp.broadcast_to(cand, (pack, tq)).astype(i16)

        def body(jg, acc):
            parts = [acc, zero16, zero16, zero16]
            n = 0
            for u in range(grp):
                x = ref[grp * jg + u]
                for g in range(tk // pack):
                    hit = jnp.where(x[g * pack:(g + 1) * pack, :] >= cand_b, one16, zero16)
                    parts[n % 4] = parts[n % 4] + hit
                    n += 1
            return (parts[0] + parts[1]) + (parts[2] + parts[3])

        acc = lax.fori_loop(0, ngroups, body, zero16)
        return jnp.sum(acc.astype(jnp.int32).astype(F32), axis=0, keepdims=True)

    def kth_largest(ref, rank, lo0, hi0, stop_on_hit):
        def unsettled(carry):
            lo, hi, _ = carry
            return jnp.max((hi - lo).astype(F32)) > 0.0

        def halve(carry):
            lo, hi, above = carry
            mid = lo + lax.shift_right_logical(hi - lo + 1, jnp.int32(1))
            cnt = count_ge(ref, mid)
            ok = cnt >= rank
            lo, hi, above = jnp.where(ok, mid, lo), jnp.where(ok, hi, mid - 1), jnp.where(ok, above, cnt)
            if stop_on_hit:
                hit = cnt == rank
                lo, hi = jnp.where(hit, mid, lo), jnp.where(hit, mid, hi)
            return lo, hi, above

        def body(carry):
            return halve(halve(carry))

        lo, _, above = lax.while_loop(unsettled, body, (lo0, hi0, jnp.zeros((1, tq), F32)))
        return lo, above

    t_hi, above = kth_largest(hi_ref, jnp.full((1, tq), float(TOP_K), F32), t_min, t_max, False)
    t_hi_b = jnp.broadcast_to(t_hi, (tk, tq)).astype(i16)

    def narrow_body(j, carry):
        lo_ref[j] = jnp.where(hi_ref[j] == t_hi_b, lo_ref[j], jnp.full((tk, tq), -32768, i16))
        return carry

    lax.fori_loop(0, grp * ngroups, narrow_body, 0)
    t_lo, _ = kth_largest(lo_ref, float(TOP_K) - above, jnp.full((1, tq), -32768, jnp.int32),
                          jnp.full((1, tq), 32767, jnp.int32), True)
    key = jnp.bitwise_or(jnp.left_shift(t_hi, 16), t_lo + 32768)
    key = jnp.clip(key, _KEY_NEG_INF, _KEY_POS_INF)
    thr = jnp.broadcast_to(_key_to_float(key), (tk, tq))

    def word_body(j, word):
        return jnp.bitwise_or(word, jnp.where(sc_ref[j] >= thr, jnp.left_shift(jnp.int32(1), j), 0))

    word_t = lax.fori_loop(0, i, word_body, jnp.zeros((tk, tq), jnp.int32))
    keep = jnp.logical_and(sc_ref[i] >= thr, key_pos + i * tk <= qry_pos)
    word_t = jnp.bitwise_or(word_t, jnp.where(keep, jnp.left_shift(jnp.int32(1), i), 0))
    o_ref[...] = jnp.transpose(word_t)


def _select(attn_grp, aux):
    tq = tk = SEL_T
    nq, nk = SEQ // tq, SEQ // tk
    return pl.pallas_call(
        _select_kernel,
        grid=(nq,),
        in_specs=[
            pl.BlockSpec((tq, N_IDX_HEADS * IDX_DIM), lambda i: (i, 3)),
            pl.BlockSpec((SEQ, LANE), lambda i: (0, AUX_KIDX // LANE)),
        ],
        out_specs=pl.BlockSpec((tq, tk), lambda i: (i, 0)),
        out_shape=jax.ShapeDtypeStruct((SEQ, tk), jnp.int32),
        scratch_shapes=[
            pltpu.VMEM((SEQ, IDX_DIM), BF16),
            pltpu.VMEM((nk, tk, tq), F32),
            pltpu.VMEM((nk, tk, tq), jnp.int16),
            pltpu.VMEM((nk, tk, tq), jnp.int16),
        ],
        compiler_params=_cparams("arbitrary"),
        name="select",
    )(attn_grp, aux)


def _attn_kernel(q_ref, k_ref, v_ref, w_ref, band_ref, o_ref, mask_ref, sa_ref, sb_ref, acc_ref):
    i = pl.program_id(0)
    tq, tk, unroll = ATT_TQ, ATT_TK, ATT_UNROLL
    dh = ATTN_HEAD_DIM
    c = (ATTN_HEAD_DIM ** -0.5) * LOG2E
    r = tq // tk
    ngrp = (r * (i + 1) + unroll - 1) // unroll
    ones = jnp.ones((unroll * tk, LANE), BF16)

    @pl.when(pl.program_id(1) == 0)
    def _():
        def expand(j, carry):
            keep = jnp.bitwise_and(w_ref[...], jnp.left_shift(jnp.int32(1), j)) != 0
            mask_ref[j] = jnp.where(keep, 0.0, MASK_NEG)
            return carry

        lax.fori_loop(0, unroll * ngrp, expand, 0)

    def logits(g, s_ref, m_prev):
        mx = jnp.full((tq, LANE), MASK_NEG, F32)
        for u in range(unroll):
            j = unroll * g + u
            kt = k_ref[pl.ds(pl.multiple_of(j * tk, tk), tk), :]
            s2 = lax.dot_general(q_ref[...], kt, _NT, preferred_element_type=F32) * c
            s2 = s2 + band_ref[jnp.clip(r * i + (r - 1) - j, 0, N_BAND - 1)] + mask_ref[j]
            s_ref[u] = s2
            for cc in range(tk // LANE):
                mx = jnp.maximum(mx, s2[:, cc * LANE:(cc + 1) * LANE])
        return jnp.maximum(m_prev, jnp.max(mx, axis=1, keepdims=True))

    def values(g, s_ref, m_prev, m_cur):
        m_b = jnp.broadcast_to(m_cur, (tq, tk))
        p = jnp.concatenate([jnp.exp2(s_ref[u] - m_b).astype(BF16) for u in range(unroll)], axis=1)
        vt = v_ref[pl.ds(pl.multiple_of(g * (unroll * tk), unroll * tk), unroll * tk), :]
        pv = jnp.dot(p, jnp.concatenate([vt, ones], axis=1), preferred_element_type=F32)
        acc_ref[...] = acc_ref[...] * jnp.exp2(m_prev - m_cur) + pv

    acc_ref[...] = jnp.zeros(acc_ref.shape, F32)
    m_none = jnp.full((tq, 1), MASK_NEG, F32)
    m_0 = logits(0, sa_ref, m_none)
    npair = (ngrp - 1) // 2

    def pair(t, carry):
        m_prev, m_cur = carry
        m_1 = logits(2 * t + 1, sb_ref, m_cur)
        values(2 * t, sa_ref, m_prev, m_cur)
        m_2 = logits(2 * t + 2, sa_ref, m_1)
        values(2 * t + 1, sb_ref, m_cur, m_1)
        return m_1, m_2

    m_prev, m_cur = lax.fori_loop(0, npair, pair, (m_none, m_0))
    last = 2 * npair

    @pl.when(ngrp - 1 == last)
    def _():
        values(last, sa_ref, m_prev, m_cur)

    @pl.when(ngrp - 1 > last)
    def _():
        m_1 = logits(last + 1, sb_ref, m_cur)
        values(last, sa_ref, m_prev, m_cur)
        values(last + 1, sb_ref, m_cur, m_1)

    o_ref[...] = acc_ref[:, :dh] / acc_ref[:, dh:dh + 1]


def _attention(attn_grp, words, band):
    tq, tk = ATT_TQ, ATT_TK
    nq, nk = SEQ // tq, SEQ // tk
    dh = ATTN_HEAD_DIM
    nh = N_ATTN_HEADS
    return pl.pallas_call(
        _attn_kernel,
        grid=(nq, nh),
        in_specs=[
            pl.BlockSpec((tq, dh), lambda i, h: (i, h)),
            pl.BlockSpec((SEQ, dh), lambda i, h: (0, nh + h)),
            pl.BlockSpec((SEQ, dh), lambda i, h: (0, 2 * nh + h)),
            pl.BlockSpec((tq, tk), lambda i, h: (i, 0)),
            pl.BlockSpec((N_BAND, None, tq, tk), lambda i, h: (0, h, 0, 0)),
        ],
        out_specs=pl.BlockSpec((tq, dh), lambda i, h: (i, h)),
        out_shape=jax.ShapeDtypeStruct((SEQ, ATTN_WIDTH), F32),
        scratch_shapes=[pltpu.VMEM((nk, tq, tk), F32),
                        pltpu.VMEM((ATT_UNROLL, tq, tk), F32), pltpu.VMEM((ATT_UNROLL, tq, tk), F32),
                        pltpu.VMEM((tq, dh + LANE), F32)],
        compiler_params=_cparams("parallel", "arbitrary"),
        name="attention",
    )(attn_grp, attn_grp, attn_grp, words, band)


POOL_HALO = 16
CONV_HALO = 8


def _block_diag_dot(x, w_ref):
    outs = []
    for g in range(w_ref.shape[0]):
        xs = x[:, g * LRU_BLOCK:(g + 1) * LRU_BLOCK].astype(BF16)
        outs.append(jnp.dot(xs, w_ref[g], preferred_element_type=F32))
    return jnp.concatenate(outs, axis=1)


def _pool_lru_kernel(pin_ref, lin_ref, gate_ref, pw_ref, ps_ref, cw_ref, cb_ref, rw_ref, rb_ref,
                     iw_ref, ib_ref, lam_ref, pool_o, lru_o, pprev_ref, lprev_ref, h_ref, *, T):
    i = pl.program_id(0)

    @pl.when(i == 0)
    def _():
        pprev_ref[...] = jnp.zeros(pprev_ref.shape, F32)
        lprev_ref[...] = jnp.zeros(lprev_ref.shape, F32)
        h_ref[...] = jnp.zeros(h_ref.shape, F32)

    t_glob = i * T + lax.broadcasted_iota(jnp.int32, (T, 1), 0)

    u = pin_ref[...]
    ext = jnp.concatenate([pprev_ref[...], u], axis=0)
    pprev_ref[...] = u[T - POOL_HALO:, :]
    p2 = ext[1:] + ext[:-1]
    p4 = p2[2:, POOL_GROUP:] + p2[:-2, POOL_GROUP:]
    p8 = p4[4:, POOL_GROUP:] + p4[:-4, POOL_GROUP:]
    p16 = p8[8:, POOL_GROUP:] + p8[:-8, POOL_GROUP:]
    wsums = (p2[15:15 + T, :POOL_GROUP], p4[13:13 + T, :POOL_GROUP],
             p8[9:9 + T, :POOL_GROUP], p16[1:1 + T, :])
    for g, win in enumerate(POOL_WINDOWS):
        gs = slice(g * POOL_GROUP, (g + 1) * POOL_GROUP)
        count = jnp.minimum(t_glob + 1, win).astype(F32)
        dlt = wsums[g] / count - u[:, gs]
        y = jnp.dot(dlt.astype(BF16), pw_ref[g], preferred_element_type=F32)
        pool_o[:, gs] = y * ps_ref[:, gs]

    x = lin_ref[...]
    lext = jnp.concatenate([lprev_ref[...], x], axis=0)
    lprev_ref[...] = x[T - CONV_HALO:, :]
    xc = cb_ref[...]
    for jj in range(CONV_WIDTH):
        off = CONV_HALO - (CONV_WIDTH - 1) + jj
        xc = xc + cw_ref[jj:jj + 1, :] * lext[off:off + T, :]
    r = jax.nn.sigmoid(_block_diag_dot(xc, rw_ref) + rb_ref[...])
    ig = jax.nn.sigmoid(_block_diag_dot(xc, iw_ref) + ib_ref[...])
    z = -lam_ref[...]
    softplus = jnp.maximum(z, 0.0) + jnp.log1p(jnp.exp(-jnp.abs(z)))
    log_a = (-LRU_C * r) * softplus
    a = jnp.exp(log_a)
    b = jnp.sqrt(-jnp.tanh(log_a) * (a * a + 1.0)) * (ig * xc)

    ngrp = T // SUBLANE
    a3 = a.reshape(ngrp, SUBLANE, LRU_WIDTH)
    b3 = b.reshape(ngrp, SUBLANE, LRU_WIDTH)
    sub = lax.broadcasted_iota(jnp.int32, a3.shape, 1)
    sh = 1
    while sh < SUBLANE:
        a_sh = jnp.where(sub >= sh, pltpu.roll(a3, sh, axis=1), 1.0)
        b_sh = jnp.where(sub >= sh, pltpu.roll(b3, sh, axis=1), 0.0)
        b3 = a3 * b_sh + b3
        a3 = a3 * a_sh
        sh *= 2
    carry = h_ref[...]
    rows = []
    for grp in range(ngrp):
        hg = a3[grp] * carry + b3[grp]
        rows.append(hg)
        carry = hg[SUBLANE - 1:, :]
    h = jnp.concatenate(rows, axis=0)
    h_ref[...] = carry

    gt = gate_ref[...]
    cdf = 0.5 * (1.0 + jnp.tanh(math.sqrt(2.0 / math.pi) * (gt + 0.044715 * (gt * gt * gt))))
    lru_o[...] = h * (gt * cdf)


def _pool_lru(aux, pool_w, pool_scale, conv_w, conv_b, r_w, r_b, i_w, i_b, lam):
    T = 256
    full = lambda a: pl.BlockSpec(a.shape, lambda i: (0,) * a.ndim)
    row = lambda v: v.reshape(1, -1)
    args = (pool_w.astype(BF16), row(pool_scale), conv_w, row(conv_b), r_w.astype(BF16), row(r_b),
            i_w.astype(BF16), row(i_b), row(lam))
    return pl.pallas_call(
        functools.partial(_pool_lru_kernel, T=T),
        grid=(SEQ // T,),
        in_specs=[
            pl.BlockSpec((T, POOL_WIDTH), lambda i: (i, AUX_POOL // POOL_WIDTH)),
            pl.BlockSpec((T, LRU_WIDTH), lambda i: (i, AUX_LRU_IN // LRU_WIDTH)),
            pl.BlockSpec((T, LRU_WIDTH), lambda i: (i, AUX_LRU_GATE // LRU_WIDTH)),
        ] + [full(a) for a in args],
        out_specs=[pl.BlockSpec((T, POOL_WIDTH), lambda i: (i, 0)),
                   pl.BlockSpec((T, LRU_WIDTH), lambda i: (i, 0))],
        out_shape=[jax.ShapeDtypeStruct((SEQ, POOL_WIDTH), F32),
                   jax.ShapeDtypeStruct((SEQ, LRU_WIDTH), F32)],
        scratch_shapes=[pltpu.VMEM((POOL_HALO, POOL_WIDTH), F32),
                        pltpu.VMEM((CONV_HALO, LRU_WIDTH), F32),
                        pltpu.VMEM((1, LRU_WIDTH), F32)],
        compiler_params=_cparams("arbitrary"),
        name="pool_lru",
    )(aux, aux, aux, *args)


def _mix_out_kernel(a_ref, p_ref, l_ref, x_ref, ga_ref, gp_ref, gl_ref, w_ref, gpost_ref, gate_ref,
                    gmlp_ref, sc_ref, sh_ref, wup_ref, o_ref, h_ref, wupb_ref):
    na = _rms(a_ref[...], ga_ref[...]).astype(BF16)
    npool = _rms(p_ref[...], gp_ref[...]).astype(BF16)
    nl = _rms(l_ref[...], gl_ref[...]).astype(BF16)
    p0, p1 = ATTN_WIDTH, ATTN_WIDTH + POOL_WIDTH
    y = jnp.dot(na, w_ref[0:p0, :], preferred_element_type=F32)
    y = y + jnp.dot(npool, w_ref[p0:p1, :], preferred_element_type=F32)
    y = y + jnp.dot(nl, w_ref[p1:, :], preferred_element_type=F32)
    x_new = x_ref[...] + gate_ref[...] * _rms(y, gpost_ref[...])
    o_ref[...] = x_new
    h_ref[...] = (_rms(x_new, gmlp_ref[...]) * (1.0 + sc_ref[...]) + sh_ref[...]).astype(BF16)
    wupb_ref[...] = wup_ref[...].astype(BF16)


def _mix_out(attn, pool, lru, x, g_attn, g_pool, g_lru, w_out, layer, g_post, gate, g_mlp, scale2, shift2,
             w_up):
    tm = 512
    steps = SEQ // tm
    slab = w_up.shape[1] // steps
    vec = lambda n: pl.BlockSpec((1, n), lambda i: (0, 0))
    rows = lambda n: pl.BlockSpec((tm, n), lambda i: (i, 0))
    return pl.pallas_call(
        _mix_out_kernel,
        grid=(steps,),
        in_specs=[rows(ATTN_WIDTH), rows(POOL_WIDTH), rows(LRU_WIDTH), rows(D_MODEL),
                  vec(ATTN_WIDTH), vec(POOL_WIDTH), vec(LRU_WIDTH),
                  pl.BlockSpec((None, D_MODEL, D_MODEL), lambda i: (layer, 0, 0)),
                  vec(D_MODEL), vec(D_MODEL), vec(D_MODEL), vec(D_MODEL), vec(D_MODEL),
                  pl.BlockSpec((None, slab, w_up.shape[2]), lambda i: (layer, i, 0))],
        out_specs=[rows(D_MODEL), rows(D_MODEL), pl.BlockSpec((slab, w_up.shape[2]), lambda i: (i, 0))],
        out_shape=[jax.ShapeDtypeStruct((SEQ, D_MODEL), F32), jax.ShapeDtypeStruct((SEQ, D_MODEL), BF16),
                   jax.ShapeDtypeStruct(w_up.shape[1:], BF16)],
        compiler_params=_cparams("parallel"),
        name="mix_out",
    )(attn, pool, lru, x, g_attn, g_pool, g_lru, w_out, g_post, gate, g_mlp, scale2, shift2, w_up)


def _mlp_down_kernel(u_ref, w_ref, x_ref, gpost_ref, gate_ref, o_ref):
    y = jnp.dot(u_ref[...], w_ref[...], preferred_element_type=F32)
    o_ref[...] = x_ref[...] + gate_ref[...] * _rms(y, gpost_ref[...])


def _mlp_down(u2, w_down, x, g_post, gate):
    tm = 256
    vec = pl.BlockSpec((1, D_MODEL), lambda i: (0, 0))
    return pl.pallas_call(
        _mlp_down_kernel,
        grid=(SEQ // tm,),
        in_specs=[pl.BlockSpec((tm, MLP_HIDDEN), lambda i: (i, 0)),
                  pl.BlockSpec((MLP_HIDDEN, D_MODEL), lambda i: (0, 0), pipeline_mode=pl.Buffered(1)),
                  pl.BlockSpec((tm, D_MODEL), lambda i: (i, 0)),
                  vec, vec],
        out_specs=pl.BlockSpec((tm, D_MODEL), lambda i: (i, 0)),
        out_shape=jax.ShapeDtypeStruct((SEQ, D_MODEL), F32),
        compiler_params=_cparams("parallel"),
        name="mlp_down",
    )(u2, w_down, x, g_post, gate)


def kernel(x, c, positions, rel_bias, w_mod, b_mod, g_pre_mix, g_post_mix, g_pre_mlp, g_post_mlp, w_in, pool_w, pool_scale, conv_w, conv_b, gate_r_w, gate_r_b, gate_i_w, gate_i_b, lru_lambda, g_attn_out, g_pool_out, g_lru_out, w_out, w_mlp_up, w_mlp_down):
    assert x.shape == (1, SEQ, D_MODEL) and positions.shape == (1, SEQ)
    del positions
    xs = x[0]
    mod = _modulation(c, w_mod, b_mod)
    band = _bias_band(rel_bias)
    row = lambda v: v.reshape(1, -1)
    n_attn_cols = 4 * ATTN_WIDTH
    aux_lo = n_attn_cols + IDX_DIM + N_IDX_HEADS
    w_in_t = jnp.swapaxes(w_in, 1, 2).astype(BF16)
    w_aux = jnp.concatenate(
        [w_in_t[:, aux_lo:, :], w_in_t[:, n_attn_cols:aux_lo, :],
         jnp.zeros((DEPTH, AUX_WIDTH - (AUX_KIDX + IDX_DIM + N_IDX_HEADS), D_MODEL), BF16)], axis=1)
    w_out_b = w_out.astype(BF16)
    for l in range(DEPTH):
        shift1, scale1, gate1, shift2, scale2, gate2 = [
            mod[l, :, n * D_MODEL:(n + 1) * D_MODEL] for n in range(6)]
        g_pre = row(g_pre_mix[l])
        attn_grp, aux = _in_projection(xs, g_pre, scale1, shift1, w_in_t, n_attn_cols, w_aux, l)
        words = _select(attn_grp, aux)
        attn = _attention(attn_grp, words, band)
        pool, lru = _pool_lru(aux, pool_w[l], pool_scale[l], conv_w[l], conv_b[l], gate_r_w[l],
                              gate_r_b[l], gate_i_w[l], gate_i_b[l], lru_lambda[l])
        xs, h_mlp, w_up_b = _mix_out(attn, pool, lru, xs, row(g_attn_out[l]), row(g_pool_out[l]),
                                     row(g_lru_out[l]), w_out_b, l, row(g_post_mix[l]), gate1,
                                     row(g_pre_mlp[l]), scale2, shift2, w_mlp_up)
        u2, w_down_b = _mlp_up(h_mlp, w_up_b, w_mlp_down, l)
        xs = _mlp_down(u2, w_down_b, xs, row(g_post_mlp[l]), gate2)
    return xs[None]
```

```python
import functools
import math

import jax
import jax.numpy as jnp
from jax import lax
from jax.experimental import pallas as pl
from jax.experimental.pallas import tpu as pltpu

D_MODEL = 2048
SEQ = 8192
DEPTH = 2
ATTN_HEAD_DIM = 128
ATTN_WIDTH = 1024
N_ATTN_HEADS = 8
POOL_WIDTH = 512
POOL_WINDOWS = (2, 4, 8, 16)
POOL_GROUP = 128
LRU_WIDTH = 512
N_LRU_BLOCKS = 4
LRU_BLOCK = 128
LRU_C = 8.0
CONV_WIDTH = 4
N_IDX_HEADS = 16
IDX_DIM = 64
TOP_K = 256
N_BUCKETS = 32
MAX_DISTANCE = 128
MLP_HIDDEN = 4 * D_MODEL
NORM_EPS = 1e-6

AUX_POOL = 0
AUX_LRU_IN = 512
AUX_LRU_GATE = 1024
AUX_KIDX = 1536
AUX_WIDTH = 1664
LANE = 128
SUBLANE = 8

SEL_T = 256
SEL_GROUP = 2
SEL_SCORE_GROUP = 4
ATT_TQ = 512
ATT_TK = SEL_T
ATT_UNROLL = 4
N_BAND = ATT_TQ // ATT_TK + 2
assert MAX_DISTANCE <= ATT_TK // 2 and SEQ // SEL_T <= 32 and SEL_T >= TOP_K
MASK_NEG = -1e30
LOG2E = math.log2(math.e)
VMEM_LIMIT = 56 * 1024 * 1024

F32 = jnp.float32
BF16 = jnp.bfloat16
_NT = (((1,), (1,)), ((), ()))


def _cparams(*sem):
    return pltpu.CompilerParams(dimension_semantics=sem, vmem_limit_bytes=VMEM_LIMIT)


def _rms(x, g):
    ms = jnp.mean(x * x, axis=-1, keepdims=True)
    return (x * lax.rsqrt(ms + NORM_EPS)) * g


def _mod_kernel(c_ref, w_ref, b_ref, o_ref):
    c = c_ref[...]
    c_act = c * jax.nn.sigmoid(c)
    o_ref[0] = jnp.sum(c_act * w_ref[0], axis=0, keepdims=True) + b_ref[0]


def _modulation(c, w_mod, b_mod):
    tn = 1024
    n = w_mod.shape[-1]
    return pl.pallas_call(
        _mod_kernel,
        grid=(DEPTH, n // tn),
        in_specs=[
            pl.BlockSpec((D_MODEL, 1), lambda l, j: (0, 0)),
            pl.BlockSpec((1, D_MODEL, tn), lambda l, j: (l, 0, j)),
            pl.BlockSpec((1, 1, tn), lambda l, j: (l, 0, j)),
        ],
        out_specs=pl.BlockSpec((1, 1, tn), lambda l, j: (l, 0, j)),
        out_shape=jax.ShapeDtypeStruct((DEPTH, 1, n), F32),
        compiler_params=_cparams("parallel", "parallel"),
        name="modulation",
    )(c.reshape(D_MODEL, 1), w_mod, b_mod.reshape(DEPTH, 1, n))


def _mlp_up_kernel(h_ref, w_ref, wd_ref, o_ref, wdb_ref):
    u = jnp.maximum(jnp.dot(h_ref[...], w_ref[...], preferred_element_type=F32), 0.0)
    o_ref[...] = (u * u).astype(o_ref.dtype)
    wdb_ref[...] = wd_ref[...].astype(BF16)


def _mlp_up(h, w_up_b, w_down, layer):
    tm = 512
    m, k = h.shape
    n = w_up_b.shape[1]
    halves = 2
    tn = n // halves
    steps_m = m // tm
    slab = w_down.shape[1] // (halves * steps_m)
    return pl.pallas_call(
        _mlp_up_kernel,
        grid=(halves, steps_m),
        in_specs=[
            pl.BlockSpec((tm, k), lambda j, i: (i, 0)),
            pl.BlockSpec((k, tn), lambda j, i: (0, j), pipeline_mode=pl.Buffered(1)),
            pl.BlockSpec((None, slab, w_down.shape[2]), lambda j, i: (layer, j * steps_m + i, 0)),
        ],
        out_specs=[pl.BlockSpec((tm, tn), lambda j, i: (i, j)),
                   pl.BlockSpec((slab, w_down.shape[2]), lambda j, i: (j * steps_m + i, 0))],
        out_shape=[jax.ShapeDtypeStruct((m, n), BF16), jax.ShapeDtypeStruct(w_down.shape[1:], BF16)],
        compiler_params=_cparams("parallel", "parallel"),
        name="mlp_up",
    )(h, w_up_b, w_down)


def _proj_kernel(x_ref, g_ref, sc_ref, sh_ref, wa_ref, wx_ref, oa_ref, ox_ref):
    h = (_rms(x_ref[...], g_ref[...]) * (1.0 + sc_ref[...]) + sh_ref[...]).astype(BF16)
    oa_ref[...] = lax.dot_general(h, wa_ref[...], _NT, preferred_element_type=F32).astype(oa_ref.dtype)
    ox_ref[...] = lax.dot_general(h, wx_ref[...], _NT, preferred_element_type=F32)


def _in_projection(x, g, scale, shift, w_attn_t, na, w_aux_t, layer):
    tm = 512
    m, k = x.shape
    nx = w_aux_t.shape[1]
    vec = pl.BlockSpec((1, k), lambda i: (0, 0))
    resident = lambda n: pl.BlockSpec((None, n, k), lambda i: (layer, 0, 0), pipeline_mode=pl.Buffered(1))
    return pl.pallas_call(
        _proj_kernel,
        grid=(m // tm,),
        in_specs=[pl.BlockSpec((tm, k), lambda i: (i, 0)), vec, vec, vec, resident(na), resident(nx)],
        out_specs=[pl.BlockSpec((tm, na), lambda i: (i, 0)), pl.BlockSpec((tm, nx), lambda i: (i, 0))],
        out_shape=[jax.ShapeDtypeStruct((m, na), BF16), jax.ShapeDtypeStruct((m, nx), F32)],
        compiler_params=_cparams("parallel"),
        name="in_proj",
    )(x, g, scale, shift, w_attn_t, w_aux_t)


def _band_kernel(rb_ref, o_ref):
    d = pl.program_id(0)
    t = lax.broadcasted_iota(jnp.int32, (ATT_TQ, ATT_TK), 0)
    s = lax.broadcasted_iota(jnp.int32, (ATT_TQ, ATT_TK), 1)
    rel = jnp.maximum((d - (ATT_TQ // ATT_TK - 1)) * ATT_TK + t - s, 0)
    max_exact = N_BUCKETS // 2
    relf = jnp.maximum(rel.astype(F32), 1.0)
    large = max_exact + (jnp.log(relf / max_exact) / math.log(MAX_DISTANCE / max_exact)
                         * (N_BUCKETS - max_exact)).astype(jnp.int32)
    large = jnp.minimum(large, N_BUCKETS - 1)
    bucket = jnp.where(rel < max_exact, rel, large)
    for h in range(N_ATTN_HEADS):
        table = jnp.broadcast_to(rb_ref[h:h + 1, :], (ATT_TQ, LANE))
        val = jnp.concatenate(
            [jnp.take_along_axis(table, bucket[:, cc * LANE:(cc + 1) * LANE], axis=1)
             for cc in range(ATT_TK // LANE)], axis=1)
        o_ref[0, h] = val * LOG2E


def _bias_band(rel_bias):
    return pl.pallas_call(
        _band_kernel,
        grid=(N_BAND,),
        in_specs=[pl.BlockSpec((N_ATTN_HEADS, LANE), lambda d: (0, 0))],
        out_specs=pl.BlockSpec((1, N_ATTN_HEADS, ATT_TQ, ATT_TK), lambda d: (d, 0, 0, 0)),
        out_shape=jax.ShapeDtypeStruct((N_BAND, N_ATTN_HEADS, ATT_TQ, ATT_TK), F32),
        compiler_params=_cparams("parallel"),
        name="bias_band",
    )(jnp.pad(rel_bias.T, ((0, 0), (0, LANE - N_BUCKETS))))


def _float_to_key(x):
    bits = lax.bitcast_convert_type(x, jnp.int32)
    return jnp.where(bits >= 0, bits, jnp.bitwise_xor(bits, jnp.int32(0x7FFFFFFF)))


_KEY_POS_INF = 0x7F800000
_KEY_NEG_INF = 0x807FFFFF - (1 << 32)


def _key_to_float(key):
    bits = jnp.where(key >= 0, key, jnp.bitwise_xor(key, jnp.int32(0x7FFFFFFF)))
    return lax.bitcast_convert_type(bits, F32)


def _select_kernel(qi_ref, kw_ref, o_ref, kb_ref, sc_ref, hi_ref, lo_ref):
    i = pl.program_id(0)
    tq = tk = SEL_T
    idx_scale = (N_IDX_HEADS * IDX_DIM) ** -0.5

    @pl.when(i == 0)
    def _():
        kb_ref[...] = kw_ref[:, 0:IDX_DIM].astype(BF16)

    w_t = jnp.transpose(kw_ref[pl.ds(pl.multiple_of(i * tq, tq), tq), :])

    nchunks = i + 1
    key_pos = lax.broadcasted_iota(jnp.int32, (tk, tq), 0)
    qry_pos = i * tq + lax.broadcasted_iota(jnp.int32, (tk, tq), 1)
    grp = SEL_GROUP

    def high_half(key):
        return lax.shift_right_arithmetic(key, jnp.int32(16))

    def score_tiles(j0, ntiles, below_diagonal, gmax):
        kc = kb_ref[pl.ds(pl.multiple_of(j0 * tk, tk), ntiles * tk), :]
        acc = jnp.zeros((ntiles * tk, tq), F32)
        for h in range(N_IDX_HEADS):
            d = lax.dot_general(kc, qi_ref[:, h * IDX_DIM:(h + 1) * IDX_DIM], _NT,
                                preferred_element_type=F32)
            acc = acc + w_t[IDX_DIM + h:IDX_DIM + h + 1, :] * jnp.maximum(d, 0.0)
        for u in range(ntiles):
            j = j0 + u
            score = acc[u * tk:(u + 1) * tk] * idx_scale
            if not below_diagonal:
                score = jnp.where(key_pos + j * tk <= qry_pos, score, -jnp.inf)
            sc_ref[j] = score
            gmax = jnp.maximum(gmax, score)
            key = _float_to_key(score)
            hi_ref[j] = high_half(key).astype(jnp.int16)
            lo_ref[j] = (jnp.bitwise_and(key, jnp.int32(0xFFFF)) - 32768).astype(jnp.int16)
        return gmax

    ngroups = (nchunks + grp - 1) // grp
    big = SEL_SCORE_GROUP
    nbig = i // big
    gmax = lax.fori_loop(0, nbig, lambda jg, gm: score_tiles(big * jg, big, True, gm),
                         jnp.full((tk, tq), -jnp.inf, F32))
    gmax = lax.fori_loop(nbig * (big // grp), ngroups,
                         lambda jg, gm: score_tiles(grp * jg, grp, False, gm), gmax)
    t_min = high_half(_float_to_key(jnp.min(gmax, axis=0, keepdims=True)))
    t_max = high_half(_float_to_key(jnp.max(gmax, axis=0, keepdims=True)))

    i16 = jnp.int16
    pack = 16
    one16 = jnp.ones((pack, tq), i16)
    zero16 = jnp.zeros((pack, tq), i16)

    def count_ge(ref, cand):
        cand_b = jnp.broadcast_to(cand, (pack, tq)).astype(i16)

        def body(jg, acc):
            parts = [acc, zero16, zero16, zero16]
            n = 0
            for u in range(grp):
                x = ref[grp * jg + u]
                for g in range(tk // pack):
                    hit = jnp.where(x[g * pack:(g + 1) * pack, :] >= cand_b, one16, zero16)
                    parts[n % 4] = parts[n % 4] + hit
                    n += 1
            return (parts[0] + parts[1]) + (parts[2] + parts[3])

        acc = lax.fori_loop(0, ngroups, body, zero16)
        return jnp.sum(acc.astype(jnp.int32).astype(F32), axis=0, keepdims=True)

    def kth_largest(ref, rank, lo0, hi0, stop_on_hit):
        def unsettled(carry):
            lo, hi, _ = carry
            return jnp.max((hi - lo).astype(F32)) > 0.0

        def halve(carry):
            lo, hi, above = carry
            mid = lo + lax.shift_right_logical(hi - lo + 1, jnp.int32(1))
            cnt = count_ge(ref, mid)
            ok = cnt >= rank
            lo, hi, above = jnp.where(ok, mid, lo), jnp.where(ok, hi, mid - 1), jnp.where(ok, above, cnt)
            if stop_on_hit:
                hit = cnt == rank
                lo, hi = jnp.where(hit, mid, lo), jnp.where(hit, mid, hi)
            return lo, hi, above

        def body(carry):
            return halve(halve(carry))

        lo, _, above = lax.while_loop(unsettled, body, (lo0, hi0, jnp.zeros((1, tq), F32)))
        return lo, above

    t_hi, above = kth_largest(hi_ref, jnp.full((1, tq), float(TOP_K), F32), t_min, t_max, False)
    t_hi_b = jnp.broadcast_to(t_hi, (tk, tq)).astype(i16)

    def narrow_body(j, carry):
        lo_ref[j] = jnp.where(hi_ref[j] == t_hi_b, lo_ref[j], jnp.full((tk, tq), -32768, i16))
        return carry

    lax.fori_loop(0, grp * ngroups, narrow_body, 0)
    t_lo, _ = kth_largest(lo_ref, float(TOP_K) - above, jnp.full((1, tq), -32768, jnp.int32),
                          jnp.full((1, tq), 32767, jnp.int32), True)
    key = jnp.bitwise_or(jnp.left_shift(t_hi, 16), t_lo + 32768)
    key = jnp.clip(key, _KEY_NEG_INF, _KEY_POS_INF)
    thr = jnp.broadcast_to(_key_to_float(key), (tk, tq))

    def word_body(j, word):
        return jnp.bitwise_or(word, jnp.where(sc_ref[j] >= thr, jnp.left_shift(jnp.int32(1), j), 0))

    word_t = lax.fori_loop(0, i, word_body, jnp.zeros((tk, tq), jnp.int32))
    keep = jnp.logical_and(sc_ref[i] >= thr, key_pos + i * tk <= qry_pos)
    word_t = jnp.bitwise_or(word_t, jnp.where(keep, jnp.left_shift(jnp.int32(1), i), 0))
    o_ref[...] = jnp.transpose(word_t)


def _select(attn_grp, aux):
    tq = tk = SEL_T
    nq, nk = SEQ // tq, SEQ // tk
    return pl.pallas_call(
        _select_kernel,
        grid=(nq,),
        in_specs=[
            pl.BlockSpec((tq, N_IDX_HEADS * IDX_DIM), lambda i: (i, 3)),
            pl.BlockSpec((SEQ, LANE), lambda i: (0, AUX_KIDX // LANE)),
        ],
        out_specs=pl.BlockSpec((tq, tk), lambda i: (i, 0)),
        out_shape=jax.ShapeDtypeStruct((SEQ, tk), jnp.int32),
        scratch_shapes=[
            pltpu.VMEM((SEQ, IDX_DIM), BF16),
            pltpu.VMEM((nk, tk, tq), F32),
            pltpu.VMEM((nk, tk, tq), jnp.int16),
            pltpu.VMEM((nk, tk, tq), jnp.int16),
        ],
        compiler_params=_cparams("arbitrary"),
        name="select",
    )(attn_grp, aux)


def _attn_kernel(q_ref, k_ref, v_ref, w_ref, band_ref, o_ref, mask_ref, sa_ref, sb_ref, acc_ref):
    i = pl.program_id(0)
    tq, tk, unroll = ATT_TQ, ATT_TK, ATT_UNROLL
    dh = ATTN_HEAD_DIM
    c = (ATTN_HEAD_DIM ** -0.5) * LOG2E
    r = tq // tk
    ngrp = (r * (i + 1) + unroll - 1) // unroll
    ones = jnp.ones((unroll * tk, LANE), BF16)

    @pl.when(pl.program_id(1) == 0)
    def _():
        def expand(j, carry):
            keep = jnp.bitwise_and(w_ref[...], jnp.left_shift(jnp.int32(1), j)) != 0
            mask_ref[j] = jnp.where(keep, 0.0, MASK_NEG)
            return carry

        lax.fori_loop(0, unroll * ngrp, expand, 0)

    def logits(hh, g, s_ref, m_prev):
        hs = slice(hh * dh, (hh + 1) * dh)
        mx = jnp.full((tq, LANE), MASK_NEG, F32)
        for u in range(unroll):
            j = unroll * g + u
            kt = k_ref[pl.ds(pl.multiple_of(j * tk, tk), tk), hs]
            s2 = lax.dot_general(q_ref[:, hs], kt, _NT, preferred_element_type=F32) * c
            s2 = s2 + band_ref[jnp.clip(r * i + (r - 1) - j, 0, N_BAND - 1), hh] + mask_ref[j]
            s_ref[u] = s2
            for cc in range(tk // LANE):
                mx = jnp.maximum(mx, s2[:, cc * LANE:(cc + 1) * LANE])
        return jnp.maximum(m_prev, jnp.max(mx, axis=1, keepdims=True))

    def values(hh, g, s_ref, m_prev, m_cur):
        m_b = jnp.broadcast_to(m_cur, (tq, tk))
        p = jnp.concatenate([jnp.exp2(s_ref[u] - m_b).astype(BF16) for u in range(unroll)], axis=1)
        vt = v_ref[pl.ds(pl.multiple_of(g * (unroll * tk), unroll * tk), unroll * tk), hh * dh:(hh + 1) * dh]
        pv = jnp.dot(p, jnp.concatenate([vt, ones], axis=1), preferred_element_type=F32)
        acc_ref[hh] = acc_ref[hh] * jnp.exp2(m_prev - m_cur) + pv

    acc_ref[...] = jnp.zeros(acc_ref.shape, F32)
    m_none = jnp.full((tq, 1), MASK_NEG, F32)
    npair = (ngrp - 1) // 2
    last = 2 * npair

    def steady(hh, first, second, m_0):
        def pair(t, carry):
            m_prev, m_cur = carry
            m_1 = logits(hh, 2 * t + 1, second, m_cur)
            values(hh, 2 * t, first, m_prev, m_cur)
            m_2 = logits(hh, 2 * t + 2, first, m_1)
            values(hh, 2 * t + 1, second, m_cur, m_1)
            return m_1, m_2

        return lax.fori_loop(0, npair, pair, (m_none, m_0))

    @pl.when(ngrp - 1 == last)
    def _():
        m_prev, m_cur = steady(0, sa_ref, sb_ref, logits(0, 0, sa_ref, m_none))
        m_0 = logits(1, 0, sb_ref, m_none)
        values(0, last, sa_ref, m_prev, m_cur)
        m_prev, m_cur = steady(1, sb_ref, sa_ref, m_0)
        values(1, last, sb_ref, m_prev, m_cur)

    @pl.when(ngrp - 1 > last)
    def _():
        m_prev, m_cur = steady(0, sa_ref, sb_ref, logits(0, 0, sa_ref, m_none))
        m_1 = logits(0, last + 1, sb_ref, m_cur)
        values(0, last, sa_ref, m_prev, m_cur)
        m_0 = logits(1, 0, sa_ref, m_none)
        values(0, last + 1, sb_ref, m_cur, m_1)
        m_prev, m_cur = steady(1, sa_ref, sb_ref, m_0)
        m_1 = logits(1, last + 1, sb_ref, m_cur)
        values(1, last, sa_ref, m_prev, m_cur)
        values(1, last + 1, sb_ref, m_cur, m_1)

    for hh in range(2):
        o_ref[:, hh * dh:(hh + 1) * dh] = acc_ref[hh, :, :dh] / acc_ref[hh, :, dh:dh + 1]


def _attention(attn_grp, words, band):
    tq, tk = ATT_TQ, ATT_TK
    nq, nk = SEQ // tq, SEQ // tk
    dh2 = 2 * ATTN_HEAD_DIM
    npairs = N_ATTN_HEADS // 2
    return pl.pallas_call(
        _attn_kernel,
        grid=(nq, npairs),
        in_specs=[
            pl.BlockSpec((tq, dh2), lambda i, h: (i, h)),
            pl.BlockSpec((SEQ, dh2), lambda i, h: (0, npairs + h)),
            pl.BlockSpec((SEQ, dh2), lambda i, h: (0, 2 * npairs + h)),
            pl.BlockSpec((tq, tk), lambda i, h: (i, 0)),
            pl.BlockSpec((N_BAND, 2, tq, tk), lambda i, h: (0, h, 0, 0)),
        ],
        out_specs=pl.BlockSpec((tq, dh2), lambda i, h: (i, h)),
        out_shape=jax.ShapeDtypeStruct((SEQ, ATTN_WIDTH), F32),
        scratch_shapes=[pltpu.VMEM((nk, tq, tk), F32),
                        pltpu.VMEM((ATT_UNROLL, tq, tk), F32), pltpu.VMEM((ATT_UNROLL, tq, tk), F32),
                        pltpu.VMEM((2, tq, ATTN_HEAD_DIM + LANE), F32)],
        compiler_params=_cparams("parallel", "arbitrary"),
        name="attention",
    )(attn_grp, attn_grp, attn_grp, words, band)


POOL_HALO = 16
CONV_HALO = 8


def _block_diag_dot(x, w_ref):
    outs = []
    for g in range(w_ref.shape[0]):
        xs = x[:, g * LRU_BLOCK:(g + 1) * LRU_BLOCK].astype(BF16)
        outs.append(jnp.dot(xs, w_ref[g], preferred_element_type=F32))
    return jnp.concatenate(outs, axis=1)


def _pool_lru_kernel(pin_ref, lin_ref, gate_ref, pw_ref, ps_ref, cw_ref, cb_ref, rw_ref, rb_ref,
                     iw_ref, ib_ref, lam_ref, pool_o, lru_o, pprev_ref, lprev_ref, h_ref, *, T):
    i = pl.program_id(0)

    @pl.when(i == 0)
    def _():
        pprev_ref[...] = jnp.zeros(pprev_ref.shape, F32)
        lprev_ref[...] = jnp.zeros(lprev_ref.shape, F32)
        h_ref[...] = jnp.zeros(h_ref.shape, F32)

    t_glob = i * T + lax.broadcasted_iota(jnp.int32, (T, 1), 0)

    u = pin_ref[...]
    ext = jnp.concatenate([pprev_ref[...], u], axis=0)
    pprev_ref[...] = u[T - POOL_HALO:, :]
    p2 = ext[1:] + ext[:-1]
    p4 = p2[2:, POOL_GROUP:] + p2[:-2, POOL_GROUP:]
    p8 = p4[4:, POOL_GROUP:] + p4[:-4, POOL_GROUP:]
    p16 = p8[8:, POOL_GROUP:] + p8[:-8, POOL_GROUP:]
    wsums = (p2[15:15 + T, :POOL_GROUP], p4[13:13 + T, :POOL_GROUP],
             p8[9:9 + T, :POOL_GROUP], p16[1:1 + T, :])
    for g, win in enumerate(POOL_WINDOWS):
        gs = slice(g * POOL_GROUP, (g + 1) * POOL_GROUP)
        count = jnp.minimum(t_glob + 1, win).astype(F32)
        dlt = wsums[g] / count - u[:, gs]
        y = jnp.dot(dlt.astype(BF16), pw_ref[g], preferred_element_type=F32)
        pool_o[:, gs] = y * ps_ref[:, gs]

    x = lin_ref[...]
    lext = jnp.concatenate([lprev_ref[...], x], axis=0)
    lprev_ref[...] = x[T - CONV_HALO:, :]
    xc = cb_ref[...]
    for jj in range(CONV_WIDTH):
        off = CONV_HALO - (CONV_WIDTH - 1) + jj
        xc = xc + cw_ref[jj:jj + 1, :] * lext[off:off + T, :]
    r = jax.nn.sigmoid(_block_diag_dot(xc, rw_ref) + rb_ref[...])
    ig = jax.nn.sigmoid(_block_diag_dot(xc, iw_ref) + ib_ref[...])
    z = -lam_ref[...]
    softplus = jnp.maximum(z, 0.0) + jnp.log1p(jnp.exp(-jnp.abs(z)))
    log_a = (-LRU_C * r) * softplus
    a = jnp.exp(log_a)
    b = jnp.sqrt(-jnp.tanh(log_a) * (a * a + 1.0)) * (ig * xc)

    ngrp = T // SUBLANE
    a3 = a.reshape(ngrp, SUBLANE, LRU_WIDTH)
    b3 = b.reshape(ngrp, SUBLANE, LRU_WIDTH)
    sub = lax.broadcasted_iota(jnp.int32, a3.shape, 1)
    sh = 1
    while sh < SUBLANE:
        a_sh = jnp.where(sub >= sh, pltpu.roll(a3, sh, axis=1), 1.0)
        b_sh = jnp.where(sub >= sh, pltpu.roll(b3, sh, axis=1), 0.0)
        b3 = a3 * b_sh + b3
        a3 = a3 * a_sh
        sh *= 2
    carry = h_ref[...]
    rows = []
    for grp in range(ngrp):
        hg = a3[grp] * carry + b3[grp]
        rows.append(hg)
        carry = hg[SUBLANE - 1:, :]
    h = jnp.concatenate(rows, axis=0)
    h_ref[...] = carry

    gt = gate_ref[...]
    cdf = 0.5 * (1.0 + jnp.tanh(math.sqrt(2.0 / math.pi) * (gt + 0.044715 * (gt * gt * gt))))
    lru_o[...] = h * (gt * cdf)


def _pool_lru(aux, pool_w, pool_scale, conv_w, conv_b, r_w, r_b, i_w, i_b, lam):
    T = 256
    full = lambda a: pl.BlockSpec(a.shape, lambda i: (0,) * a.ndim)
    row = lambda v: v.reshape(1, -1)
    args = (pool_w.astype(BF16), row(pool_scale), conv_w, row(conv_b), r_w.astype(BF16), row(r_b),
            i_w.astype(BF16), row(i_b), row(lam))
    return pl.pallas_call(
        functools.partial(_pool_lru_kernel, T=T),
        grid=(SEQ // T,),
        in_specs=[
            pl.BlockSpec((T, POOL_WIDTH), lambda i: (i, AUX_POOL // POOL_WIDTH)),
            pl.BlockSpec((T, LRU_WIDTH), lambda i: (i, AUX_LRU_IN // LRU_WIDTH)),
            pl.BlockSpec((T, LRU_WIDTH), lambda i: (i, AUX_LRU_GATE // LRU_WIDTH)),
        ] + [full(a) for a in args],
        out_specs=[pl.BlockSpec((T, POOL_WIDTH), lambda i: (i, 0)),
                   pl.BlockSpec((T, LRU_WIDTH), lambda i: (i, 0))],
        out_shape=[jax.ShapeDtypeStruct((SEQ, POOL_WIDTH), F32),
                   jax.ShapeDtypeStruct((SEQ, LRU_WIDTH), F32)],
        scratch_shapes=[pltpu.VMEM((POOL_HALO, POOL_WIDTH), F32),
                        pltpu.VMEM((CONV_HALO, LRU_WIDTH), F32),
                        pltpu.VMEM((1, LRU_WIDTH), F32)],
        compiler_params=_cparams("arbitrary"),
        name="pool_lru",
    )(aux, aux, aux, *args)


def _mix_out_kernel(a_ref, p_ref, l_ref, x_ref, ga_ref, gp_ref, gl_ref, w_ref, gpost_ref, gate_ref,
                    gmlp_ref, sc_ref, sh_ref, wup_ref, o_ref, h_ref, wupb_ref):
    na = _rms(a_ref[...], ga_ref[...]).astype(BF16)
    npool = _rms(p_ref[...], gp_ref[...]).astype(BF16)
    nl = _rms(l_ref[...], gl_ref[...]).astype(BF16)
    p0, p1 = ATTN_WIDTH, ATTN_WIDTH + POOL_WIDTH
    y = jnp.dot(na, w_ref[0:p0, :], preferred_element_type=F32)
    y = y + jnp.dot(npool, w_ref[p0:p1, :], preferred_element_type=F32)
    y = y + jnp.dot(nl, w_ref[p1:, :], preferred_element_type=F32)
    x_new = x_ref[...] + gate_ref[...] * _rms(y, gpost_ref[...])
    o_ref[...] = x_new
    h_ref[...] = (_rms(x_new, gmlp_ref[...]) * (1.0 + sc_ref[...]) + sh_ref[...]).astype(BF16)
    wupb_ref[...] = wup_ref[...].astype(BF16)


def _mix_out(attn, pool, lru, x, g_attn, g_pool, g_lru, w_out, layer, g_post, gate, g_mlp, scale2, shift2,
             w_up):
    tm = 512
    steps = SEQ // tm
    slab = w_up.shape[1] // steps
    vec = lambda n: pl.BlockSpec((1, n), lambda i: (0, 0))
    rows = lambda n: pl.BlockSpec((tm, n), lambda i: (i, 0))
    return pl.pallas_call(
        _mix_out_kernel,
        grid=(steps,),
        in_specs=[rows(ATTN_WIDTH), rows(POOL_WIDTH), rows(LRU_WIDTH), rows(D_MODEL),
                  vec(ATTN_WIDTH), vec(POOL_WIDTH), vec(LRU_WIDTH),
                  pl.BlockSpec((None, D_MODEL, D_MODEL), lambda i: (layer, 0, 0)),
                  vec(D_MODEL), vec(D_MODEL), vec(D_MODEL), vec(D_MODEL), vec(D_MODEL),
                  pl.BlockSpec((None, slab, w_up.shape[2]), lambda i: (layer, i, 0))],
        out_specs=[rows(D_MODEL), rows(D_MODEL), pl.BlockSpec((slab, w_up.shape[2]), lambda i: (i, 0))],
        out_shape=[jax.ShapeDtypeStruct((SEQ, D_MODEL), F32), jax.ShapeDtypeStruct((SEQ, D_MODEL), BF16),
                   jax.ShapeDtypeStruct(w_up.shape[1:], BF16)],
        compiler_params=_cparams("parallel"),
        name="mix_out",
    )(attn, pool, lru, x, g_attn, g_pool, g_lru, w_out, g_post, gate, g_mlp, scale2, shift2, w_up)


def _mlp_down_kernel(u_ref, w_ref, x_ref, gpost_ref, gate_ref, o_ref):
    y = jnp.dot(u_ref[...], w_ref[...], preferred_element_type=F32)
    o_ref[...] = x_ref[...] + gate_ref[...] * _rms(y, gpost_ref[...])


def _mlp_down(u2, w_down, x, g_post, gate):
    tm = 256
    vec = pl.BlockSpec((1, D_MODEL), lambda i: (0, 0))
    return pl.pallas_call(
        _mlp_down_kernel,
        grid=(SEQ // tm,),
        in_specs=[pl.BlockSpec((tm, MLP_HIDDEN), lambda i: (i, 0)),
                  pl.BlockSpec((MLP_HIDDEN, D_MODEL), lambda i: (0, 0), pipeline_mode=pl.Buffered(1)),
                  pl.BlockSpec((tm, D_MODEL), lambda i: (i, 0)),
                  vec, vec],
        out_specs=pl.BlockSpec((tm, D_MODEL), lambda i: (i, 0)),
        out_shape=jax.ShapeDtypeStruct((SEQ, D_MODEL), F32),
        compiler_params=_cparams("parallel"),
        name="mlp_down",
    )(u2, w_down, x, g_post, gate)


def kernel(x, c, positions, rel_bias, w_mod, b_mod, g_pre_mix, g_post_mix, g_pre_mlp, g_post_mlp, w_in, pool_w, pool_scale, conv_w, conv_b, gate_r_w, gate_r_b, gate_i_w, gate_i_b, lru_lambda, g_attn_out, g_pool_out, g_lru_out, w_out, w_mlp_up, w_mlp_down):
    assert x.shape == (1, SEQ, D_MODEL) and positions.shape == (1, SEQ)
    del positions
    xs = x[0]
    mod = _modulation(c, w_mod, b_mod)
    band = _bias_band(rel_bias)
    row = lambda v: v.reshape(1, -1)
    n_attn_cols = 4 * ATTN_WIDTH
    aux_lo = n_attn_cols + IDX_DIM + N_IDX_HEADS
    w_in_t = jnp.swapaxes(w_in, 1, 2).astype(BF16)
    w_aux = jnp.concatenate(
        [w_in_t[:, aux_lo:, :], w_in_t[:, n_attn_cols:aux_lo, :],
         jnp.zeros((DEPTH, AUX_WIDTH - (AUX_KIDX + IDX_DIM + N_IDX_HEADS), D_MODEL), BF16)], axis=1)
    w_out_b = w_out.astype(BF16)
    for l in range(DEPTH):
        shift1, scale1, gate1, shift2, scale2, gate2 = [
            mod[l, :, n * D_MODEL:(n + 1) * D_MODEL] for n in range(6)]
        g_pre = row(g_pre_mix[l])
        attn_grp, aux = _in_projection(xs, g_pre, scale1, shift1, w_in_t, n_attn_cols, w_aux, l)
        words = _select(attn_grp, aux)
        attn = _attention(attn_grp, words, band)
        pool, lru = _pool_lru(aux, pool_w[l], pool_scale[l], conv_w[l], conv_b[l], gate_r_w[l],
                              gate_r_b[l], gate_i_w[l], gate_i_b[l], lru_lambda[l])
        xs, h_mlp, w_up_b = _mix_out(attn, pool, lru, xs, row(g_attn_out[l]), row(g_pool_out[l]),
                                     row(g_lru_out[l]), w_out_b, l, row(g_post_mix[l]), gate1,
                                     row(g_pre_mlp[l]), scale2, shift2, w_mlp_up)
        u2, w_down_b = _mlp_up(h_mlp, w_up_b, w_mlp_down, l)
        xs = _mlp_down(u2, w_down_b, xs, row(g_post_mlp[l]), gate2)
    return xs[None]
```

```python
import functools
import math

import jax
import jax.numpy as jnp
from jax import lax
from jax.experimental import pallas as pl
from jax.experimental.pallas import tpu as pltpu

D_MODEL = 2048
SEQ = 8192
DEPTH = 2
ATTN_HEAD_DIM = 128
ATTN_WIDTH = 1024
N_ATTN_HEADS = 8
POOL_WIDTH = 512
POOL_WINDOWS = (2, 4, 8, 16)
POOL_GROUP = 128
LRU_WIDTH = 512
N_LRU_BLOCKS = 4
LRU_BLOCK = 128
LRU_C = 8.0
CONV_WIDTH = 4
N_IDX_HEADS = 16
IDX_DIM = 64
TOP_K = 256
N_BUCKETS = 32
MAX_DISTANCE = 128
MLP_HIDDEN = 4 * D_MODEL
NORM_EPS = 1e-6

AUX_POOL = 0
AUX_LRU_IN = 512
AUX_LRU_GATE = 1024
AUX_KIDX = 1536
AUX_WIDTH = 1664
LANE = 128
SUBLANE = 8

SEL_T = 256
SEL_GROUP = 2
SEL_SCORE_GROUP = 4
SEL_SUB = 128
ATT_TQ = 512
ATT_TK = SEL_T
ATT_UNROLL = 4
N_BAND = ATT_TQ // ATT_TK + 2
assert MAX_DISTANCE <= ATT_TK // 2 and SEQ // SEL_T <= 32 and SEL_T >= TOP_K
MASK_NEG = -1e30
LOG2E = math.log2(math.e)
VMEM_LIMIT = 56 * 1024 * 1024

F32 = jnp.float32
BF16 = jnp.bfloat16
_NT = (((1,), (1,)), ((), ()))


def _cparams(*sem):
    return pltpu.CompilerParams(dimension_semantics=sem, vmem_limit_bytes=VMEM_LIMIT)


def _rms(x, g):
    ms = jnp.mean(x * x, axis=-1, keepdims=True)
    return (x * lax.rsqrt(ms + NORM_EPS)) * g


def _mod_kernel(c_ref, w_ref, b_ref, o_ref):
    c = c_ref[...]
    c_act = c * jax.nn.sigmoid(c)
    o_ref[0] = jnp.sum(c_act * w_ref[0], axis=0, keepdims=True) + b_ref[0]


def _modulation(c, w_mod, b_mod):
    tn = 1024
    n = w_mod.shape[-1]
    return pl.pallas_call(
        _mod_kernel,
        grid=(DEPTH, n // tn),
        in_specs=[
            pl.BlockSpec((D_MODEL, 1), lambda l, j: (0, 0)),
            pl.BlockSpec((1, D_MODEL, tn), lambda l, j: (l, 0, j)),
            pl.BlockSpec((1, 1, tn), lambda l, j: (l, 0, j)),
        ],
        out_specs=pl.BlockSpec((1, 1, tn), lambda l, j: (l, 0, j)),
        out_shape=jax.ShapeDtypeStruct((DEPTH, 1, n), F32),
        compiler_params=_cparams("parallel", "parallel"),
        name="modulation",
    )(c.reshape(D_MODEL, 1), w_mod, b_mod.reshape(DEPTH, 1, n))


def _mlp_up_kernel(h_ref, w_ref, wd_ref, o_ref, wdb_ref):
    u = jnp.maximum(jnp.dot(h_ref[...], w_ref[...], preferred_element_type=F32), 0.0)
    o_ref[...] = (u * u).astype(o_ref.dtype)
    wdb_ref[...] = wd_ref[...].astype(BF16)


def _mlp_up(h, w_up_b, w_down, layer):
    tm = 512
    m, k = h.shape
    n = w_up_b.shape[1]
    halves = 2
    tn = n // halves
    steps_m = m // tm
    slab = w_down.shape[1] // (halves * steps_m)
    return pl.pallas_call(
        _mlp_up_kernel,
        grid=(halves, steps_m),
        in_specs=[
            pl.BlockSpec((tm, k), lambda j, i: (i, 0)),
            pl.BlockSpec((k, tn), lambda j, i: (0, j), pipeline_mode=pl.Buffered(1)),
            pl.BlockSpec((None, slab, w_down.shape[2]), lambda j, i: (layer, j * steps_m + i, 0)),
        ],
        out_specs=[pl.BlockSpec((tm, tn), lambda j, i: (i, j)),
                   pl.BlockSpec((slab, w_down.shape[2]), lambda j, i: (j * steps_m + i, 0))],
        out_shape=[jax.ShapeDtypeStruct((m, n), BF16), jax.ShapeDtypeStruct(w_down.shape[1:], BF16)],
        compiler_params=_cparams("parallel", "parallel"),
        name="mlp_up",
    )(h, w_up_b, w_down)


def _proj_kernel(x_ref, g_ref, sc_ref, sh_ref, wa_ref, wx_ref, oa_ref, ox_ref):
    h = (_rms(x_ref[...], g_ref[...]) * (1.0 + sc_ref[...]) + sh_ref[...]).astype(BF16)
    oa_ref[...] = lax.dot_general(h, wa_ref[...], _NT, preferred_element_type=F32).astype(oa_ref.dtype)
    ox_ref[...] = lax.dot_general(h, wx_ref[...], _NT, preferred_element_type=F32)


def _in_projection(x, g, scale, shift, w_attn_t, na, w_aux_t, layer):
    tm = 512
    m, k = x.shape
    nx = w_aux_t.shape[1]
    vec = pl.BlockSpec((1, k), lambda i: (0, 0))
    resident = lambda n: pl.BlockSpec((None, n, k), lambda i: (layer, 0, 0), pipeline_mode=pl.Buffered(1))
    return pl.pallas_call(
        _proj_kernel,
        grid=(m // tm,),
        in_specs=[pl.BlockSpec((tm, k), lambda i: (i, 0)), vec, vec, vec, resident(na), resident(nx)],
        out_specs=[pl.BlockSpec((tm, na), lambda i: (i, 0)), pl.BlockSpec((tm, nx), lambda i: (i, 0))],
        out_shape=[jax.ShapeDtypeStruct((m, na), BF16), jax.ShapeDtypeStruct((m, nx), F32)],
        compiler_params=_cparams("parallel"),
        name="in_proj",
    )(x, g, scale, shift, w_attn_t, w_aux_t)


def _band_kernel(rb_ref, o_ref):
    d = pl.program_id(0)
    t = lax.broadcasted_iota(jnp.int32, (ATT_TQ, ATT_TK), 0)
    s = lax.broadcasted_iota(jnp.int32, (ATT_TQ, ATT_TK), 1)
    rel = jnp.maximum((d - (ATT_TQ // ATT_TK - 1)) * ATT_TK + t - s, 0)
    max_exact = N_BUCKETS // 2
    relf = jnp.maximum(rel.astype(F32), 1.0)
    large = max_exact + (jnp.log(relf / max_exact) / math.log(MAX_DISTANCE / max_exact)
                         * (N_BUCKETS - max_exact)).astype(jnp.int32)
    large = jnp.minimum(large, N_BUCKETS - 1)
    bucket = jnp.where(rel < max_exact, rel, large)
    for h in range(N_ATTN_HEADS):
        table = jnp.broadcast_to(rb_ref[h:h + 1, :], (ATT_TQ, LANE))
        val = jnp.concatenate(
            [jnp.take_along_axis(table, bucket[:, cc * LANE:(cc + 1) * LANE], axis=1)
             for cc in range(ATT_TK // LANE)], axis=1)
        o_ref[0, h] = val * LOG2E


def _bias_band(rel_bias):
    return pl.pallas_call(
        _band_kernel,
        grid=(N_BAND,),
        in_specs=[pl.BlockSpec((N_ATTN_HEADS, LANE), lambda d: (0, 0))],
        out_specs=pl.BlockSpec((1, N_ATTN_HEADS, ATT_TQ, ATT_TK), lambda d: (d, 0, 0, 0)),
        out_shape=jax.ShapeDtypeStruct((N_BAND, N_ATTN_HEADS, ATT_TQ, ATT_TK), F32),
        compiler_params=_cparams("parallel"),
        name="bias_band",
    )(jnp.pad(rel_bias.T, ((0, 0), (0, LANE - N_BUCKETS))))


def _float_to_key(x):
    bits = lax.bitcast_convert_type(x, jnp.int32)
    return jnp.where(bits >= 0, bits, jnp.bitwise_xor(bits, jnp.int32(0x7FFFFFFF)))


_KEY_POS_INF = 0x7F800000
_KEY_NEG_INF = 0x807FFFFF - (1 << 32)


def _key_to_float(key):
    bits = jnp.where(key >= 0, key, jnp.bitwise_xor(key, jnp.int32(0x7FFFFFFF)))
    return lax.bitcast_convert_type(bits, F32)


def _select_kernel(qi_ref, kw_ref, o_ref, kb_ref, sc_ref, hi_ref, lo_ref):
    i = pl.program_id(0)
    tq = tk = SEL_T
    idx_scale = (N_IDX_HEADS * IDX_DIM) ** -0.5
    assert math.frexp(idx_scale)[0] == 0.5

    @pl.when(i == 0)
    def _():
        kb_ref[...] = kw_ref[:, 0:IDX_DIM].astype(BF16)

    w_t = jnp.transpose(kw_ref[pl.ds(pl.multiple_of(i * tq, tq), tq), :]) * idx_scale

    nchunks = i + 1
    key_pos = lax.broadcasted_iota(jnp.int32, (tk, tq), 0)
    qry_pos = i * tq + lax.broadcasted_iota(jnp.int32, (tk, tq), 1)
    grp = SEL_GROUP

    def high_half(key):
        return lax.shift_right_arithmetic(key, jnp.int32(16))

    def score_tiles(j0, ntiles, below_diagonal, gmax):
        sub = SEL_SUB
        sub_pos = lax.broadcasted_iota(jnp.int32, (sub, tq), 0)
        sub_qry = i * tq + lax.broadcasted_iota(jnp.int32, (sub, tq), 1)
        best =[gmax[s * sub:(s + 1) * sub] for s in range(tk // sub)]
        for u in range(ntiles):
            j = j0 + u
            for s in range(tk // sub):
                rows = slice(s * sub, (s + 1) * sub)
                kc = kb_ref[pl.ds(pl.multiple_of(j * tk + s * sub, sub), sub), :]
                acc = jnp.zeros((sub, tq), F32)
                for h in range(N_IDX_HEADS):
                    d = lax.dot_general(kc, qi_ref[:, h * IDX_DIM:(h + 1) * IDX_DIM], _NT,
                                        preferred_element_type=F32)
                    acc = acc + w_t[IDX_DIM + h:IDX_DIM + h + 1, :] * jnp.maximum(d, 0.0)
                score = acc
                if not below_diagonal:
                    score = jnp.where(sub_pos + (j * tk + s * sub) <= sub_qry, score, -jnp.inf)
                sc_ref[j, rows, :] = score
                best[s] = jnp.maximum(best[s], score)
                key = _float_to_key(score)
                hi_ref[j, rows, :] = high_half(key).astype(jnp.int16)
                lo_ref[j, rows, :] = (jnp.bitwise_and(key, jnp.int32(0xFFFF)) - 32768).astype(jnp.int16)
        return jnp.concatenate(best, axis=0)

    ngroups = (nchunks + grp - 1) // grp
    big = SEL_SCORE_GROUP
    nbig = i // big
    gmax = lax.fori_loop(0, nbig, lambda jg, gm: score_tiles(big * jg, big, True, gm),
                         jnp.full((tk, tq), -jnp.inf, F32))
    gmax = lax.fori_loop(nbig * (big // grp), ngroups,
                         lambda jg, gm: score_tiles(grp * jg, grp, False, gm), gmax)
    t_min = high_half(_float_to_key(jnp.min(gmax, axis=0, keepdims=True)))
    t_max = high_half(_float_to_key(jnp.max(gmax, axis=0, keepdims=True)))

    i16 = jnp.int16
    pack = 16
    one16 = jnp.ones((pack, tq), i16)
    zero16 = jnp.zeros((pack, tq), i16)

    def count_ge(ref, cand):
        cand_b = jnp.broadcast_to(cand, (pack, tq)).astype(i16)

        def body(jg, acc):
            parts = [acc, zero16, zero16, zero16]
            n = 0
            for u in range(grp):
                x = ref[grp * jg + u]
                for g in range(tk // pack):
                    hit = jnp.where(x[g * pack:(g + 1) * pack, :] >= cand_b, one16, zero16)
                    parts[n % 4] = parts[n % 4] + hit
                    n += 1
            return (parts[0] + parts[1]) + (parts[2] + parts[3])

        acc = lax.fori_loop(0, ngroups, body, zero16)
        return jnp.sum(acc.astype(jnp.int32).astype(F32), axis=0, keepdims=True)

    def kth_largest(ref, rank, lo0, hi0, stop_on_hit):
        def unsettled(carry):
            lo, hi, _ = carry
            return jnp.max((hi - lo).astype(F32)) > 0.0

        def halve(carry):
            lo, hi, above = carry
            mid = lo + lax.shift_right_logical(hi - lo + 1, jnp.int32(1))
            cnt = count_ge(ref, mid)
            ok = cnt >= rank
            lo, hi, above = jnp.where(ok, mid, lo), jnp.where(ok, hi, mid - 1), jnp.where(ok, above, cnt)
            if stop_on_hit:
                hit = cnt == rank
                lo, hi = jnp.where(hit, mid, lo), jnp.where(hit, mid, hi)
            return lo, hi, above

        def body(carry):
            return halve(halve(carry))

        lo, _, above = lax.while_loop(unsettled, body, (lo0, hi0, jnp.zeros((1, tq), F32)))
        return lo, above

    t_hi, above = kth_largest(hi_ref, jnp.full((1, tq), float(TOP_K), F32), t_min, t_max, False)
    t_hi_b = jnp.broadcast_to(t_hi, (tk, tq)).astype(i16)

    def narrow_body(j, carry):
        lo_ref[j] = jnp.where(hi_ref[j] == t_hi_b, lo_ref[j], jnp.full((tk, tq), -32768, i16))
        return carry

    lax.fori_loop(0, grp * ngroups, narrow_body, 0)
    t_lo, _ = kth_largest(lo_ref, float(TOP_K) - above, jnp.full((1, tq), -32768, jnp.int32),
                          jnp.full((1, tq), 32767, jnp.int32), True)
    key = jnp.bitwise_or(jnp.left_shift(t_hi, 16), t_lo + 32768)
    key = jnp.clip(key, _KEY_NEG_INF, _KEY_POS_INF)
    thr = jnp.broadcast_to(_key_to_float(key), (tk, tq))

    def word_body(j, word):
        return jnp.bitwise_or(word, jnp.where(sc_ref[j] >= thr, jnp.left_shift(jnp.int32(1), j), 0))

    word_t = lax.fori_loop(0, i, word_body, jnp.zeros((tk, tq), jnp.int32))
    keep = jnp.logical_and(sc_ref[i] >= thr, key_pos + i * tk <= qry_pos)
    word_t = jnp.bitwise_or(word_t, jnp.where(keep, jnp.left_shift(jnp.int32(1), i), 0))
    o_ref[...] = jnp.transpose(word_t)


def _select(attn_grp, aux):
    tq = tk = SEL_T
    nq, nk = SEQ // tq, SEQ // tk
    return pl.pallas_call(
        _select_kernel,
        grid=(nq,),
        in_specs=[
            pl.BlockSpec((tq, N_IDX_HEADS * IDX_DIM), lambda i: (i, 3)),
            pl.BlockSpec((SEQ, LANE), lambda i: (0, AUX_KIDX // LANE)),
        ],
        out_specs=pl.BlockSpec((tq, tk), lambda i: (i, 0)),
        out_shape=jax.ShapeDtypeStruct((SEQ, tk), jnp.int32),
        scratch_shapes=[
            pltpu.VMEM((SEQ, IDX_DIM), BF16),
            pltpu.VMEM((nk, tk, tq), F32),
            pltpu.VMEM((nk, tk, tq), jnp.int16),
            pltpu.VMEM((nk, tk, tq), jnp.int16),
        ],
        compiler_params=_cparams("arbitrary"),
        name="select",
    )(attn_grp, aux)


def _attn_kernel(q_ref, k_ref, v_ref, w_ref, band_ref, o_ref, mask_ref, sa_ref, sb_ref, acc_ref):
    i = pl.program_id(0)
    tq, tk, unroll = ATT_TQ, ATT_TK, ATT_UNROLL
    dh = ATTN_HEAD_DIM
    c = (ATTN_HEAD_DIM ** -0.5) * LOG2E
    r = tq // tk
    ngrp = (r * (i + 1) + unroll - 1) // unroll
    ones = jnp.ones((unroll * tk, LANE), BF16)

    @pl.when(pl.program_id(1) == 0)
    def _():
        def expand(j, carry):
            keep = jnp.bitwise_and(w_ref[...], jnp.left_shift(jnp.int32(1), j)) != 0
            mask_ref[j] = jnp.where(keep, 0.0, MASK_NEG)
            return carry

        lax.fori_loop(0, unroll * ngrp, expand, 0)

    def logits(hh, g, s_ref, m_prev):
        hs = slice(hh * dh, (hh + 1) * dh)
        mx = jnp.full((tq, LANE), MASK_NEG, F32)
        for u in range(unroll):
            j = unroll * g + u
            kt = k_ref[pl.ds(pl.multiple_of(j * tk, tk), tk), hs]
            s2 = lax.dot_general(q_ref[:, hs], kt, _NT, preferred_element_type=F32) * c
            s2 = s2 + band_ref[jnp.clip(r * i + (r - 1) - j, 0, N_BAND - 1), hh] + mask_ref[j]
            s_ref[u] = s2
            for cc in range(tk // LANE):
                mx = jnp.maximum(mx, s2[:, cc * LANE:(cc + 1) * LANE])
        return jnp.maximum(m_prev, jnp.max(mx, axis=1, keepdims=True))

    def values(hh, g, s_ref, m_prev, m_cur):
        m_b = jnp.broadcast_to(m_cur, (tq, tk))
        p = jnp.concatenate([jnp.exp2(s_ref[u] - m_b).astype(BF16) for u in range(unroll)], axis=1)
        vt = v_ref[pl.ds(pl.multiple_of(g * (unroll * tk), unroll * tk), unroll * tk), hh * dh:(hh + 1) * dh]
        pv = jnp.dot(p, jnp.concatenate([vt, ones], axis=1), preferred_element_type=F32)
        acc_ref[hh] = acc_ref[hh] * jnp.exp2(m_prev - m_cur) + pv

    acc_ref[...] = jnp.zeros(acc_ref.shape, F32)
    m_none = jnp.full((tq, 1), MASK_NEG, F32)
    npair = (ngrp - 1) // 2
    last = 2 * npair

    def steady(hh, first, second, m_0):
        def pair(t, carry):
            m_prev, m_cur = carry
            m_1 = logits(hh, 2 * t + 1, second, m_cur)
            values(hh, 2 * t, first, m_prev, m_cur)
            m_2 = logits(hh, 2 * t + 2, first, m_1)
            values(hh, 2 * t + 1, second, m_cur, m_1)
            return m_1, m_2

        return lax.fori_loop(0, npair, pair, (m_none, m_0))

    @pl.when(ngrp - 1 == last)
    def _():
        m_prev, m_cur = steady(0, sa_ref, sb_ref, logits(0, 0, sa_ref, m_none))
        m_0 = logits(1, 0, sb_ref, m_none)
        values(0, last, sa_ref, m_prev, m_cur)
        m_prev, m_cur = steady(1, sb_ref, sa_ref, m_0)
        values(1, last, sb_ref, m_prev, m_cur)

    @pl.when(ngrp - 1 > last)
    def _():
        m_prev, m_cur = steady(0, sa_ref, sb_ref, logits(0, 0, sa_ref, m_none))
        m_1 = logits(0, last + 1, sb_ref, m_cur)
        values(0, last, sa_ref, m_prev, m_cur)
        m_0 = logits(1, 0, sa_ref, m_none)
        values(0, last + 1, sb_ref, m_cur, m_1)
        m_prev, m_cur = steady(1, sa_ref, sb_ref, m_0)
        m_1 = logits(1, last + 1, sb_ref, m_cur)
        values(1, last, sa_ref, m_prev, m_cur)
        values(1, last + 1, sb_ref, m_cur, m_1)

    for hh in range(2):
        o_ref[:, hh * dh:(hh + 1) * dh] = acc_ref[hh, :, :dh] / acc_ref[hh, :, dh:dh + 1]


def _attention(attn_grp, words, band):
    tq, tk = ATT_TQ, ATT_TK
    nq, nk = SEQ // tq, SEQ // tk
    dh2 = 2 * ATTN_HEAD_DIM
    npairs = N_ATTN_HEADS // 2
    return pl.pallas_call(
        _attn_kernel,
        grid=(nq, npairs),
        in_specs=[
            pl.BlockSpec((tq, dh2), lambda i, h: (i, h)),
            pl.BlockSpec((SEQ, dh2), lambda i, h: (0, npairs + h)),
            pl.BlockSpec((SEQ, dh2), lambda i, h: (0, 2 * npairs + h)),
            pl.BlockSpec((tq, tk), lambda i, h: (i, 0)),
            pl.BlockSpec((N_BAND, 2, tq, tk), lambda i, h: (0, h, 0, 0)),
        ],
        out_specs=pl.BlockSpec((tq, dh2), lambda i, h: (i, h)),
        out_shape=jax.ShapeDtypeStruct((SEQ, ATTN_WIDTH), F32),
        scratch_shapes=[pltpu.VMEM((nk, tq, tk), F32),
                        pltpu.VMEM((ATT_UNROLL, tq, tk), F32), pltpu.VMEM((ATT_UNROLL, tq, tk), F32),
                        pltpu.VMEM((2, tq, ATTN_HEAD_DIM + LANE), F32)],
        compiler_params=_cparams("parallel", "arbitrary"),
        name="attention",
    )(attn_grp, attn_grp, attn_grp, words, band)


POOL_HALO = 16
CONV_HALO = 8


def _block_diag_dot(x, w_ref):
    outs = []
    for g in range(w_ref.shape[0]):
        xs = x[:, g * LRU_BLOCK:(g + 1) * LRU_BLOCK].astype(BF16)
        outs.append(jnp.dot(xs, w_ref[g], preferred_element_type=F32))
    return jnp.concatenate(outs, axis=1)


def _pool_lru_kernel(pin_ref, lin_ref, gate_ref, pw_ref, ps_ref, cw_ref, cb_ref, rw_ref, rb_ref,
                     iw_ref, ib_ref, lam_ref, pool_o, lru_o, pprev_ref, lprev_ref, h_ref, *, T):
    i = pl.program_id(0)

    @pl.when(i == 0)
    def _():
        pprev_ref[...] = jnp.zeros(pprev_ref.shape, F32)
        lprev_ref[...] = jnp.zeros(lprev_ref.shape, F32)
        h_ref[...] = jnp.zeros(h_ref.shape, F32)

    t_glob = i * T + lax.broadcasted_iota(jnp.int32, (T, 1), 0)

    u = pin_ref[...]
    ext = jnp.concatenate([pprev_ref[...], u], axis=0)
    pprev_ref[...] = u[T - POOL_HALO:, :]
    p2 = ext[1:] + ext[:-1]
    p4 = p2[2:, POOL_GROUP:] + p2[:-2, POOL_GROUP:]
    p8 = p4[4:, POOL_GROUP:] + p4[:-4, POOL_GROUP:]
    p16 = p8[8:, POOL_GROUP:] + p8[:-8, POOL_GROUP:]
    wsums = (p2[15:15 + T, :POOL_GROUP], p4[13:13 + T, :POOL_GROUP],
             p8[9:9 + T, :POOL_GROUP], p16[1:1 + T, :])
    for g, win in enumerate(POOL_WINDOWS):
        gs = slice(g * POOL_GROUP, (g + 1) * POOL_GROUP)
        count = jnp.minimum(t_glob + 1, win).astype(F32)
        dlt = wsums[g] / count - u[:, gs]
        y = jnp.dot(dlt.astype(BF16), pw_ref[g], preferred_element_type=F32)
        pool_o[:, gs] = y * ps_ref[:, gs]

    x = lin_ref[...]
    lext = jnp.concatenate([lprev_ref[...], x], axis=0)
    lprev_ref[...] = x[T - CONV_HALO:, :]
    xc = cb_ref[...]
    for jj in range(CONV_WIDTH):
        off = CONV_HALO - (CONV_WIDTH - 1) + jj
        xc = xc + cw_ref[jj:jj + 1, :] * lext[off:off + T, :]
    r = jax.nn.sigmoid(_block_diag_dot(xc, rw_ref) + rb_ref[...])
    ig = jax.nn.sigmoid(_block_diag_dot(xc, iw_ref) + ib_ref[...])
    z = -lam_ref[...]
    softplus = jnp.maximum(z, 0.0) + jnp.log1p(jnp.exp(-jnp.abs(z)))
    log_a = (-LRU_C * r) * softplus
    a = jnp.exp(log_a)
    b = jnp.sqrt(-jnp.tanh(log_a) * (a * a + 1.0)) * (ig * xc)

    ngrp = T // SUBLANE
    a3 = a.reshape(ngrp, SUBLANE, LRU_WIDTH)
    b3 = b.reshape(ngrp, SUBLANE, LRU_WIDTH)
    sub = lax.broadcasted_iota(jnp.int32, a3.shape, 1)
    sh = 1
    while sh < SUBLANE:
        a_sh = jnp.where(sub >= sh, pltpu.roll(a3, sh, axis=1), 1.0)
        b_sh = jnp.where(sub >= sh, pltpu.roll(b3, sh, axis=1), 0.0)
        b3 = a3 * b_sh + b3
        a3 = a3 * a_sh
        sh *= 2
    carry = h_ref[...]
    rows = []
    for grp in range(ngrp):
        hg = a3[grp] * carry + b3[grp]
        rows.append(hg)
        carry = hg[SUBLANE - 1:, :]
    h = jnp.concatenate(rows, axis=0)
    h_ref[...] = carry

    gt = gate_ref[...]
    cdf = 0.5 * (1.0 + jnp.tanh(math.sqrt(2.0 / math.pi) * (gt + 0.044715 * (gt * gt * gt))))
    lru_o[...] = h * (gt * cdf)


def _pool_lru(aux, pool_w, pool_scale, conv_w, conv_b, r_w, r_b, i_w, i_b, lam):
    T = 256
    full = lambda a: pl.BlockSpec(a.shape, lambda i: (0,) * a.ndim)
    row = lambda v: v.reshape(1, -1)
    args = (pool_w.astype(BF16), row(pool_scale), conv_w, row(conv_b), r_w.astype(BF16), row(r_b),
            i_w.astype(BF16), row(i_b), row(lam))
    return pl.pallas_call(
        functools.partial(_pool_lru_kernel, T=T),
        grid=(SEQ // T,),
        in_specs=[
            pl.BlockSpec((T, POOL_WIDTH), lambda i: (i, AUX_POOL // POOL_WIDTH)),
            pl.BlockSpec((T, LRU_WIDTH), lambda i: (i, AUX_LRU_IN // LRU_WIDTH)),
            pl.BlockSpec((T, LRU_WIDTH), lambda i: (i, AUX_LRU_GATE // LRU_WIDTH)),
        ] + [full(a) for a in args],
        out_specs=[pl.BlockSpec((T, POOL_WIDTH), lambda i: (i, 0)),
                   pl.BlockSpec((T, LRU_WIDTH), lambda i: (i, 0))],
        out_shape=[jax.ShapeDtypeStruct((SEQ, POOL_WIDTH), F32),
                   jax.ShapeDtypeStruct((SEQ, LRU_WIDTH), F32)],
        scratch_shapes=[pltpu.VMEM((POOL_HALO, POOL_WIDTH), F32),
                        pltpu.VMEM((CONV_HALO, LRU_WIDTH), F32),
                        pltpu.VMEM((1, LRU_WIDTH), F32)],
        compiler_params=_cparams("arbitrary"),
        name="pool_lru",
    )(aux, aux, aux, *args)


def _mix_out_kernel(a_ref, p_ref, l_ref, x_ref, ga_ref, gp_ref, gl_ref, w_ref, gpost_ref, gate_ref,
                    gmlp_ref, sc_ref, sh_ref, wup_ref, o_ref, h_ref, wupb_ref):
    na = _rms(a_ref[...], ga_ref[...]).astype(BF16)
    npool = _rms(p_ref[...], gp_ref[...]).astype(BF16)
    nl = _rms(l_ref[...], gl_ref[...]).astype(BF16)
    p0, p1 = ATTN_WIDTH, ATTN_WIDTH + POOL_WIDTH
    y = jnp.dot(na, w_ref[0:p0, :], preferred_element_type=F32)
    y = y + jnp.dot(npool, w_ref[p0:p1, :], preferred_element_type=F32)
    y = y + jnp.dot(nl, w_ref[p1:, :], preferred_element_type=F32)
    x_new = x_ref[...] + gate_ref[...] * _rms(y, gpost_ref[...])
    o_ref[...] = x_new
    h_ref[...] = (_rms(x_new, gmlp_ref[...]) * (1.0 + sc_ref[...]) + sh_ref[...]).astype(BF16)
    wupb_ref[...] = wup_ref[...].astype(BF16)


def _mix_out(attn, pool, lru, x, g_attn, g_pool, g_lru, w_out, layer, g_post, gate, g_mlp, scale2, shift2,
             w_up):
    tm = 512
    steps = SEQ // tm
    slab = w_up.shape[1] // steps
    vec = lambda n: pl.BlockSpec((1, n), lambda i: (0, 0))
    rows = lambda n: pl.BlockSpec((tm, n), lambda i: (i, 0))
    return pl.pallas_call(
        _mix_out_kernel,
        grid=(steps,),
        in_specs=[rows(ATTN_WIDTH), rows(POOL_WIDTH), rows(LRU_WIDTH), rows(D_MODEL),
                  vec(ATTN_WIDTH), vec(POOL_WIDTH), vec(LRU_WIDTH),
                  pl.BlockSpec((None, D_MODEL, D_MODEL), lambda i: (layer, 0, 0)),
                  vec(D_MODEL), vec(D_MODEL), vec(D_MODEL), vec(D_MODEL), vec(D_MODEL),
                  pl.BlockSpec((None, slab, w_up.shape[2]), lambda i: (layer, i, 0))],
        out_specs=[rows(D_MODEL), rows(D_MODEL), pl.BlockSpec((slab, w_up.shape[2]), lambda i: (i, 0))],
        out_shape=[jax.ShapeDtypeStruct((SEQ, D_MODEL), F32), jax.ShapeDtypeStruct((SEQ, D_MODEL), BF16),
                   jax.ShapeDtypeStruct(w_up.shape[1:], BF16)],
        compiler_params=_cparams("parallel"),
        name="mix_out",
    )(attn, pool, lru, x, g_attn, g_pool, g_lru, w_out, g_post, gate, g_mlp, scale2, shift2, w_up)


def _mlp_down_kernel(u_ref, w_ref, x_ref, gpost_ref, gate_ref, o_ref):
    y = jnp.dot(u_ref[...], w_ref[...], preferred_element_type=F32)
    o_ref[...] = x_ref[...] + gate_ref[...] * _rms(y, gpost_ref[...])


def _mlp_down(u2, w_down, x, g_post, gate):
    tm = 256
    vec = pl.BlockSpec((1, D_MODEL), lambda i: (0, 0))
    return pl.pallas_call(
        _mlp_down_kernel,
        grid=(SEQ // tm,),
        in_specs=[pl.BlockSpec((tm, MLP_HIDDEN), lambda i: (i, 0)),
                  pl.BlockSpec((MLP_HIDDEN, D_MODEL), lambda i: (0, 0), pipeline_mode=pl.Buffered(1)),
                  pl.BlockSpec((tm, D_MODEL), lambda i: (i, 0)),
                  vec, vec],
        out_specs=pl.BlockSpec((tm, D_MODEL), lambda i: (i, 0)),
        out_shape=jax.ShapeDtypeStruct((SEQ, D_MODEL), F32),
        compiler_params=_cparams("parallel"),
        name="mlp_down",
    )(u2, w_down, x, g_post, gate)


def kernel(x, c, positions, rel_bias, w_mod, b_mod, g_pre_mix, g_post_mix, g_pre_mlp, g_post_mlp, w_in, pool_w, pool_scale, conv_w, conv_b, gate_r_w, gate_r_b, gate_i_w, gate_i_b, lru_lambda, g_attn_out, g_pool_out, g_lru_out, w_out, w_mlp_up, w_mlp_down):
    assert x.shape == (1, SEQ, D_MODEL) and positions.shape == (1, SEQ)
    del positions
    xs = x[0]
    mod = _modulation(c, w_mod, b_mod)
    band = _bias_band(rel_bias)
    row = lambda v: v.reshape(1, -1)
    n_attn_cols = 4 * ATTN_WIDTH
    aux_lo = n_attn_cols + IDX_DIM + N_IDX_HEADS
    w_in_t = jnp.swapaxes(w_in, 1, 2).astype(BF16)
    w_aux = jnp.concatenate(
        [w_in_t[:, aux_lo:, :], w_in_t[:, n_attn_cols:aux_lo, :],
         jnp.zeros((DEPTH, AUX_WIDTH - (AUX_KIDX + IDX_DIM + N_IDX_HEADS), D_MODEL), BF16)], axis=1)
    w_out_b = w_out.astype(BF16)
    for l in range(DEPTH):
        shift1, scale1, gate1, shift2, scale2, gate2 = [
            mod[l, :, n * D_MODEL:(n + 1) * D_MODEL] for n in range(6)]
        g_pre = row(g_pre_mix[l])
        attn_grp, aux = _in_projection(xs, g_pre, scale1, shift1, w_in_t, n_attn_cols, w_aux, l)
        words = _select(attn_grp, aux)
        attn = _attention(attn_grp, words, band)
        pool, lru = _pool_lru(aux, pool_w[l], pool_scale[l], conv_w[l], conv_b[l], gate_r_w[l],
                              gate_r_b[l], gate_i_w[l], gate_i_b[l], lru_lambda[l])
        xs, h_mlp, w_up_b = _mix_out(attn, pool, lru, xs, row(g_attn_out[l]), row(g_pool_out[l]),
                                     row(g_lru_out[l]), w_out_b, l, row(g_post_mix[l]), gate1,
                                     row(g_pre_mlp[l]), scale2, shift2, w_mlp_up)
        u2, w_down_b = _mlp_up(h_mlp, w_up_b, w_mlp_down, l)
        xs = _mlp_down(u2, w_down_b, xs, row(g_post_mlp[l]), gate2)
    return xs[None]
```

```python
import functools
import math

import jax
import jax.numpy as jnp
from jax import lax
from jax.experimental import pallas as pl
from jax.experimental.pallas import tpu as pltpu

D_MODEL = 2048
SEQ = 8192
DEPTH = 2
ATTN_HEAD_DIM = 128
ATTN_WIDTH = 1024
N_ATTN_HEADS = 8
POOL_WIDTH = 512
POOL_WINDOWS = (2, 4, 8, 16)
POOL_GROUP = 128
LRU_WIDTH = 512
N_LRU_BLOCKS = 4
LRU_BLOCK = 128
LRU_C = 8.0
CONV_WIDTH = 4
N_IDX_HEADS = 16
IDX_DIM = 64
TOP_K = 256
N_BUCKETS = 32
MAX_DISTANCE = 128
MLP_HIDDEN = 4 * D_MODEL
NORM_EPS = 1e-6

AUX_POOL = 0
AUX_LRU_IN = 512
AUX_LRU_GATE = 1024
AUX_KIDX = 1536
AUX_WIDTH = 1664
LANE = 128
SUBLANE = 8

SEL_T = 256
SEL_GROUP = 2
SEL_SCORE_GROUP = 4
ATT_TQ = 512
ATT_TK = SEL_T
ATT_UNROLL = 4
N_BAND = ATT_TQ // ATT_TK + 2
assert MAX_DISTANCE <= ATT_TK // 2 and SEQ // SEL_T <= 32 and SEL_T >= TOP_K
MASK_NEG = -1e30
LOG2E = math.log2(math.e)
VMEM_LIMIT = 56 * 1024 * 1024

F32 = jnp.float32
BF16 = jnp.bfloat16
_NT = (((1,), (1,)), ((), ()))


def _cparams(*sem):
    return pltpu.CompilerParams(dimension_semantics=sem, vmem_limit_bytes=VMEM_LIMIT)


def _rms(x, g):
    ms = jnp.mean(x * x, axis=-1, keepdims=True)
    return (x * lax.rsqrt(ms + NORM_EPS)) * g


def _mod_kernel(c_ref, w_ref, b_ref, o_ref):
    c = c_ref[...]
    c_act = c * jax.nn.sigmoid(c)
    o_ref[0] = jnp.sum(c_act * w_ref[0], axis=0, keepdims=True) + b_ref[0]


def _modulation(c, w_mod, b_mod):
    tn = 1024
    n = w_mod.shape[-1]
    return pl.pallas_call(
        _mod_kernel,
        grid=(DEPTH, n // tn),
        in_specs=[
            pl.BlockSpec((D_MODEL, 1), lambda l, j: (0, 0)),
            pl.BlockSpec((1, D_MODEL, tn), lambda l, j: (l, 0, j)),
            pl.BlockSpec((1, 1, tn), lambda l, j: (l, 0, j)),
        ],
        out_specs=pl.BlockSpec((1, 1, tn), lambda l, j: (l, 0, j)),
        out_shape=jax.ShapeDtypeStruct((DEPTH, 1, n), F32),
        compiler_params=_cparams("parallel", "parallel"),
        name="modulation",
    )(c.reshape(D_MODEL, 1), w_mod, b_mod.reshape(DEPTH, 1, n))


def _mlp_up_kernel(h_ref, w_ref, wd_ref, o_ref, wdb_ref):
    u = jnp.maximum(jnp.dot(h_ref[...], w_ref[...], preferred_element_type=F32), 0.0)
    o_ref[...] = (u * u).astype(o_ref.dtype)
    wdb_ref[...] = wd_ref[...].astype(BF16)


def _mlp_up(h, w_up_b, w_down, layer):
    tm = 512
    m, k = h.shape
    n = w_up_b.shape[1]
    halves = 2
    tn = n // halves
    steps_m = m // tm
    slab = w_down.shape[1] // (halves * steps_m)
    return pl.pallas_call(
        _mlp_up_kernel,
        grid=(halves, steps_m),
        in_specs=[
            pl.BlockSpec((tm, k), lambda j, i: (i, 0)),
            pl.BlockSpec((k, tn), lambda j, i: (0, j), pipeline_mode=pl.Buffered(1)),
            pl.BlockSpec((None, slab, w_down.shape[2]), lambda j, i: (layer, j * steps_m + i, 0)),
        ],
        out_specs=[pl.BlockSpec((tm, tn), lambda j, i: (i, j)),
                   pl.BlockSpec((slab, w_down.shape[2]), lambda j, i: (j * steps_m + i, 0))],
        out_shape=[jax.ShapeDtypeStruct((m, n), BF16), jax.ShapeDtypeStruct(w_down.shape[1:], BF16)],
        compiler_params=_cparams("parallel", "parallel"),
        name="mlp_up",
    )(h, w_up_b, w_down)


def _proj_kernel(x_ref, g_ref, sc_ref, sh_ref, wa_ref, wx_ref, oa_ref, ox_ref):
    h = (_rms(x_ref[...], g_ref[...]) * (1.0 + sc_ref[...]) + sh_ref[...]).astype(BF16)
    oa_ref[...] = lax.dot_general(h, wa_ref[...], _NT, preferred_element_type=F32).astype(oa_ref.dtype)
    ox_ref[...] = lax.dot_general(h, wx_ref[...], _NT, preferred_element_type=F32)


def _in_projection(x, g, scale, shift, w_attn_t, na, w_aux_t, layer):
    tm = 512
    m, k = x.shape
    nx = w_aux_t.shape[1]
    vec = pl.BlockSpec((1, k), lambda i: (0, 0))
    resident = lambda n: pl.BlockSpec((None, n, k), lambda i: (layer, 0, 0), pipeline_mode=pl.Buffered(1))
    return pl.pallas_call(
        _proj_kernel,
        grid=(m // tm,),
        in_specs=[pl.BlockSpec((tm, k), lambda i: (i, 0)), vec, vec, vec, resident(na), resident(nx)],
        out_specs=[pl.BlockSpec((tm, na), lambda i: (i, 0)), pl.BlockSpec((tm, nx), lambda i: (i, 0))],
        out_shape=[jax.ShapeDtypeStruct((m, na), BF16), jax.ShapeDtypeStruct((m, nx), F32)],
        compiler_params=_cparams("parallel"),
        name="in_proj",
    )(x, g, scale, shift, w_attn_t, w_aux_t)


def _band_kernel(rb_ref, o_ref):
    d = pl.program_id(0)
    t = lax.broadcasted_iota(jnp.int32, (ATT_TQ, ATT_TK), 0)
    s = lax.broadcasted_iota(jnp.int32, (ATT_TQ, ATT_TK), 1)
    rel = jnp.maximum((d - (ATT_TQ // ATT_TK - 1)) * ATT_TK + t - s, 0)
    max_exact = N_BUCKETS // 2
    relf = jnp.maximum(rel.astype(F32), 1.0)
    large = max_exact + (jnp.log(relf / max_exact) / math.log(MAX_DISTANCE / max_exact)
                         * (N_BUCKETS - max_exact)).astype(jnp.int32)
    large = jnp.minimum(large, N_BUCKETS - 1)
    bucket = jnp.where(rel < max_exact, rel, large)
    for h in range(N_ATTN_HEADS):
        table = jnp.broadcast_to(rb_ref[h:h + 1, :], (ATT_TQ, LANE))
        val = jnp.concatenate(
            [jnp.take_along_axis(table, bucket[:, cc * LANE:(cc + 1) * LANE], axis=1)
             for cc in range(ATT_TK // LANE)], axis=1)
        o_ref[0, h] = val * LOG2E


def _bias_band(rel_bias):
    return pl.pallas_call(
        _band_kernel,
        grid=(N_BAND,),
        in_specs=[pl.BlockSpec((N_ATTN_HEADS, LANE), lambda d: (0, 0))],
        out_specs=pl.BlockSpec((1, N_ATTN_HEADS, ATT_TQ, ATT_TK), lambda d: (d, 0, 0, 0)),
        out_shape=jax.ShapeDtypeStruct((N_BAND, N_ATTN_HEADS, ATT_TQ, ATT_TK), F32),
        compiler_params=_cparams("parallel"),
        name="bias_band",
    )(jnp.pad(rel_bias.T, ((0, 0), (0, LANE - N_BUCKETS))))


def _float_to_key(x):
    bits = lax.bitcast_convert_type(x, jnp.int32)
    return jnp.where(bits >= 0, bits, jnp.bitwise_xor(bits, jnp.int32(0x7FFFFFFF)))


_KEY_POS_INF = 0x7F800000
_KEY_NEG_INF = 0x807FFFFF - (1 << 32)


def _key_to_float(key):
    bits = jnp.where(key >= 0, key, jnp.bitwise_xor(key, jnp.int32(0x7FFFFFFF)))
    return lax.bitcast_convert_type(bits, F32)


def _select_kernel(qi_ref, kw_ref, o_ref, kb_ref, sc_ref, hi_ref, lo_ref):
    i = pl.program_id(0)
    tq = tk = SEL_T
    idx_scale = (N_IDX_HEADS * IDX_DIM) ** -0.5

    @pl.when(i == 0)
    def _():
        kb_ref[...] = kw_ref[:, 0:IDX_DIM].astype(BF16)

    w_t = jnp.transpose(kw_ref[pl.ds(pl.multiple_of(i * tq, tq), tq), :])

    nchunks = i + 1
    key_pos = lax.broadcasted_iota(jnp.int32, (tk, tq), 0)
    qry_pos = i * tq + lax.broadcasted_iota(jnp.int32, (tk, tq), 1)
    grp = SEL_GROUP

    def high_half(key):
        return lax.shift_right_arithmetic(key, jnp.int32(16))

    def score_tiles(j0, ntiles, below_diagonal, gmax):
        kc = kb_ref[pl.ds(pl.multiple_of(j0 * tk, tk), ntiles * tk), :]
        acc = jnp.zeros((ntiles * tk, tq), F32)
        for h in range(N_IDX_HEADS):
            d = lax.dot_general(kc, qi_ref[:, h * IDX_DIM:(h + 1) * IDX_DIM], _NT,
                                preferred_element_type=F32)
            acc = acc + w_t[IDX_DIM + h:IDX_DIM + h + 1, :] * jnp.maximum(d, 0.0)
        for u in range(ntiles):
            j = j0 + u
            score = acc[u * tk:(u + 1) * tk] * idx_scale
            if not below_diagonal:
                score = jnp.where(key_pos + j * tk <= qry_pos, score, -jnp.inf)
            sc_ref[j] = score
            gmax = jnp.maximum(gmax, score)
            key = _float_to_key(score)
            hi_ref[j] = high_half(key).astype(jnp.int16)
            lo_ref[j] = (jnp.bitwise_and(key, jnp.int32(0xFFFF)) - 32768).astype(jnp.int16)
        return gmax

    ngroups = (nchunks + grp - 1) // grp
    big = SEL_SCORE_GROUP
    nbig = i // big
    nhuge = i // (2 * big)
    gmax = lax.fori_loop(0, nhuge, lambda jg, gm: score_tiles(2 * big * jg, 2 * big, True, gm),
                         jnp.full((tk, tq), -jnp.inf, F32))
    gmax = lax.fori_loop(2 * nhuge, nbig, lambda jg, gm: score_tiles(big * jg, big, True, gm), gmax)
    gmax = lax.fori_loop(nbig * (big // grp), ngroups,
                         lambda jg, gm: score_tiles(grp * jg, grp, False, gm), gmax)
    t_min = high_half(_float_to_key(jnp.min(gmax, axis=0, keepdims=True)))
    t_max = high_half(_float_to_key(jnp.max(gmax, axis=0, keepdims=True)))

    i16 = jnp.int16
    pack = 16
    one16 = jnp.ones((pack, tq), i16)
    zero16 = jnp.zeros((pack, tq), i16)

    def count_ge(ref, cand):
        cand_b = jnp.broadcast_to(cand, (pack, tq)).astype(i16)

        def body(jg, acc):
            parts = [acc, zero16, zero16, zero16]
            n = 0
            for u in range(grp):
                x = ref[grp * jg + u]
                for g in range(tk // pack):
                    hit = jnp.where(x[g * pack:(g + 1) * pack, :] >= cand_b, one16, zero16)
                    parts[n % 4] = parts[n % 4] + hit
                    n += 1
            return (parts[0] + parts[1]) + (parts[2] + parts[3])

        acc = lax.fori_loop(0, ngroups, body, zero16)
        return jnp.sum(acc.astype(jnp.int32).astype(F32), axis=0, keepdims=True)

    def kth_largest(ref, rank, lo0, hi0, stop_on_hit):
        def unsettled(carry):
            lo, hi, _ = carry
            return jnp.max((hi - lo).astype(F32)) > 0.0

        def halve(carry):
            lo, hi, above = carry
            mid = lo + lax.shift_right_logical(hi - lo + 1, jnp.int32(1))
            cnt = count_ge(ref, mid)
            ok = cnt >= rank
            lo, hi, above = jnp.where(ok, mid, lo), jnp.where(ok, hi, mid - 1), jnp.where(ok, above, cnt)
            if stop_on_hit:
                hit = cnt == rank
                lo, hi = jnp.where(hit, mid, lo), jnp.where(hit, mid, hi)
            return lo, hi, above

        def body(carry):
            return halve(halve(carry))

        lo, _, above = lax.while_loop(unsettled, body, (lo0, hi0, jnp.zeros((1, tq), F32)))
        return lo, above

    t_hi, above = kth_largest(hi_ref, jnp.full((1, tq), float(TOP_K), F32), t_min, t_max, False)
    t_hi_b = jnp.broadcast_to(t_hi, (tk, tq)).astype(i16)

    def narrow_body(j, carry):
        lo_ref[j] = jnp.where(hi_ref[j] == t_hi_b, lo_ref[j], jnp.full((tk, tq), -32768, i16))
        return carry

    lax.fori_loop(0, grp * ngroups, narrow_body, 0)
    t_lo, _ = kth_largest(lo_ref, float(TOP_K) - above, jnp.full((1, tq), -32768, jnp.int32),
                          jnp.full((1, tq), 32767, jnp.int32), True)
    key = jnp.bitwise_or(jnp.left_shift(t_hi, 16), t_lo + 32768)
    key = jnp.clip(key, _KEY_NEG_INF, _KEY_POS_INF)
    thr = jnp.broadcast_to(_key_to_float(key), (tk, tq))

    def word_body(j, word):
        return jnp.bitwise_or(word, jnp.where(sc_ref[j] >= thr, jnp.left_shift(jnp.int32(1), j), 0))

    word_t = lax.fori_loop(0, i, word_body, jnp.zeros((tk, tq), jnp.int32))
    keep = jnp.logical_and(sc_ref[i] >= thr, key_pos + i * tk <= qry_pos)
    word_t = jnp.bitwise_or(word_t, jnp.where(keep, jnp.left_shift(jnp.int32(1), i), 0))
    o_ref[...] = jnp.transpose(word_t)


def _select(attn_grp, aux):
    tq = tk = SEL_T
    nq, nk = SEQ // tq, SEQ // tk
    return pl.pallas_call(
        _select_kernel,
        grid=(nq,),
        in_specs=[
            pl.BlockSpec((tq, N_IDX_HEADS * IDX_DIM), lambda i: (i, 3)),
            pl.BlockSpec((SEQ, LANE), lambda i: (0, AUX_KIDX // LANE)),
        ],
        out_specs=pl.BlockSpec((tq, tk), lambda i: (i, 0)),
        out_shape=jax.ShapeDtypeStruct((SEQ, tk), jnp.int32),
        scratch_shapes=[
            pltpu.VMEM((SEQ, IDX_DIM), BF16),
            pltpu.VMEM((nk, tk, tq), F32),
            pltpu.VMEM((nk, tk, tq), jnp.int16),
            pltpu.VMEM((nk, tk, tq), jnp.int16),
        ],
        compiler_params=_cparams("arbitrary"),
        name="select",
    )(attn_grp, aux)


def _attn_kernel(q_ref, k_ref, v_ref, w_ref, band_ref, o_ref, mask_ref, sa_ref, sb_ref, acc_ref):
    i = pl.program_id(0)
    tq, tk, unroll = ATT_TQ, ATT_TK, ATT_UNROLL
    dh = ATTN_HEAD_DIM
    c = (ATTN_HEAD_DIM ** -0.5) * LOG2E
    r = tq // tk
    ngrp = (r * (i + 1) + unroll - 1) // unroll
    ones = jnp.ones((unroll * tk, LANE), BF16)

    @pl.when(pl.program_id(1) == 0)
    def _():
        def expand(j, carry):
            keep = jnp.bitwise_and(w_ref[...], jnp.left_shift(jnp.int32(1), j)) != 0
            mask_ref[j] = jnp.where(keep, 0.0, MASK_NEG)
            return carry

        lax.fori_loop(0, unroll * ngrp, expand, 0)

    def logits(hh, g, s_ref, m_prev):
        hs = slice(hh * dh, (hh + 1) * dh)
        mx = jnp.full((tq, LANE), MASK_NEG, F32)
        for u in range(unroll):
            j = unroll * g + u
            kt = k_ref[pl.ds(pl.multiple_of(j * tk, tk), tk), hs]
            s2 = lax.dot_general(q_ref[:, hs], kt, _NT, preferred_element_type=F32) * c
            s2 = s2 + band_ref[jnp.clip(r * i + (r - 1) - j, 0, N_BAND - 1), hh] + mask_ref[j]
            s_ref[u] = s2
            for cc in range(tk // LANE):
                mx = jnp.maximum(mx, s2[:, cc * LANE:(cc + 1) * LANE])
        return jnp.maximum(m_prev, jnp.max(mx, axis=1, keepdims=True))

    def values(hh, g, s_ref, m_prev, m_cur):
        m_b = jnp.broadcast_to(m_cur, (tq, tk))
        p = jnp.concatenate([jnp.exp2(s_ref[u] - m_b).astype(BF16) for u in range(unroll)], axis=1)
        vt = v_ref[pl.ds(pl.multiple_of(g * (unroll * tk), unroll * tk), unroll * tk), hh * dh:(hh + 1) * dh]
        pv = jnp.dot(p, jnp.concatenate([vt, ones], axis=1), preferred_element_type=F32)
        acc_ref[hh] = acc_ref[hh] * jnp.exp2(m_prev - m_cur) + pv

    acc_ref[...] = jnp.zeros(acc_ref.shape, F32)
    m_none = jnp.full((tq, 1), MASK_NEG, F32)
    npair = (ngrp - 1) // 2
    last = 2 * npair

    def steady(hh, first, second, m_0):
        def pair(t, carry):
            m_prev, m_cur = carry
            m_1 = logits(hh, 2 * t + 1, second, m_cur)
            values(hh, 2 * t, first, m_prev, m_cur)
            m_2 = logits(hh, 2 * t + 2, first, m_1)
            values(hh, 2 * t + 1, second, m_cur, m_1)
            return m_1, m_2

        return lax.fori_loop(0, npair, pair, (m_none, m_0))

    @pl.when(ngrp - 1 == last)
    def _():
        m_prev, m_cur = steady(0, sa_ref, sb_ref, logits(0, 0, sa_ref, m_none))
        m_0 = logits(1, 0, sb_ref, m_none)
        values(0, last, sa_ref, m_prev, m_cur)
        m_prev, m_cur = steady(1, sb_ref, sa_ref, m_0)
        values(1, last, sb_ref, m_prev, m_cur)

    @pl.when(ngrp - 1 > last)
    def _():
        m_prev, m_cur = steady(0, sa_ref, sb_ref, logits(0, 0, sa_ref, m_none))
        m_1 = logits(0, last + 1, sb_ref, m_cur)
        values(0, last, sa_ref, m_prev, m_cur)
        m_0 = logits(1, 0, sa_ref, m_none)
        values(0, last + 1, sb_ref, m_cur, m_1)
        m_prev, m_cur = steady(1, sa_ref, sb_ref, m_0)
        m_1 = logits(1, last + 1, sb_ref, m_cur)
        values(1, last, sa_ref, m_prev, m_cur)
        values(1, last + 1, sb_ref, m_cur, m_1)

    for hh in range(2):
        o_ref[:, hh * dh:(hh + 1) * dh] = acc_ref[hh, :, :dh] / acc_ref[hh, :, dh:dh + 1]


def _attention(attn_grp, words, band):
    tq, tk = ATT_TQ, ATT_TK
    nq, nk = SEQ // tq, SEQ // tk
    dh2 = 2 * ATTN_HEAD_DIM
    npairs = N_ATTN_HEADS // 2
    return pl.pallas_call(
        _attn_kernel,
        grid=(nq, npairs),
        in_specs=[
            pl.BlockSpec((tq, dh2), lambda i, h: (i, h)),
            pl.BlockSpec((SEQ, dh2), lambda i, h: (0, npairs + h)),
            pl.BlockSpec((SEQ, dh2), lambda i, h: (0, 2 * npairs + h)),
            pl.BlockSpec((tq, tk), lambda i, h: (i, 0)),
            pl.BlockSpec((N_BAND, 2, tq, tk), lambda i, h: (0, h, 0, 0)),
        ],
        out_specs=pl.BlockSpec((tq, dh2), lambda i, h: (i, h)),
        out_shape=jax.ShapeDtypeStruct((SEQ, ATTN_WIDTH), F32),
        scratch_shapes=[pltpu.VMEM((nk, tq, tk), F32),
                        pltpu.VMEM((ATT_UNROLL, tq, tk), F32), pltpu.VMEM((ATT_UNROLL, tq, tk), F32),
                        pltpu.VMEM((2, tq, ATTN_HEAD_DIM + LANE), F32)],
        compiler_params=_cparams("parallel", "arbitrary"),
        name="attention",
    )(attn_grp, attn_grp, attn_grp, words, band)


POOL_HALO = 16
CONV_HALO = 8


def _block_diag_dot(x, w_ref):
    outs = []
    for g in range(w_ref.shape[0]):
        xs = x[:, g * LRU_BLOCK:(g + 1) * LRU_BLOCK].astype(BF16)
        outs.append(jnp.dot(xs, w_ref[g], preferred_element_type=F32))
    return jnp.concatenate(outs, axis=1)


def _pool_lru_kernel(pin_ref, lin_ref, gate_ref, pw_ref, ps_ref, cw_ref, cb_ref, rw_ref, rb_ref,
                     iw_ref, ib_ref, lam_ref, pool_o, lru_o, pprev_ref, lprev_ref, h_ref, *, T):
    i = pl.program_id(0)

    @pl.when(i == 0)
    def _():
        pprev_ref[...] = jnp.zeros(pprev_ref.shape, F32)
        lprev_ref[...] = jnp.zeros(lprev_ref.shape, F32)
        h_ref[...] = jnp.zeros(h_ref.shape, F32)

    t_glob = i * T + lax.broadcasted_iota(jnp.int32, (T, 1), 0)

    u = pin_ref[...]
    ext = jnp.concatenate([pprev_ref[...], u], axis=0)
    pprev_ref[...] = u[T - POOL_HALO:, :]
    p2 = ext[1:] + ext[:-1]
    p4 = p2[2:, POOL_GROUP:] + p2[:-2, POOL_GROUP:]
    p8 = p4[4:, POOL_GROUP:] + p4[:-4, POOL_GROUP:]
    p16 = p8[8:, POOL_GROUP:] + p8[:-8, POOL_GROUP:]
    wsums = (p2[15:15 + T, :POOL_GROUP], p4[13:13 + T, :POOL_GROUP],
             p8[9:9 + T, :POOL_GROUP], p16[1:1 + T, :])
    for g, win in enumerate(POOL_WINDOWS):
        gs = slice(g * POOL_GROUP, (g + 1) * POOL_GROUP)
        count = jnp.minimum(t_glob + 1, win).astype(F32)
        dlt = wsums[g] / count - u[:, gs]
        y = jnp.dot(dlt.astype(BF16), pw_ref[g], preferred_element_type=F32)
        pool_o[:, gs] = y * ps_ref[:, gs]

    x = lin_ref[...]
    lext = jnp.concatenate([lprev_ref[...], x], axis=0)
    lprev_ref[...] = x[T - CONV_HALO:, :]
    xc = cb_ref[...]
    for jj in range(CONV_WIDTH):
        off = CONV_HALO - (CONV_WIDTH - 1) + jj
        xc = xc + cw_ref[jj:jj + 1, :] * lext[off:off + T, :]
    r = jax.nn.sigmoid(_block_diag_dot(xc, rw_ref) + rb_ref[...])
    ig = jax.nn.sigmoid(_block_diag_dot(xc, iw_ref) + ib_ref[...])
    z = -lam_ref[...]
    softplus = jnp.maximum(z, 0.0) + jnp.log1p(jnp.exp(-jnp.abs(z)))
    log_a = (-LRU_C * r) * softplus
    a = jnp.exp(log_a)
    b = jnp.sqrt(-jnp.tanh(log_a) * (a * a + 1.0)) * (ig * xc)

    ngrp = T // SUBLANE
    a3 = a.reshape(ngrp, SUBLANE, LRU_WIDTH)
    b3 = b.reshape(ngrp, SUBLANE, LRU_WIDTH)
    sub = lax.broadcasted_iota(jnp.int32, a3.shape, 1)
    sh = 1
    while sh < SUBLANE:
        a_sh = jnp.where(sub >= sh, pltpu.roll(a3, sh, axis=1), 1.0)
        b_sh = jnp.where(sub >= sh, pltpu.roll(b3, sh, axis=1), 0.0)
        b3 = a3 * b_sh + b3
        a3 = a3 * a_sh
        sh *= 2
    carry = h_ref[...]
    rows = []
    for grp in range(ngrp):
        hg = a3[grp] * carry + b3[grp]
        rows.append(hg)
        carry = hg[SUBLANE - 1:, :]
    h = jnp.concatenate(rows, axis=0)
    h_ref[...] = carry

    gt = gate_ref[...]
    cdf = 0.5 * (1.0 + jnp.tanh(math.sqrt(2.0 / math.pi) * (gt + 0.044715 * (gt * gt * gt))))
    lru_o[...] = h * (gt * cdf)


def _pool_lru(aux, pool_w, pool_scale, conv_w, conv_b, r_w, r_b, i_w, i_b, lam):
    T = 256
    full = lambda a: pl.BlockSpec(a.shape, lambda i: (0,) * a.ndim)
    row = lambda v: v.reshape(1, -1)
    args = (pool_w.astype(BF16), row(pool_scale), conv_w, row(conv_b), r_w.astype(BF16), row(r_b),
            i_w.astype(BF16), row(i_b), row(lam))
    return pl.pallas_call(
        functools.partial(_pool_lru_kernel, T=T),
        grid=(SEQ // T,),
        in_specs=[
            pl.BlockSpec((T, POOL_WIDTH), lambda i: (i, AUX_POOL // POOL_WIDTH)),
            pl.BlockSpec((T, LRU_WIDTH), lambda i: (i, AUX_LRU_IN // LRU_WIDTH)),
            pl.BlockSpec((T, LRU_WIDTH), lambda i: (i, AUX_LRU_GATE // LRU_WIDTH)),
        ] + [full(a) for a in args],
        out_specs=[pl.BlockSpec((T, POOL_WIDTH), lambda i: (i, 0)),
                   pl.BlockSpec((T, LRU_WIDTH), lambda i: (i, 0))],
        out_shape=[jax.ShapeDtypeStruct((SEQ, POOL_WIDTH), F32),
                   jax.ShapeDtypeStruct((SEQ, LRU_WIDTH), F32)],
        scratch_shapes=[pltpu.VMEM((POOL_HALO, POOL_WIDTH), F32),
                        pltpu.VMEM((CONV_HALO, LRU_WIDTH), F32),
                        pltpu.VMEM((1, LRU_WIDTH), F32)],
        compiler_params=_cparams("arbitrary"),
        name="pool_lru",
    )(aux, aux, aux, *args)


def _mix_out_kernel(a_ref, p_ref, l_ref, x_ref, ga_ref, gp_ref, gl_ref, w_ref, gpost_ref, gate_ref,
                    gmlp_ref, sc_ref, sh_ref, wup_ref, o_ref, h_ref, wupb_ref):
    na = _rms(a_ref[...], ga_ref[...]).astype(BF16)
    npool = _rms(p_ref[...], gp_ref[...]).astype(BF16)
    nl = _rms(l_ref[...], gl_ref[...]).astype(BF16)
    p0, p1 = ATTN_WIDTH, ATTN_WIDTH + POOL_WIDTH
    y = jnp.dot(na, w_ref[0:p0, :], preferred_element_type=F32)
    y = y + jnp.dot(npool, w_ref[p0:p1, :], preferred_element_type=F32)
    y = y + jnp.dot(nl, w_ref[p1:, :], preferred_element_type=F32)
    x_new = x_ref[...] + gate_ref[...] * _rms(y, gpost_ref[...])
    o_ref[...] = x_new
    h_ref[...] = (_rms(x_new, gmlp_ref[...]) * (1.0 + sc_ref[...]) + sh_ref[...]).astype(BF16)
    wupb_ref[...] = wup_ref[...].astype(BF16)


def _mix_out(attn, pool, lru, x, g_attn, g_pool, g_lru, w_out, layer, g_post, gate, g_mlp, scale2, shift2,
             w_up):
    tm = 512
    steps = SEQ // tm
    slab = w_up.shape[1] // steps
    vec = lambda n: pl.BlockSpec((1, n), lambda i: (0, 0))
    rows = lambda n: pl.BlockSpec((tm, n), lambda i: (i, 0))
    return pl.pallas_call(
        _mix_out_kernel,
        grid=(steps,),
        in_specs=[rows(ATTN_WIDTH), rows(POOL_WIDTH), rows(LRU_WIDTH), rows(D_MODEL),
                  vec(ATTN_WIDTH), vec(POOL_WIDTH), vec(LRU_WIDTH),
                  pl.BlockSpec((None, D_MODEL, D_MODEL), lambda i: (layer, 0, 0)),
                  vec(D_MODEL), vec(D_MODEL), vec(D_MODEL), vec(D_MODEL), vec(D_MODEL),
                  pl.BlockSpec((None, slab, w_up.shape[2]), lambda i: (layer, i, 0))],
        out_specs=[rows(D_MODEL), rows(D_MODEL), pl.BlockSpec((slab, w_up.shape[2]), lambda i: (i, 0))],
        out_shape=[jax.ShapeDtypeStruct((SEQ, D_MODEL), F32), jax.ShapeDtypeStruct((SEQ, D_MODEL), BF16),
                   jax.ShapeDtypeStruct(w_up.shape[1:], BF16)],
        compiler_params=_cparams("parallel"),
        name="mix_out",
    )(attn, pool, lru, x, g_attn, g_pool, g_lru, w_out, g_post, gate, g_mlp, scale2, shift2, w_up)


def _mlp_down_kernel(u_ref, w_ref, x_ref, gpost_ref, gate_ref, o_ref):
    y = jnp.dot(u_ref[...], w_ref[...], preferred_element_type=F32)
    o_ref[...] = x_ref[...] + gate_ref[...] * _rms(y, gpost_ref[...])


def _mlp_down(u2, w_down, x, g_post, gate):
    tm = 256
    vec = pl.BlockSpec((1, D_MODEL), lambda i: (0, 0))
    return pl.pallas_call(
        _mlp_down_kernel,
        grid=(SEQ // tm,),
        in_specs=[pl.BlockSpec((tm, MLP_HIDDEN), lambda i: (i, 0)),
                  pl.BlockSpec((MLP_HIDDEN, D_MODEL), lambda i: (0, 0), pipeline_mode=pl.Buffered(1)),
                  pl.BlockSpec((tm, D_MODEL), lambda i: (i, 0)),
                  vec, vec],
        out_specs=pl.BlockSpec((tm, D_MODEL), lambda i: (i, 0)),
        out_shape=jax.ShapeDtypeStruct((SEQ, D_MODEL), F32),
        compiler_params=_cparams("parallel"),
        name="mlp_down",
    )(u2, w_down, x, g_post, gate)


def kernel(x, c, positions, rel_bias, w_mod, b_mod, g_pre_mix, g_post_mix, g_pre_mlp, g_post_mlp, w_in, pool_w, pool_scale, conv_w, conv_b, gate_r_w, gate_r_b, gate_i_w, gate_i_b, lru_lambda, g_attn_out, g_pool_out, g_lru_out, w_out, w_mlp_up, w_mlp_down):
    assert x.shape == (1, SEQ, D_MODEL) and positions.shape == (1, SEQ)
    del positions
    xs = x[0]
    mod = _modulation(c, w_mod, b_mod)
    band = _bias_band(rel_bias)
    row = lambda v: v.reshape(1, -1)
    n_attn_cols = 4 * ATTN_WIDTH
    aux_lo = n_attn_cols + IDX_DIM + N_IDX_HEADS
    w_in_t = jnp.swapaxes(w_in, 1, 2).astype(BF16)
    w_aux = jnp.concatenate(
        [w_in_t[:, aux_lo:, :], w_in_t[:, n_attn_cols:aux_lo, :],
         jnp.zeros((DEPTH, AUX_WIDTH - (AUX_KIDX + IDX_DIM + N_IDX_HEADS), D_MODEL), BF16)], axis=1)
    w_out_b = w_out.astype(BF16)
    for l in range(DEPTH):
        shift1, scale1, gate1, shift2, scale2, gate2 = [
            mod[l, :, n * D_MODEL:(n + 1) * D_MODEL] for n in range(6)]
        g_pre = row(g_pre_mix[l])
        attn_grp, aux = _in_projection(xs, g_pre, scale1, shift1, w_in_t, n_attn_cols, w_aux, l)
        words = _select(attn_grp, aux)
        attn = _attention(attn_grp, words, band)
        pool, lru = _pool_lru(aux, pool_w[l], pool_scale[l], conv_w[l], conv_b[l], gate_r_w[l],
                              gate_r_b[l], gate_i_w[l], gate_i_b[l], lru_lambda[l])
        xs, h_mlp, w_up_b = _mix_out(attn, pool, lru, xs, row(g_attn_out[l]), row(g_pool_out[l]),
                                     row(g_lru_out[l]), w_out_b, l, row(g_post_mix[l]), gate1,
                                     row(g_pre_mlp[l]), scale2, shift2, w_mlp_up)
        u2, w_down_b = _mlp_up(h_mlp, w_up_b, w_mlp_down, l)
        xs = _mlp_down(u2, w_down_b, xs, row(g_post_mlp[l]), gate2)
    return xs[None]
```

```python
import functools
import math

import jax
import jax.numpy as jnp
from jax import lax
from jax.experimental import pallas as pl
from jax.experimental.pallas import tpu as pltpu

D_MODEL = 2048
SEQ = 8192
DEPTH = 2
ATTN_HEAD_DIM = 128
ATTN_WIDTH = 1024
N_ATTN_HEADS = 8
POOL_WIDTH = 512
POOL_WINDOWS = (2, 4, 8, 16)
POOL_GROUP = 128
LRU_WIDTH = 512
N_LRU_BLOCKS = 4
LRU_BLOCK = 128
LRU_C = 8.0
CONV_WIDTH = 4
N_IDX_HEADS = 16
IDX_DIM = 64
TOP_K = 256
N_BUCKETS = 32
MAX_DISTANCE = 128
MLP_HIDDEN = 4 * D_MODEL
NORM_EPS = 1e-6

AUX_POOL = 0
AUX_LRU_IN = 512
AUX_LRU_GATE = 1024
AUX_KIDX = 1536
AUX_WIDTH = 1664
LANE = 128
SUBLANE = 8

SEL_T = 256
SEL_GROUP = 2
SEL_SCORE_GROUP = 4
ATT_TQ = 512
ATT_TK = SEL_T
ATT_UNROLL = 4
N_BAND = ATT_TQ // ATT_TK + 2
assert MAX_DISTANCE <= ATT_TK // 2 and SEQ // SEL_T <= 32 and SEL_T >= TOP_K
MASK_NEG = -1e30
LOG2E = math.log2(math.e)
VMEM_LIMIT = 56 * 1024 * 1024

F32 = jnp.float32
BF16 = jnp.bfloat16
_NT = (((1,), (1,)), ((), ()))


def _cparams(*sem):
    return pltpu.CompilerParams(dimension_semantics=sem, vmem_limit_bytes=VMEM_LIMIT)


def _rms(x, g):
    ms = jnp.mean(x * x, axis=-1, keepdims=True)
    return (x * lax.rsqrt(ms + NORM_EPS)) * g


def _mod_kernel(c_ref, w_ref, b_ref, o_ref):
    c = c_ref[...]
    c_act = c * jax.nn.sigmoid(c)
    o_ref[0] = jnp.sum(c_act * w_ref[0], axis=0, keepdims=True) + b_ref[0]


def _modulation(c, w_mod, b_mod):
    tn = 1024
    n = w_mod.shape[-1]
    return pl.pallas_call(
        _mod_kernel,
        grid=(DEPTH, n // tn),
        in_specs=[
            pl.BlockSpec((D_MODEL, 1), lambda l, j: (0, 0)),
            pl.BlockSpec((1, D_MODEL, tn), lambda l, j: (l, 0, j)),
            pl.BlockSpec((1, 1, tn), lambda l, j: (l, 0, j)),
        ],
        out_specs=pl.BlockSpec((1, 1, tn), lambda l, j: (l, 0, j)),
        out_shape=jax.ShapeDtypeStruct((DEPTH, 1, n), F32),
        compiler_params=_cparams("parallel", "parallel"),
        name="modulation",
    )(c.reshape(D_MODEL, 1), w_mod, b_mod.reshape(DEPTH, 1, n))


def _mlp_up_kernel(h_ref, w_ref, wd_ref, o_ref, wdb_ref):
    u = jnp.maximum(jnp.dot(h_ref[...], w_ref[...], preferred_element_type=F32), 0.0)
    o_ref[...] = (u * u).astype(o_ref.dtype)
    wdb_ref[...] = wd_ref[...].astype(BF16)


def _mlp_up(h, w_up_b, w_down, layer):
    tm = 512
    m, k = h.shape
    n = w_up_b.shape[1]
    halves = 2
    tn = n // halves
    steps_m = m // tm
    slab = w_down.shape[1] // (halves * steps_m)
    return pl.pallas_call(
        _mlp_up_kernel,
        grid=(halves, steps_m),
        in_specs=[
            pl.BlockSpec((tm, k), lambda j, i: (i, 0)),
            pl.BlockSpec((k, tn), lambda j, i: (0, j), pipeline_mode=pl.Buffered(1)),
            pl.BlockSpec((None, slab, w_down.shape[2]), lambda j, i: (layer, j * steps_m + i, 0)),
        ],
        out_specs=[pl.BlockSpec((tm, tn), lambda j, i: (i, j)),
                   pl.BlockSpec((slab, w_down.shape[2]), lambda j, i: (j * steps_m + i, 0))],
        out_shape=[jax.ShapeDtypeStruct((m, n), BF16), jax.ShapeDtypeStruct(w_down.shape[1:], BF16)],
        compiler_params=_cparams("parallel", "parallel"),
        name="mlp_up",
    )(h, w_up_b, w_down)


def _proj_kernel(x_ref, g_ref, sc_ref, sh_ref, wa_ref, wx_ref, oa_ref, ox_ref):
    h = (_rms(x_ref[...], g_ref[...]) * (1.0 + sc_ref[...]) + sh_ref[...]).astype(BF16)
    oa_ref[...] = lax.dot_general(h, wa_ref[...], _NT, preferred_element_type=F32).astype(oa_ref.dtype)
    ox_ref[...] = lax.dot_general(h, wx_ref[...], _NT, preferred_element_type=F32)


def _in_projection(x, g, scale, shift, w_attn_t, na, w_aux_t, layer):
    tm = 512
    m, k = x.shape
    nx = w_aux_t.shape[1]
    vec = pl.BlockSpec((1, k), lambda i: (0, 0))
    resident = lambda n: pl.BlockSpec((None, n, k), lambda i: (layer, 0, 0), pipeline_mode=pl.Buffered(1))
    return pl.pallas_call(
        _proj_kernel,
        grid=(m // tm,),
        in_specs=[pl.BlockSpec((tm, k), lambda i: (i, 0)), vec, vec, vec, resident(na), resident(nx)],
        out_specs=[pl.BlockSpec((tm, na), lambda i: (i, 0)), pl.BlockSpec((tm, nx), lambda i: (i, 0))],
        out_shape=[jax.ShapeDtypeStruct((m, na), BF16), jax.ShapeDtypeStruct((m, nx), F32)],
        compiler_params=_cparams("parallel"),
        name="in_proj",
    )(x, g, scale, shift, w_attn_t, w_aux_t)


def _band_kernel(rb_ref, o_ref):
    d = pl.program_id(0)
    t = lax.broadcasted_iota(jnp.int32, (ATT_TQ, ATT_TK), 0)
    s = lax.broadcasted_iota(jnp.int32, (ATT_TQ, ATT_TK), 1)
    rel = jnp.maximum((d - (ATT_TQ // ATT_TK - 1)) * ATT_TK + t - s, 0)
    max_exact = N_BUCKETS // 2
    relf = jnp.maximum(rel.astype(F32), 1.0)
    large = max_exact + (jnp.log(relf / max_exact) / math.log(MAX_DISTANCE / max_exact)
                         * (N_BUCKETS - max_exact)).astype(jnp.int32)
    large = jnp.minimum(large, N_BUCKETS - 1)
    bucket = jnp.where(rel < max_exact, rel, large)
    for h in range(N_ATTN_HEADS):
        table = jnp.broadcast_to(rb_ref[h:h + 1, :], (ATT_TQ, LANE))
        val = jnp.concatenate(
            [jnp.take_along_axis(table, bucket[:, cc * LANE:(cc + 1) * LANE], axis=1)
             for cc in range(ATT_TK // LANE)], axis=1)
        o_ref[0, h] = val * LOG2E


def _bias_band(rel_bias):
    return pl.pallas_call(
        _band_kernel,
        grid=(N_BAND,),
        in_specs=[pl.BlockSpec((N_ATTN_HEADS, LANE), lambda d: (0, 0))],
        out_specs=pl.BlockSpec((1, N_ATTN_HEADS, ATT_TQ, ATT_TK), lambda d: (d, 0, 0, 0)),
        out_shape=jax.ShapeDtypeStruct((N_BAND, N_ATTN_HEADS, ATT_TQ, ATT_TK), F32),
        compiler_params=_cparams("parallel"),
        name="bias_band",
    )(jnp.pad(rel_bias.T, ((0, 0), (0, LANE - N_BUCKETS))))


def _float_to_key(x):
    bits = lax.bitcast_convert_type(x, jnp.int32)
    return jnp.where(bits >= 0, bits, jnp.bitwise_xor(bits, jnp.int32(0x7FFFFFFF)))


_KEY_POS_INF = 0x7F800000
_KEY_NEG_INF = 0x807FFFFF - (1 << 32)


def _key_to_float(key):
    bits = jnp.where(key >= 0, key, jnp.bitwise_xor(key, jnp.int32(0x7FFFFFFF)))
    return lax.bitcast_convert_type(bits, F32)


def _select_kernel(qi_ref, kw_ref, o_ref, kb_ref, sc_ref, hi_ref, lo_ref):
    i = pl.program_id(0)
    tq = tk = SEL_T
    idx_scale = (N_IDX_HEADS * IDX_DIM) ** -0.5

    @pl.when(i == 0)
    def _():
        kb_ref[...] = kw_ref[:, 0:IDX_DIM].astype(BF16)

    w_t = jnp.transpose(kw_ref[pl.ds(pl.multiple_of(i * tq, tq), tq), :])

    nchunks = i + 1
    key_pos = lax.broadcasted_iota(jnp.int32, (tk, tq), 0)
    qry_pos = i * tq + lax.broadcasted_iota(jnp.int32, (tk, tq), 1)
    grp = SEL_GROUP

    def high_half(key):
        return lax.shift_right_arithmetic(key, jnp.int32(16))

    def score_tiles(j0, ntiles, below_diagonal, gmax):
        kc = kb_ref[pl.ds(pl.multiple_of(j0 * tk, tk), ntiles * tk), :]
        acc = jnp.zeros((ntiles * tk, tq), F32)
        for h in range(N_IDX_HEADS):
            d = lax.dot_general(kc, qi_ref[:, h * IDX_DIM:(h + 1) * IDX_DIM], _NT,
                                preferred_element_type=F32)
            acc = acc + w_t[IDX_DIM + h:IDX_DIM + h + 1, :] * jnp.maximum(d, 0.0)
        for u in range(ntiles):
            j = j0 + u
            score = acc[u * tk:(u + 1) * tk] * idx_scale
            if not below_diagonal:
                score = jnp.where(key_pos + j * tk <= qry_pos, score, -jnp.inf)
            sc_ref[j] = score
            gmax = jnp.maximum(gmax, score)
            key = _float_to_key(score)
            hi_ref[j] = high_half(key).astype(jnp.int16)
            lo_ref[j] = (jnp.bitwise_and(key, jnp.int32(0xFFFF)) - 32768).astype(jnp.int16)
        return gmax

    ngroups = (nchunks + grp - 1) // grp
    big = SEL_SCORE_GROUP
    nbig = i // big
    gmax = jnp.full((tk, tq), -jnp.inf, F32)
    first = 0
    for step in (4 * big, 2 * big, big):
        stop = i // step
        gmax = lax.fori_loop(first, stop, lambda jg, gm, step=step: score_tiles(step * jg, step, True, gm), gmax)
        first = 2 * stop
    gmax = lax.fori_loop(nbig * (big // grp), ngroups,
                         lambda jg, gm: score_tiles(grp * jg, grp, False, gm), gmax)
    t_min = high_half(_float_to_key(jnp.min(gmax, axis=0, keepdims=True)))
    t_max = high_half(_float_to_key(jnp.max(gmax, axis=0, keepdims=True)))

    i16 = jnp.int16
    pack = 16
    one16 = jnp.ones((pack, tq), i16)
    zero16 = jnp.zeros((pack, tq), i16)

    def count_ge(ref, cand):
        cand_b = jnp.broadcast_to(cand, (pack, tq)).astype(i16)

        def body(jg, acc):
            parts = [acc, zero16, zero16, zero16]
            n = 0
            for u in range(grp):
                x = ref[grp * jg + u]
                for g in range(tk // pack):
                    hit = jnp.where(x[g * pack:(g + 1) * pack, :] >= cand_b, one16, zero16)
                    parts[n % 4] = parts[n % 4] + hit
                    n += 1
            return (parts[0] + parts[1]) + (parts[2] + parts[3])

        acc = lax.fori_loop(0, ngroups, body, zero16)
        return jnp.sum(acc.astype(jnp.int32).astype(F32), axis=0, keepdims=True)

    def kth_largest(ref, rank, lo0, hi0, stop_on_hit):
        def unsettled(carry):
            lo, hi, _ = carry
            return jnp.max((hi - lo).astype(F32)) > 0.0

        def halve(carry):
            lo, hi, above = carry
            mid = lo + lax.shift_right_logical(hi - lo + 1, jnp.int32(1))
            cnt = count_ge(ref, mid)
            ok = cnt >= rank
            lo, hi, above = jnp.where(ok, mid, lo), jnp.where(ok, hi, mid - 1), jnp.where(ok, above, cnt)
            if stop_on_hit:
                hit = cnt == rank
                lo, hi = jnp.where(hit, mid, lo), jnp.where(hit, mid, hi)
            return lo, hi, above

        def body(carry):
            return halve(halve(carry))

        lo, _, above = lax.while_loop(unsettled, body, (lo0, hi0, jnp.zeros((1, tq), F32)))
        return lo, above

    t_hi, above = kth_largest(hi_ref, jnp.full((1, tq), float(TOP_K), F32), t_min, t_max, False)
    t_hi_b = jnp.broadcast_to(t_hi, (tk, tq)).astype(i16)

    def narrow_body(j, carry):
        lo_ref[j] = jnp.where(hi_ref[j] == t_hi_b, lo_ref[j], jnp.full((tk, tq), -32768, i16))
        return carry

    lax.fori_loop(0, grp * ngroups, narrow_body, 0)
    t_lo, _ = kth_largest(lo_ref, float(TOP_K) - above, jnp.full((1, tq), -32768, jnp.int32),
                          jnp.full((1, tq), 32767, jnp.int32), True)
    key = jnp.bitwise_or(jnp.left_shift(t_hi, 16), t_lo + 32768)
    key = jnp.clip(key, _KEY_NEG_INF, _KEY_POS_INF)
    thr = jnp.broadcast_to(_key_to_float(key), (tk, tq))

    def word_body(j, word):
        return jnp.bitwise_or(word, jnp.where(sc_ref[j] >= thr, jnp.left_shift(jnp.int32(1), j), 0))

    word_t = lax.fori_loop(0, i, word_body, jnp.zeros((tk, tq), jnp.int32))
    keep = jnp.logical_and(sc_ref[i] >= thr, key_pos + i * tk <= qry_pos)
    word_t = jnp.bitwise_or(word_t, jnp.where(keep, jnp.left_shift(jnp.int32(1), i), 0))
    o_ref[...] = jnp.transpose(word_t)


def _select(attn_grp, aux):
    tq = tk = SEL_T
    nq, nk = SEQ // tq, SEQ // tk
    return pl.pallas_call(
        _select_kernel,
        grid=(nq,),
        in_specs=[
            pl.BlockSpec((tq, N_IDX_HEADS * IDX_DIM), lambda i: (i, 3)),
            pl.BlockSpec((SEQ, LANE), lambda i: (0, AUX_KIDX // LANE)),
        ],
        out_specs=pl.BlockSpec((tq, tk), lambda i: (i, 0)),
        out_shape=jax.ShapeDtypeStruct((SEQ, tk), jnp.int32),
        scratch_shapes=[
            pltpu.VMEM((SEQ, IDX_DIM), BF16),
            pltpu.VMEM((nk, tk, tq), F32),
            pltpu.VMEM((nk, tk, tq), jnp.int16),
            pltpu.VMEM((nk, tk, tq), jnp.int16),
        ],
        compiler_params=_cparams("arbitrary"),
        name="select",
    )(attn_grp, aux)


def _attn_kernel(q_ref, k_ref, v_ref, w_ref, band_ref, o_ref, mask_ref, sa_ref, sb_ref, acc_ref):
    i = pl.program_id(0)
    tq, tk, unroll = ATT_TQ, ATT_TK, ATT_UNROLL
    dh = ATTN_HEAD_DIM
    c = (ATTN_HEAD_DIM ** -0.5) * LOG2E
    r = tq // tk
    ngrp = (r * (i + 1) + unroll - 1) // unroll
    ones = jnp.ones((unroll * tk, LANE), BF16)

    @pl.when(pl.program_id(1) == 0)
    def _():
        def expand(j, carry):
            keep = jnp.bitwise_and(w_ref[...], jnp.left_shift(jnp.int32(1), j)) != 0
            mask_ref[j] = jnp.where(keep, 0.0, MASK_NEG)
            return carry

        lax.fori_loop(0, unroll * ngrp, expand, 0)

    def logits(hh, g, s_ref, m_prev):
        hs = slice(hh * dh, (hh + 1) * dh)
        mx = jnp.full((tq, LANE), MASK_NEG, F32)
        for u in range(unroll):
            j = unroll * g + u
            kt = k_ref[pl.ds(pl.multiple_of(j * tk, tk), tk), hs]
            s2 = lax.dot_general(q_ref[:, hs], kt, _NT, preferred_element_type=F32) * c
            s2 = s2 + band_ref[jnp.clip(r * i + (r - 1) - j, 0, N_BAND - 1), hh] + mask_ref[j]
            s_ref[u] = s2
            for cc in range(tk // LANE):
                mx = jnp.maximum(mx, s2[:, cc * LANE:(cc + 1) * LANE])
        return jnp.maximum(m_prev, jnp.max(mx, axis=1, keepdims=True))

    def values(hh, g, s_ref, m_prev, m_cur):
        m_b = jnp.broadcast_to(m_cur, (tq, tk))
        p = jnp.concatenate([jnp.exp2(s_ref[u] - m_b).astype(BF16) for u in range(unroll)], axis=1)
        vt = v_ref[pl.ds(pl.multiple_of(g * (unroll * tk), unroll * tk), unroll * tk), hh * dh:(hh + 1) * dh]
        pv = jnp.dot(p, jnp.concatenate([vt, ones], axis=1), preferred_element_type=F32)
        acc_ref[hh] = acc_ref[hh] * jnp.exp2(m_prev - m_cur) + pv

    acc_ref[...] = jnp.zeros(acc_ref.shape, F32)
    m_none = jnp.full((tq, 1), MASK_NEG, F32)
    npair = (ngrp - 1) // 2
    last = 2 * npair

    def steady(hh, first, second, m_0):
        def pair(t, carry):
            m_prev, m_cur = carry
            m_1 = logits(hh, 2 * t + 1, second, m_cur)
            values(hh, 2 * t, first, m_prev, m_cur)
            m_2 = logits(hh, 2 * t + 2, first, m_1)
            values(hh, 2 * t + 1, second, m_cur, m_1)
            return m_1, m_2

        return lax.fori_loop(0, npair, pair, (m_none, m_0))

    @pl.when(ngrp - 1 == last)
    def _():
        m_prev, m_cur = steady(0, sa_ref, sb_ref, logits(0, 0, sa_ref, m_none))
        m_0 = logits(1, 0, sb_ref, m_none)
        values(0, last, sa_ref, m_prev, m_cur)
        m_prev, m_cur = steady(1, sb_ref, sa_ref, m_0)
        values(1, last, sb_ref, m_prev, m_cur)

    @pl.when(ngrp - 1 > last)
    def _():
        m_prev, m_cur = steady(0, sa_ref, sb_ref, logits(0, 0, sa_ref, m_none))
        m_1 = logits(0, last + 1, sb_ref, m_cur)
        values(0, last, sa_ref, m_prev, m_cur)
        m_0 = logits(1, 0, sa_ref, m_none)
        values(0, last + 1, sb_ref, m_cur, m_1)
        m_prev, m_cur = steady(1, sa_ref, sb_ref, m_0)
        m_1 = logits(1, last + 1, sb_ref, m_cur)
        values(1, last, sa_ref, m_prev, m_cur)
        values(1, last + 1, sb_ref, m_cur, m_1)

    for hh in range(2):
        o_ref[:, hh * dh:(hh + 1) * dh] = acc_ref[hh, :, :dh] / acc_ref[hh, :, dh:dh + 1]


def _attention(attn_grp, words, band):
    tq, tk = ATT_TQ, ATT_TK
    nq, nk = SEQ // tq, SEQ // tk
    dh2 = 2 * ATTN_HEAD_DIM
    npairs = N_ATTN_HEADS // 2
    return pl.pallas_call(
        _attn_kernel,
        grid=(nq, npairs),
        in_specs=[
            pl.BlockSpec((tq, dh2), lambda i, h: (i, h)),
            pl.BlockSpec((SEQ, dh2), lambda i, h: (0, npairs + h)),
            pl.BlockSpec((SEQ, dh2), lambda i, h: (0, 2 * npairs + h)),
            pl.BlockSpec((tq, tk), lambda i, h: (i, 0)),
            pl.BlockSpec((N_BAND, 2, tq, tk), lambda i, h: (0, h, 0, 0)),
        ],
        out_specs=pl.BlockSpec((tq, dh2), lambda i, h: (i, h)),
        out_shape=jax.ShapeDtypeStruct((SEQ, ATTN_WIDTH), F32),
        scratch_shapes=[pltpu.VMEM((nk, tq, tk), F32),
                        pltpu.VMEM((ATT_UNROLL, tq, tk), F32), pltpu.VMEM((ATT_UNROLL, tq, tk), F32),
                        pltpu.VMEM((2, tq, ATTN_HEAD_DIM + LANE), F32)],
        compiler_params=_cparams("parallel", "arbitrary"),
        name="attention",
    )(attn_grp, attn_grp, attn_grp, words, band)


POOL_HALO = 16
CONV_HALO = 8


def _block_diag_dot(x, w_ref):
    outs = []
    for g in range(w_ref.shape[0]):
        xs = x[:, g * LRU_BLOCK:(g + 1) * LRU_BLOCK].astype(BF16)
        outs.append(jnp.dot(xs, w_ref[g], preferred_element_type=F32))
    return jnp.concatenate(outs, axis=1)


def _pool_lru_kernel(pin_ref, lin_ref, gate_ref, pw_ref, ps_ref, cw_ref, cb_ref, rw_ref, rb_ref,
                     iw_ref, ib_ref, lam_ref, pool_o, lru_o, pprev_ref, lprev_ref, h_ref, *, T):
    i = pl.program_id(0)

    @pl.when(i == 0)
    def _():
        pprev_ref[...] = jnp.zeros(pprev_ref.shape, F32)
        lprev_ref[...] = jnp.zeros(lprev_ref.shape, F32)
        h_ref[...] = jnp.zeros(h_ref.shape, F32)

    t_glob = i * T + lax.broadcasted_iota(jnp.int32, (T, 1), 0)

    u = pin_ref[...]
    ext = jnp.concatenate([pprev_ref[...], u], axis=0)
    pprev_ref[...] = u[T - POOL_HALO:, :]
    p2 = ext[1:] + ext[:-1]
    p4 = p2[2:, POOL_GROUP:] + p2[:-2, POOL_GROUP:]
    p8 = p4[4:, POOL_GROUP:] + p4[:-4, POOL_GROUP:]
    p16 = p8[8:, POOL_GROUP:] + p8[:-8, POOL_GROUP:]
    wsums = (p2[15:15 + T, :POOL_GROUP], p4[13:13 + T, :POOL_GROUP],
             p8[9:9 + T, :POOL_GROUP], p16[1:1 + T, :])
    for g, win in enumerate(POOL_WINDOWS):
        gs = slice(g * POOL_GROUP, (g + 1) * POOL_GROUP)
        count = jnp.minimum(t_glob + 1, win).astype(F32)
        dlt = wsums[g] / count - u[:, gs]
        y = jnp.dot(dlt.astype(BF16), pw_ref[g], preferred_element_type=F32)
        pool_o[:, gs] = y * ps_ref[:, gs]

    x = lin_ref[...]
    lext = jnp.concatenate([lprev_ref[...], x], axis=0)
    lprev_ref[...] = x[T - CONV_HALO:, :]
    xc = cb_ref[...]
    for jj in range(CONV_WIDTH):
        off = CONV_HALO - (CONV_WIDTH - 1) + jj
        xc = xc + cw_ref[jj:jj + 1, :] * lext[off:off + T, :]
    r = jax.nn.sigmoid(_block_diag_dot(xc, rw_ref) + rb_ref[...])
    ig = jax.nn.sigmoid(_block_diag_dot(xc, iw_ref) + ib_ref[...])
    z = -lam_ref[...]
    softplus = jnp.maximum(z, 0.0) + jnp.log1p(jnp.exp(-jnp.abs(z)))
    log_a = (-LRU_C * r) * softplus
    a = jnp.exp(log_a)
    b = jnp.sqrt(-jnp.tanh(log_a) * (a * a + 1.0)) * (ig * xc)

    ngrp = T // SUBLANE
    a3 = a.reshape(ngrp, SUBLANE, LRU_WIDTH)
    b3 = b.reshape(ngrp, SUBLANE, LRU_WIDTH)
    sub = lax.broadcasted_iota(jnp.int32, a3.shape, 1)
    sh = 1
    while sh < SUBLANE:
        a_sh = jnp.where(sub >= sh, pltpu.roll(a3, sh, axis=1), 1.0)
        b_sh = jnp.where(sub >= sh, pltpu.roll(b3, sh, axis=1), 0.0)
        b3 = a3 * b_sh + b3
        a3 = a3 * a_sh
        sh *= 2
    carry = h_ref[...]
    rows = []
    for grp in range(ngrp):
        hg = a3[grp] * carry + b3[grp]
        rows.append(hg)
        carry = hg[SUBLANE - 1:, :]
    h = jnp.concatenate(rows, axis=0)
    h_ref[...] = carry

    gt = gate_ref[...]
    cdf = 0.5 * (1.0 + jnp.tanh(math.sqrt(2.0 / math.pi) * (gt + 0.044715 * (gt * gt * gt))))
    lru_o[...] = h * (gt * cdf)


def _pool_lru(aux, pool_w, pool_scale, conv_w, conv_b, r_w, r_b, i_w, i_b, lam):
    T = 256
    full = lambda a: pl.BlockSpec(a.shape, lambda i: (0,) * a.ndim)
    row = lambda v: v.reshape(1, -1)
    args = (pool_w.astype(BF16), row(pool_scale), conv_w, row(conv_b), r_w.astype(BF16), row(r_b),
            i_w.astype(BF16), row(i_b), row(lam))
    return pl.pallas_call(
        functools.partial(_pool_lru_kernel, T=T),
        grid=(SEQ // T,),
        in_specs=[
            pl.BlockSpec((T, POOL_WIDTH), lambda i: (i, AUX_POOL // POOL_WIDTH)),
            pl.BlockSpec((T, LRU_WIDTH), lambda i: (i, AUX_LRU_IN // LRU_WIDTH)),
            pl.BlockSpec((T, LRU_WIDTH), lambda i: (i, AUX_LRU_GATE // LRU_WIDTH)),
        ] + [full(a) for a in args],
        out_specs=[pl.BlockSpec((T, POOL_WIDTH), lambda i: (i, 0)),
                   pl.BlockSpec((T, LRU_WIDTH), lambda i: (i, 0))],
        out_shape=[jax.ShapeDtypeStruct((SEQ, POOL_WIDTH), F32),
                   jax.ShapeDtypeStruct((SEQ, LRU_WIDTH), F32)],
        scratch_shapes=[pltpu.VMEM((POOL_HALO, POOL_WIDTH), F32),
                        pltpu.VMEM((CONV_HALO, LRU_WIDTH), F32),
                        pltpu.VMEM((1, LRU_WIDTH), F32)],
        compiler_params=_cparams("arbitrary"),
        name="pool_lru",
    )(aux, aux, aux, *args)


def _mix_out_kernel(a_ref, p_ref, l_ref, x_ref, ga_ref, gp_ref, gl_ref, w_ref, gpost_ref, gate_ref,
                    gmlp_ref, sc_ref, sh_ref, wup_ref, o_ref, h_ref, wupb_ref):
    na = _rms(a_ref[...], ga_ref[...]).astype(BF16)
    npool = _rms(p_ref[...], gp_ref[...]).astype(BF16)
    nl = _rms(l_ref[...], gl_ref[...]).astype(BF16)
    p0, p1 = ATTN_WIDTH, ATTN_WIDTH + POOL_WIDTH
    y = jnp.dot(na, w_ref[0:p0, :], preferred_element_type=F32)
    y = y + jnp.dot(npool, w_ref[p0:p1, :], preferred_element_type=F32)
    y = y + jnp.dot(nl, w_ref[p1:, :], preferred_element_type=F32)
    x_new = x_ref[...] + gate_ref[...] * _rms(y, gpost_ref[...])
    o_ref[...] = x_new
    h_ref[...] = (_rms(x_new, gmlp_ref[...]) * (1.0 + sc_ref[...]) + sh_ref[...]).astype(BF16)
    wupb_ref[...] = wup_ref[...].astype(BF16)


def _mix_out(attn, pool, lru, x, g_attn, g_pool, g_lru, w_out, layer, g_post, gate, g_mlp, scale2, shift2,
             w_up):
    tm = 512
    steps = SEQ // tm
    slab = w_up.shape[1] // steps
    vec = lambda n: pl.BlockSpec((1, n), lambda i: (0, 0))
    rows = lambda n: pl.BlockSpec((tm, n), lambda i: (i, 0))
    return pl.pallas_call(
        _mix_out_kernel,
        grid=(steps,),
        in_specs=[rows(ATTN_WIDTH), rows(POOL_WIDTH), rows(LRU_WIDTH), rows(D_MODEL),
                  vec(ATTN_WIDTH), vec(POOL_WIDTH), vec(LRU_WIDTH),
                  pl.BlockSpec((None, D_MODEL, D_MODEL), lambda i: (layer, 0, 0)),
                  vec(D_MODEL), vec(D_MODEL), vec(D_MODEL), vec(D_MODEL), vec(D_MODEL),
                  pl.BlockSpec((None, slab, w_up.shape[2]), lambda i: (layer, i, 0))],
        out_specs=[rows(D_MODEL), rows(D_MODEL), pl.BlockSpec((slab, w_up.shape[2]), lambda i: (i, 0))],
        out_shape=[jax.ShapeDtypeStruct((SEQ, D_MODEL), F32), jax.ShapeDtypeStruct((SEQ, D_MODEL), BF16),
                   jax.ShapeDtypeStruct(w_up.shape[1:], BF16)],
        compiler_params=_cparams("parallel"),
        name="mix_out",
    )(attn, pool, lru, x, g_attn, g_pool, g_lru, w_out, g_post, gate, g_mlp, scale2, shift2, w_up)


def _mlp_down_kernel(u_ref, w_ref, x_ref, gpost_ref, gate_ref, o_ref):
    y = jnp.dot(u_ref[...], w_ref[...], preferred_element_type=F32)
    o_ref[...] = x_ref[...] + gate_ref[...] * _rms(y, gpost_ref[...])


def _mlp_down(u2, w_down, x, g_post, gate):
    tm = 256
    vec = pl.BlockSpec((1, D_MODEL), lambda i: (0, 0))
    return pl.pallas_call(
        _mlp_down_kernel,
        grid=(SEQ // tm,),
        in_specs=[pl.BlockSpec((tm, MLP_HIDDEN), lambda i: (i, 0)),
                  pl.BlockSpec((MLP_HIDDEN, D_MODEL), lambda i: (0, 0), pipeline_mode=pl.Buffered(1)),
                  pl.BlockSpec((tm, D_MODEL), lambda i: (i, 0)),
                  vec, vec],
        out_specs=pl.BlockSpec((tm, D_MODEL), lambda i: (i, 0)),
        out_shape=jax.ShapeDtypeStruct((SEQ, D_MODEL), F32),
        compiler_params=_cparams("parallel"),
        name="mlp_down",
    )(u2, w_down, x, g_post, gate)


def kernel(x, c, positions, rel_bias, w_mod, b_mod, g_pre_mix, g_post_mix, g_pre_mlp, g_post_mlp, w_in, pool_w, pool_scale, conv_w, conv_b, gate_r_w, gate_r_b, gate_i_w, gate_i_b, lru_lambda, g_attn_out, g_pool_out, g_lru_out, w_out, w_mlp_up, w_mlp_down):
    assert x.shape == (1, SEQ, D_MODEL) and positions.shape == (1, SEQ)
    del positions
    xs = x[0]
    mod = _modulation(c, w_mod, b_mod)
    band = _bias_band(rel_bias)
    row = lambda v: v.reshape(1, -1)
    n_attn_cols = 4 * ATTN_WIDTH
    aux_lo = n_attn_cols + IDX_DIM + N_IDX_HEADS
    w_in_t = jnp.swapaxes(w_in, 1, 2).astype(BF16)
    w_aux = jnp.concatenate(
        [w_in_t[:, aux_lo:, :], w_in_t[:, n_attn_cols:aux_lo, :],
         jnp.zeros((DEPTH, AUX_WIDTH - (AUX_KIDX + IDX_DIM + N_IDX_HEADS), D_MODEL), BF16)], axis=1)
    w_out_b = w_out.astype(BF16)
    for l in range(DEPTH):
        shift1, scale1, gate1, shift2, scale2, gate2 = [
            mod[l, :, n * D_MODEL:(n + 1) * D_MODEL] for n in range(6)]
        g_pre = row(g_pre_mix[l])
        attn_grp, aux = _in_projection(xs, g_pre, scale1, shift1, w_in_t, n_attn_cols, w_aux, l)
        words = _select(attn_grp, aux)
        attn = _attention(attn_grp, words, band)
        pool, lru = _pool_lru(aux, pool_w[l], pool_scale[l], conv_w[l], conv_b[l], gate_r_w[l],
                              gate_r_b[l], gate_i_w[l], gate_i_b[l], lru_lambda[l])
        xs, h_mlp, w_up_b = _mix_out(attn, pool, lru, xs, row(g_attn_out[l]), row(g_pool_out[l]),
                                     row(g_lru_out[l]), w_out_b, l, row(g_post_mix[l]), gate1,
                                     row(g_pre_mlp[l]), scale2, shift2, w_mlp_up)
        u2, w_down_b = _mlp_up(h_mlp, w_up_b, w_mlp_down, l)
        xs = _mlp_down(u2, w_down_b, xs, row(g_post_mlp[l]), gate2)
    return xs[None]
```
